```python
import jax
import jax.numpy as jnp
from jax import lax
import numpy as np

D_MODEL = 2048
BATCH = 4
SEQ = 2048
DEPTH = 2

HEAD_DIM = 128
Q_BLOCK = 128
EPS = 1e-6
SB_WIDTH = D_MODEL // 2
SB_HEADS = SB_WIDTH // HEAD_DIM
RG_WIDTH = D_MODEL // 2
RG_BLOCK = 64
RG_BLOCKS = RG_WIDTH // RG_BLOCK
CONV_WIDTH = 4
RG_C = 8.0
EVEN_IN = 3 * SB_WIDTH + 2 * RG_WIDTH
EVEN_MIX = SB_WIDTH + RG_WIDTH
NSA_HEADS = D_MODEL // HEAD_DIM
NSA_KV_HEADS = 4
NSA_GROUP = NSA_HEADS // NSA_KV_HEADS
NSA_WIDTH = NSA_HEADS * HEAD_DIM
KV_WIDTH = NSA_KV_HEADS * HEAD_DIM
N_BRANCH = 3
NSA_IN = NSA_WIDTH + 6 * KV_WIDTH + N_BRANCH * NSA_HEADS
CMP_BLOCK = 32
CMP_STRIDE = 16
SLC_BLOCK = 64
N_SEL = 8
WINDOW = 512
FORCE_BONUS = 1e6
D_FF = 7 * D_MODEL // 2
N_EXPERTS = 8
TOP_K = 2
N_EVEN = (DEPTH + 1) // 2
N_ODD = DEPTH // 2
F32 = jnp.float32

kernel_name = 'hybrid_stickbreak_rglru_nsa_moe_adaln'


def rmsnorm(x, g):
    xf = x.astype(F32)
    y = xf * lax.rsqrt(jnp.mean(xf * xf, axis=-1, keepdims=True) + EPS)
    return (y * g.astype(F32)).astype(x.dtype)


def adaln(c, w, b):
    m = jnp.einsum('bd,de->be', jax.nn.silu(c), w) + b
    shift, scale, gate = jnp.split(m, 3, axis=-1)
    return shift[:, None, :], scale[:, None, :], gate[:, None, :]


def masked_softmax(s, mask):
    s = jnp.where(mask, s, -jnp.inf)
    m = jnp.max(s, axis=-1, keepdims=True)
    e = jnp.where(mask, jnp.exp(s - jnp.where(jnp.isfinite(m), m, 0.0)), 0.0)
    return e / jnp.maximum(jnp.sum(e, axis=-1, keepdims=True), 1e-30)


def stick_breaking_attention(q, k, v):
    B, S, H, Dh = q.shape
    nb = S // Q_BLOCK
    scale = Dh ** -0.5
    qb = q.reshape(B, nb, Q_BLOCK, H, Dh).transpose(1, 0, 3, 2, 4)
    kt = k.transpose(0, 2, 1, 3)
    vt = v.transpose(0, 2, 1, 3)
    kpos = jnp.arange(S)

    def block(args):
        qi, bi = args
        qpos = bi * Q_BLOCK + jnp.arange(Q_BLOCK)
        causal = kpos[None, :] < qpos[:, None]
        z = jnp.einsum('bhqd,bhkd->bhqk', qi, kt, preferred_element_type=F32) * scale
        log_beta = jax.nn.log_sigmoid(z)
        log_keep = jnp.where(causal, jax.nn.log_sigmoid(-z), 0.0)
        between = lax.cumsum(log_keep, axis=3, reverse=True) - log_keep
        w = jnp.where(causal, jnp.exp(log_beta + between), 0.0)
        o = jnp.einsum('bhqk,bhkd->bhqd', w.astype(vt.dtype), vt, preferred_element_type=F32)
        return o.astype(q.dtype)

    o = lax.map(block, (qb, jnp.arange(nb)))
    return o.transpose(1, 0, 3, 2, 4).reshape(B, S, H * Dh)


def causal_depthwise_conv(x, w, b):
    C = x.shape[-1]
    K = w.shape[0]
    y = lax.conv_general_dilated(x, w[:, None, :].astype(x.dtype), window_strides=(1,),
                                 padding=[(K - 1, 0)], dimension_numbers=('NWC', 'WIO', 'NWC'),
                                 feature_group_count=C)
    return y + b


def block_diag_linear(x, w, b):
    B, S, C = x.shape
    nblk, cb, _ = w.shape
    y = jnp.einsum('bsnc,ncd->bsnd', x.reshape(B, S, nblk, cb), w).reshape(B, S, C)
    return y + b


def rg_lru(x, wa, ba, wx, bx, lam):
    xf = x.astype(F32)
    r = jax.nn.sigmoid(block_diag_linear(x, wa, ba).astype(F32))
    i = jax.nn.sigmoid(block_diag_linear(x, wx, bx).astype(F32))
    log_a = RG_C * r * jax.nn.log_sigmoid(lam.astype(F32))
    a = jnp.exp(log_a)
    u = jnp.sqrt(-jnp.expm1(2.0 * log_a)) * (i * xf)

    def combine(left, right):
        a_l, b_l = left
        a_r, b_r = right
        return a_l * a_r, a_r * b_l + b_r

    _, h = lax.associative_scan(combine, (a, u), axis=1)
    return h.astype(x.dtype)


def even_mixer(h, w_in, conv_w, conv_b, wa, ba, wx, bx, lam, w_out):
    B, S, _ = h.shape
    proj = jnp.einsum('bsd,de->bse', h, w_in)
    q, k, v, xr, gr = jnp.split(
        proj, [SB_WIDTH, 2 * SB_WIDTH, 3 * SB_WIDTH, 3 * SB_WIDTH + RG_WIDTH], axis=-1)
    heads = lambda t: t.reshape(B, S, SB_HEADS, HEAD_DIM)
    o_a = stick_breaking_attention(heads(q), heads(k), heads(v))
    xr = causal_depthwise_conv(xr, conv_w, conv_b)
    o_b = rg_lru(xr, wa, ba, wx, bx, lam) * jax.nn.gelu(gr)
    return jnp.einsum('bse,ed->bsd', jnp.concatenate([o_a, o_b], axis=-1), w_out)


def compress_blocks(t, pos, w1, w2):
    B, S, G, Dh = t.shape
    nc = (S - CMP_BLOCK) // CMP_STRIDE + 1
    idx = jnp.arange(nc)[:, None] * CMP_STRIDE + jnp.arange(CMP_BLOCK)[None, :]
    blk = t[:, idx] + pos[None, None, :, None, :]
    flat = blk.transpose(0, 1, 3, 2, 4).reshape(B, nc, G, CMP_BLOCK * Dh)
    return jax.nn.gelu(flat @ w1) @ w2


def nsa_mixer(h, w_in, pos_k, pos_v, k_w1, k_w2, v_w1, v_w2, w_out):
    B, S, _ = h.shape
    G, HG, Dh = NSA_KV_HEADS, NSA_GROUP, HEAD_DIM
    nb = S // Q_BLOCK
    n_s = S // SLC_BLOCK
    k_sel = min(N_SEL, n_s)
    scale = Dh ** -0.5
    proj = jnp.einsum('bsd,de->bse', h, w_in)
    offs = [NSA_WIDTH + j * KV_WIDTH for j in range(7)]
    q, kc, vc, ks, vs, kw, vw, g = jnp.split(proj, offs, axis=-1)
    q = q.reshape(B, S, G, HG, Dh)
    kv = lambda t: t.reshape(B, S, G, Dh)
    gates = jax.nn.sigmoid(g.astype(F32)).reshape(B, S, G, HG, N_BRANCH)

    k_cmp = compress_blocks(kv(kc), pos_k, k_w1, k_w2)
    v_cmp = compress_blocks(kv(vc), pos_v, v_w1, v_w2)
    nc = k_cmp.shape[1]
    cmp_end = jnp.arange(nc) * CMP_STRIDE + CMP_BLOCK - 1
    c_start = jnp.arange(nc)[:, None] * CMP_STRIDE
    blk = jnp.arange(n_s)
    overlap = ((c_start < (blk[None, :] + 1) * SLC_BLOCK)
               & (c_start + CMP_BLOCK > blk[None, :] * SLC_BLOCK)).astype(F32)

    ks_blk = kv(ks).reshape(B, n_s, SLC_BLOCK, G, Dh).transpose(0, 3, 1, 2, 4)
    vs_blk = kv(vs).reshape(B, n_s, SLC_BLOCK, G, Dh).transpose(0, 3, 1, 2, 4)
    kw_pad = jnp.pad(kv(kw), ((0, 0), (WINDOW, 0), (0, 0), (0, 0)))
    vw_pad = jnp.pad(kv(vw), ((0, 0), (WINDOW, 0), (0, 0), (0, 0)))
    gather = jax.vmap(jax.vmap(lambda blocks, ids: blocks[ids]))

    q_ch = q.reshape(B, nb, Q_BLOCK, G, HG, Dh).transpose(1, 0, 2, 3, 4, 5)
    g_ch = gates.reshape(B, nb, Q_BLOCK, G, HG, N_BRANCH).transpose(1, 0, 2, 3, 4, 5)

    def chunk(args):
        qc, gc, ci = args
        qpos = ci * Q_BLOCK + jnp.arange(Q_BLOCK)
        s_c = jnp.einsum('bqghd,bngd->bghqn', qc, k_cmp, preferred_element_type=F32) * scale
        p_c = masked_softmax(s_c, cmp_end[None, :] <= qpos[:, None])
        o_c = jnp.einsum('bghqn,bngd->bqghd', p_c.astype(v_cmp.dtype), v_cmp,
                         preferred_element_type=F32)
        imp = jnp.einsum('bghqn,nj->bgqj', p_c, overlap)
        cur = qpos[:, None] // SLC_BLOCK
        forced = (blk[None, :] == 0) | (blk[None, :] == cur) | (blk[None, :] == cur - 1)
        rank = jnp.where(blk[None, :] * SLC_BLOCK <= qpos[:, None],
                         imp + FORCE_BONUS * forced.astype(F32), -jnp.inf)
        _, sel = lax.top_k(rank, k_sel)
        k_g = gather(ks_blk, sel)
        v_g = gather(vs_blk, sel)
        s_s = jnp.einsum('bqghd,bgqnld->bghqnl', qc, k_g, preferred_element_type=F32) * scale
        tok = sel[..., None] * SLC_BLOCK + jnp.arange(SLC_BLOCK)
        m_s = (tok <= qpos[None, None, :, None, None])[:, :, None]
        flat_n = k_sel * SLC_BLOCK
        p_s = masked_softmax(s_s.reshape(B, G, HG, Q_BLOCK, flat_n),
                             m_s.reshape(B, G, 1, Q_BLOCK, flat_n)).reshape(s_s.shape)
        o_s = jnp.einsum('bghqnl,bgqnld->bqghd', p_s.astype(v_g.dtype), v_g,
                         preferred_element_type=F32)
        kwin = lax.dynamic_slice_in_dim(kw_pad, ci * Q_BLOCK, Q_BLOCK + WINDOW, axis=1)
        vwin = lax.dynamic_slice_in_dim(vw_pad, ci * Q_BLOCK, Q_BLOCK + WINDOW, axis=1)
        kpos = ci * Q_BLOCK - WINDOW + jnp.arange(Q_BLOCK + WINDOW)
        m_w = ((kpos[None, :] <= qpos[:, None]) & (kpos[None, :] > qpos[:, None] - WINDOW)
               & (kpos[None, :] >= 0))
        s_w = jnp.einsum('bqghd,bkgd->bghqk', qc, kwin, preferred_element_type=F32) * scale
        p_w = masked_softmax(s_w, m_w)
        o_w = jnp.einsum('bghqk,bkgd->bqghd', p_w.astype(vwin.dtype), vwin,
                         preferred_element_type=F32)
        o = gc[..., 0:1] * o_c + gc[..., 1:2] * o_s + gc[..., 2:3] * o_w
        return o.astype(qc.dtype)

    o = lax.map(chunk, (q_ch, g_ch, jnp.arange(nb)))
    o = o.transpose(1, 0, 2, 3, 4, 5).reshape(B, S, NSA_WIDTH)
    return jnp.einsum('bse,ed->bsd', o, w_out)


def dense_swiglu(h, w_gate, w_up, w_down):
    return (jax.nn.silu(h @ w_gate) * (h @ w_up)) @ w_down


def moe_swiglu(h, router_w, w_gate, w_up, w_down):
    B, S, D = h.shape
    t = h.reshape(B * S, D)
    logits = (t @ router_w).astype(F32)
    top_v, top_i = lax.top_k(logits, TOP_K)
    probs = jax.nn.softmax(top_v, axis=-1)
    gates = jnp.sum(jax.nn.one_hot(top_i, N_EXPERTS, dtype=F32) * probs[..., None], axis=1)
    out = jnp.zeros_like(t)
    for e in range(N_EXPERTS):
        y = dense_swiglu(t, w_gate[e], w_up[e], w_down[e])
        out = out + gates[:, e:e + 1].astype(t.dtype) * y
    return out.reshape(B, S, D)


def setup_inputs(seed: int = 0) -> dict:
    key = jax.random.key(seed)
    keys = iter(jax.random.split(key, 64))

    def nrm(shape, scale):
        return jax.random.normal(next(keys), shape, F32) * scale

    D = D_MODEL
    u = jax.random.uniform(next(keys), (N_EVEN, RG_WIDTH), F32, minval=0.9, maxval=0.999)
    a_base = u ** (1.0 / RG_C)
    rg_lambda = jnp.log(a_base) - jnp.log1p(-a_base)
    return {
        'x': nrm((BATCH, SEQ, D), 1.0),
        'c': nrm((BATCH, D), 1.0),
        'ada_mix_w': nrm((DEPTH, D, 3 * D), D ** -0.5),
        'ada_mix_b': nrm((DEPTH, 3 * D), 0.02),
        'norm_mix_g': 1.0 + nrm((DEPTH, D), 0.02),
        'ada_ffn_w': nrm((DEPTH, D, 3 * D), D ** -0.5),
        'ada_ffn_b': nrm((DEPTH, 3 * D), 0.02),
        'norm_ffn_g': 1.0 + nrm((DEPTH, D), 0.02),
        'even_in_w': nrm((N_EVEN, D, EVEN_IN), D ** -0.5),
        'rg_conv_w': nrm((N_EVEN, CONV_WIDTH, RG_WIDTH), CONV_WIDTH ** -0.5),
        'rg_conv_b': nrm((N_EVEN, RG_WIDTH), 0.02),
        'rg_wa': nrm((N_EVEN, RG_BLOCKS, RG_BLOCK, RG_BLOCK), RG_BLOCK ** -0.5),
        'rg_ba': nrm((N_EVEN, RG_WIDTH), 0.02),
        'rg_wx': nrm((N_EVEN, RG_BLOCKS, RG_BLOCK, RG_BLOCK), RG_BLOCK ** -0.5),
        'rg_bx': nrm((N_EVEN, RG_WIDTH), 0.02),
        'rg_lambda': rg_lambda,
        'even_out_w': nrm((N_EVEN, EVEN_MIX, D), EVEN_MIX ** -0.5),
        'dense_w_gate': nrm((N_EVEN, D, D_FF), D ** -0.5),
        'dense_w_up': nrm((N_EVEN, D, D_FF), D ** -0.5),
        'dense_w_down': nrm((N_EVEN, D_FF, D), D_FF ** -0.5),
        'nsa_in_w': nrm((N_ODD, D, NSA_IN), D ** -0.5),
        'cmp_pos_k': nrm((N_ODD, CMP_BLOCK, HEAD_DIM), 0.1),
        'cmp_pos_v': nrm((N_ODD, CMP_BLOCK, HEAD_DIM), 0.1),
        'cmp_k_w1': nrm((N_ODD, CMP_BLOCK * HEAD_DIM, HEAD_DIM), (CMP_BLOCK * HEAD_DIM) ** -0.5),
        'cmp_k_w2': nrm((N_ODD, HEAD_DIM, HEAD_DIM), HEAD_DIM ** -0.5),
        'cmp_v_w1': nrm((N_ODD, CMP_BLOCK * HEAD_DIM, HEAD_DIM), (CMP_BLOCK * HEAD_DIM) ** -0.5),
        'cmp_v_w2': nrm((N_ODD, HEAD_DIM, HEAD_DIM), HEAD_DIM ** -0.5),
        'nsa_out_w': nrm((N_ODD, NSA_WIDTH, D), NSA_WIDTH ** -0.5),
        'router_w': nrm((N_ODD, D, N_EXPERTS), D ** -0.5),
        'moe_w_gate': nrm((N_ODD, N_EXPERTS, D, D_FF), D ** -0.5),
        'moe_w_up': nrm((N_ODD, N_EXPERTS, D, D_FF), D ** -0.5),
        'moe_w_down': nrm((N_ODD, N_EXPERTS, D_FF, D), D_FF ** -0.5),
        'final_norm_g': 1.0 + nrm((D,), 0.02),
    }


def reference(x, c, ada_mix_w, ada_mix_b, norm_mix_g, ada_ffn_w, ada_ffn_b, norm_ffn_g,
              even_in_w, rg_conv_w, rg_conv_b, rg_wa, rg_ba, rg_wx, rg_bx, rg_lambda,
              even_out_w, dense_w_gate, dense_w_up, dense_w_down,
              nsa_in_w, cmp_pos_k, cmp_pos_v, cmp_k_w1, cmp_k_w2, cmp_v_w1, cmp_v_w2,
              nsa_out_w, router_w, moe_w_gate, moe_w_up, moe_w_down, final_norm_g):
    for layer in range(DEPTH):
        j = layer // 2
        shift, scale, gate = adaln(c, ada_mix_w[layer], ada_mix_b[layer])
        h = rmsnorm(x, norm_mix_g[layer]) * (1.0 + scale) + shift
        if layer % 2 == 0:
            mix = even_mixer(h, even_in_w[j], rg_conv_w[j], rg_conv_b[j], rg_wa[j], rg_ba[j],
                             rg_wx[j], rg_bx[j], rg_lambda[j], even_out_w[j])
        else:
            mix = nsa_mixer(h, nsa_in_w[j], cmp_pos_k[j], cmp_pos_v[j], cmp_k_w1[j], cmp_k_w2[j],
                            cmp_v_w1[j], cmp_v_w2[j], nsa_out_w[j])
        x = x + gate * mix
        shift, scale, gate = adaln(c, ada_ffn_w[layer], ada_ffn_b[layer])
        h = rmsnorm(x, norm_ffn_g[layer]) * (1.0 + scale) + shift
        if layer % 2 == 0:
            ffn = dense_swiglu(h, dense_w_gate[j], dense_w_up[j], dense_w_down[j])
        else:
            ffn = moe_swiglu(h, router_w[j], moe_w_gate[j], moe_w_up[j], moe_w_down[j])
        x = x + gate * ffn
    return rmsnorm(x, final_norm_g)
```

```python
import functools

import jax
import jax.numpy as jnp
from jax import lax
from jax.experimental import pallas as pl
from jax.experimental.pallas import tpu as pltpu

F32 = jnp.float32
BF16 = jnp.bfloat16

LANE = 128
SUBLANE = 8
VMEM_LIMIT_BYTES = 56 * 1024 * 1024

HEAD_DIM = 128
EPS = 1e-6
RG_C = 8.0
NSA_KV_HEADS = 4
N_BRANCH = 3
CMP_BLOCK = 32
CMP_STRIDE = 16
SLC_BLOCK = 64
N_SEL = 8
WINDOW = 512
FORCE_BONUS = 1e6
TOP_K = 2
NEG_BIG = -1e30

ROW_TILE = 256
MM_BM = 1024
MM_BN = 512
FFN_BM = 2048
FFN_SUB = 256
FFN_BF = 256
SB_BQ = 256
NSA_BQ = 128
NSA_BK = 256
RG_CB = 256
RG_TC = 256


def _params(*sem):
    return pltpu.CompilerParams(dimension_semantics=sem, vmem_limit_bytes=VMEM_LIMIT_BYTES)


def _dot(a, b):
    return jnp.dot(a, b, preferred_element_type=F32)


def _dot_nt(a, b):
    return lax.dot_general(a, b, (((1,), (1,)), ((), ())), preferred_element_type=F32)


def _sigmoid(x):
    return 1.0 / (1.0 + jnp.exp(-x))


def _log_sigmoid(x):
    return jnp.minimum(x, 0.0) - jnp.log1p(jnp.exp(-jnp.abs(x)))


def _gelu_tanh(x):
    return 0.5 * x * (1.0 + jnp.tanh(0.7978845608028654 * (x + 0.044715 * (x * x * x))))


def _split_bf16(x):
    hi = x.astype(BF16)
    lo = (x - hi.astype(F32)).astype(BF16)
    return hi, lo


def _adaln_kernel(c_ref, w_ref, b_ref, o_ref):
    c = c_ref[...]
    s = (c * _sigmoid(c)).astype(BF16)
    o_ref[0] = _dot(s, w_ref[0].astype(BF16)) + b_ref[0]


def _adaln(c_pad, w, b, bn=1024):
    depth, d, n3 = w.shape
    rows = c_pad.shape[0]
    return pl.pallas_call(
        _adaln_kernel,
        grid=(depth, n3 // bn),
        in_specs=[
            pl.BlockSpec((rows, d), lambda l, j: (0, 0)),
            pl.BlockSpec((1, d, bn), lambda l, j: (l, 0, j)),
            pl.BlockSpec((1, 1, bn), lambda l, j: (l, 0, j)),
        ],
        out_specs=pl.BlockSpec((1, rows, bn), lambda l, j: (l, 0, j)),
        out_shape=jax.ShapeDtypeStruct((depth, rows, n3), F32),
        compiler_params=_params("arbitrary", "arbitrary"),
        name="adaln",
    )(c_pad, w, b.reshape(depth, 1, n3))


def _normmod_kernel(*refs, has_res, has_mod):
    it = iter(refs)
    x_ref = next(it)
    if has_res:
        y_ref, gate_ref = next(it), next(it)
    g_ref = next(it)
    if has_mod:
        shift_ref, scale_ref = next(it), next(it)
    if has_res:
        xo_ref = next(it)
    h_ref = next(it)

    x = x_ref[...]
    if has_res:
        x = x + gate_ref[0] * y_ref[...]
        xo_ref[...] = x
    ms = jnp.mean(x * x, axis=-1, keepdims=True)
    h = x * lax.rsqrt(ms + EPS) * g_ref[...]
    if has_mod:
        h = h * (1.0 + scale_ref[0]) + shift_ref[0]
    h_ref[...] = h.astype(h_ref.dtype)


def _normmod(x, g, seq, res=None, mod=None, out_dtype=BF16):
    t, d = x.shape
    bt = ROW_TILE
    per_b = seq // bt
    row = pl.BlockSpec((bt, d), lambda i: (i, 0))
    vec = pl.BlockSpec((1, 1, d), lambda i: (i // per_b, 0, 0))
    args, specs = [x], [row]
    if res is not None:
        args += [res[0], res[1]]
        specs += [row, vec]
    args.append(g.reshape(1, d))
    specs.append(pl.BlockSpec((1, d), lambda i: (0, 0)))
    if mod is not None:
        args += [mod[0], mod[1]]
        specs += [vec, vec]
    out_shape, out_specs = [], []
    if res is not None:
        out_shape.append(jax.ShapeDtypeStruct((t, d), F32))
        out_specs.append(row)
    out_shape.append(jax.ShapeDtypeStruct((t, d), out_dtype))
    out_specs.append(row)
    outs = pl.pallas_call(
        functools.partial(_normmod_kernel, has_res=res is not None, has_mod=mod is not None),
        grid=(t // bt,),
        in_specs=specs,
        out_specs=out_specs,
        out_shape=out_shape,
        compiler_params=_params("arbitrary"),
        name="normmod",
    )(*args)
    return outs if res is not None else (x, outs[0])


def _mm_kernel(*refs, ks):
    a_refs = refs[: len(ks)]
    w_ref, o_ref, wb_ref = refs[len(ks):]

    @pl.when(pl.program_id(1) == 0)
    def _():
        wb_ref[...] = w_ref[...].astype(BF16)

    acc = None
    k0 = 0
    for a_ref, k in zip(a_refs, ks):
        part = _dot(a_ref[...], wb_ref[k0:k0 + k, :])
        acc = part if acc is None else acc + part
        k0 += k
    o_ref[...] = acc.astype(o_ref.dtype)


def _matmul(a_list, w, col0, n, out_dtype, bm=MM_BM, bn=MM_BN):
    m = a_list[0].shape[0]
    ks = tuple(a.shape[1] for a in a_list)
    ktot = sum(ks)
    bm = min(bm, m)
    bn = min(bn, n)
    cb0 = col0 // bn
    a_specs = [pl.BlockSpec((bm, k), lambda j, i: (i, 0)) for k in ks]
    return pl.pallas_call(
        functools.partial(_mm_kernel, ks=ks),
        grid=(n // bn, m // bm),
        in_specs=a_specs + [pl.BlockSpec((ktot, bn), lambda j, i: (0, cb0 + j))],
        out_specs=pl.BlockSpec((bm, bn), lambda j, i: (i, j)),
        out_shape=jax.ShapeDtypeStruct((m, n), out_dtype),
        scratch_shapes=[pltpu.VMEM((ktot, bn), BF16)],
        compiler_params=_params("arbitrary", "arbitrary"),
        name="proj_matmul",
    )(*a_list, w)


def _ffn_kernel(te_ref, nv_ref, hb_ref, h_ref, wg_ref, wu_ref, wd_ref, o_ref, wgb, wub, wdb, *, sub, nsub):
    del te_ref, hb_ref
    s = pl.program_id(0)
    f = pl.program_id(1)
    nv = nv_ref[s]

    @pl.when(nv > 0)
    def _():
        wgb[...] = wg_ref[0].astype(BF16)
        wub[...] = wu_ref[0].astype(BF16)
        wdb[...] = wd_ref[0].astype(BF16)

    for r in range(nsub):
        rows = pl.ds(r * sub, sub)

        @pl.when(r * sub < nv)
        def _():
            h = h_ref[rows, :]
            g = _dot(h, wgb[...])
            u = _dot(h, wub[...])
            a = (g * _sigmoid(g) * u).astype(BF16)
            y = _dot(a, wdb[...])

            @pl.when(f == 0)
            def _():
                o_ref[rows, :] = y

            @pl.when(f > 0)
            def _():
                o_ref[rows, :] += y

        @pl.when(jnp.logical_and(r * sub >= nv, f == 0))
        def _():
            o_ref[rows, :] = jnp.zeros((sub, o_ref.shape[1]), F32)


def _ffn(h, w_gate, w_up, w_down, tile_expert, tile_rows, tile_hblk):
    rows, d = h.shape
    _, _, dff = w_gate.shape
    bm, bf, sub = FFN_BM, FFN_BF, FFN_SUB
    nt, nf = rows // bm, dff // bf

    def f_eff(s, f, nv):
        return jnp.where(nv[s] > 0, f, nf - 1)

    grid_spec = pltpu.PrefetchScalarGridSpec(
        num_scalar_prefetch=3,
        grid=(nt, nf),
        in_specs=[
            pl.BlockSpec((bm, d), lambda s, f, te, nv, hb: (hb[s], 0), pipeline_mode=pl.Buffered(1)),
            pl.BlockSpec((1, d, bf), lambda s, f, te, nv, hb: (te[s], 0, f_eff(s, f, nv))),
            pl.BlockSpec((1, d, bf), lambda s, f, te, nv, hb: (te[s], 0, f_eff(s, f, nv))),
            pl.BlockSpec((1, bf, d), lambda s, f, te, nv, hb: (te[s], f_eff(s, f, nv), 0)),
        ],
        out_specs=pl.BlockSpec((bm, d), lambda s, f, te, nv, hb: (s, 0), pipeline_mode=pl.Buffered(1)),
        scratch_shapes=[
            pltpu.VMEM((d, bf), BF16),
            pltpu.VMEM((d, bf), BF16),
            pltpu.VMEM((bf, d), BF16),
        ],
    )
    return pl.pallas_call(
        functools.partial(_ffn_kernel, sub=sub, nsub=bm // sub),
        grid_spec=grid_spec,
        out_shape=jax.ShapeDtypeStruct((rows, d), F32),
        compiler_params=_params("arbitrary", "arbitrary"),
        name="swiglu_ffn",
    )(tile_expert, tile_rows, tile_hblk, h, w_gate, w_up, w_down)


def _sb_kernel(q_ref, k_ref, v_ref, o_ref, *, bq, scale):
    i = pl.program_id(2)
    q = q_ref[...]
    rj = lax.broadcasted_iota(jnp.int32, (2 * bq, bq), 0)
    cs = lax.broadcasted_iota(jnp.int32, (2 * bq, bq), 1)
    tri = jnp.where((rj & (bq - 1)) > cs, 1.0, 0.0).astype(BF16)
    row = lax.broadcasted_iota(jnp.int32, (bq, bq), 0)
    col = lax.broadcasted_iota(jnp.int32, (bq, bq), 1)
    causal = col < row

    def tile(j, carry, acc, diag):
        k0 = pl.multiple_of(j * bq, bq)
        k = k_ref[pl.ds(k0, bq), :]
        v = v_ref[pl.ds(k0, bq), :]
        z = _dot_nt(q, k) * scale
        log_beta = _log_sigmoid(z)
        log_keep = log_beta - z
        if diag:
            log_keep = jnp.where(causal, log_keep, 0.0)
        hi, lo = _split_bf16(log_keep)
        between = _dot(jnp.concatenate([hi, lo], axis=1), tri) + carry
        w = jnp.exp(log_beta + between)
        if diag:
            w = jnp.where(causal, w, 0.0)
        acc = acc + _dot(w.astype(BF16), v)
        carry = carry + jnp.sum(log_keep, axis=-1, keepdims=True)
        return carry, acc

    carry0 = jnp.zeros((bq, 1), F32)
    acc0 = jnp.zeros((bq, q.shape[1]), F32)
    carry, acc = tile(i, carry0, acc0, True)

    def body(t, c):
        return tile(i - 1 - t, c[0], c[1], False)

    carry, acc = lax.fori_loop(0, i, body, (carry, acc))
    o_ref[...] = acc.astype(o_ref.dtype)


def _sb_attention(qkv, batch, seq, heads):
    t = qkv.shape[0]
    bq = min(SB_BQ, seq)
    nq = seq // bq
    dh = HEAD_DIM
    return pl.pallas_call(
        functools.partial(_sb_kernel, bq=bq, scale=dh ** -0.5),
        grid=(batch, heads, nq),
        in_specs=[
            pl.BlockSpec((bq, dh), lambda b, h, i: (b * nq + i, h)),
            pl.BlockSpec((seq, dh), lambda b, h, i: (b, heads + h)),
            pl.BlockSpec((seq, dh), lambda b, h, i: (b, 2 * heads + h)),
        ],
        out_specs=pl.BlockSpec((bq, dh), lambda b, h, i: (b * nq + i, h)),
        out_shape=jax.ShapeDtypeStruct((t, heads * dh), BF16),
        compiler_params=_params("arbitrary", "arbitrary", "arbitrary"),
        name="stickbreak_attn",
    )(qkv, qkv, qkv)


def _rglru_kernel(x_ref, g_ref, cw_ref, cb_ref, wa_ref, ba_ref, wx_ref, bx_ref, lam_ref, o_ref, xp_ref,
                  *, seq, cb, tc, kw):
    pad = SUBLANE
    xp_ref[0:pad, :] = jnp.zeros((pad, cb), F32)
    xp_ref[pad:, :] = x_ref[...]
    log_lam = _log_sigmoid(lam_ref[...])
    sub_iota = lax.broadcasted_iota(jnp.int32, (SUBLANE, cb), 0)
    ngroup = cb // LANE

    def chunk(ci, h):
        t0 = pl.multiple_of(ci * tc, tc)
        win = xp_ref[pl.ds(t0, tc + pad), :]
        xc = cb_ref[...] + cw_ref[0:1, :] * win[pad - kw + 1:pad - kw + 1 + tc, :]
        for k in range(1, kw):
            off = pad - kw + 1 + k
            xc = xc + cw_ref[k:k + 1, :] * win[off:off + tc, :]
        xcb = xc.astype(BF16)
        ra = jnp.concatenate(
            [_dot(xcb[:, q * LANE:(q + 1) * LANE], wa_ref[q].astype(BF16)) for q in range(ngroup)], axis=1)
        rx = jnp.concatenate(
            [_dot(xcb[:, q * LANE:(q + 1) * LANE], wx_ref[q].astype(BF16)) for q in range(ngroup)], axis=1)
        r = _sigmoid(ra + ba_ref[...])
        gi = _sigmoid(rx + bx_ref[...])
        log_a = RG_C * r * log_lam
        a = jnp.exp(log_a)
        u = jnp.sqrt(-jnp.tanh(log_a) * (a * a + 1.0)) * (gi * xc)
        gate = _gelu_tanh(g_ref[pl.ds(t0, tc), :])
        outs = []
        for gidx in range(tc // SUBLANE):
            av = a[gidx * SUBLANE:(gidx + 1) * SUBLANE, :]
            bv = u[gidx * SUBLANE:(gidx + 1) * SUBLANE, :]
            for sh in (1, 2, 4):
                a_s = pltpu.roll(av, sh, axis=0)
                b_s = pltpu.roll(bv, sh, axis=0)
                m = sub_iota >= sh
                bv = jnp.where(m, av * b_s + bv, bv)
                av = jnp.where(m, av * a_s, av)
            hv = av * h + bv
            outs.append(hv)
            h = jnp.broadcast_to(hv[SUBLANE - 1:SUBLANE, :], (SUBLANE, cb))
        hs = jnp.concatenate(outs, axis=0)
        o_ref[pl.ds(t0, tc), :] = (hs * gate).astype(o_ref.dtype)
        return h

    lax.fori_loop(0, seq // tc, chunk, jnp.zeros((SUBLANE, cb), F32))


def _block_diag_pairs(w):
    nblk, c, _ = w.shape
    w2 = w.reshape(nblk // 2, 2, c, c)
    z = jnp.zeros((nblk // 2, c, c), w.dtype)
    top = jnp.concatenate([w2[:, 0], z], axis=2)
    bot = jnp.concatenate([z, w2[:, 1]], axis=2)
    return jnp.concatenate([top, bot], axis=1)


def _rglru(rg, batch, seq, conv_w, conv_b, wa, ba, wx, bx, lam):
    t, c2 = rg.shape
    c = c2 // 2
    cb = min(RG_CB, c)
    tc = min(RG_TC, seq)
    ncb = c // cb
    kw = conv_w.shape[0]
    gpb = cb // LANE
    vec = pl.BlockSpec((1, cb), lambda b, j: (0, j))
    return pl.pallas_call(
        functools.partial(_rglru_kernel, seq=seq, cb=cb, tc=tc, kw=kw),
        grid=(batch, ncb),
        in_specs=[
            pl.BlockSpec((seq, cb), lambda b, j: (b, j)),
            pl.BlockSpec((seq, cb), lambda b, j: (b, ncb + j)),
            pl.BlockSpec((kw, cb), lambda b, j: (0, j)),
            vec,
            pl.BlockSpec((gpb, LANE, LANE), lambda b, j: (j, 0, 0)),
            vec,
            pl.BlockSpec((gpb, LANE, LANE), lambda b, j: (j, 0, 0)),
            vec,
            vec,
        ],
        out_specs=pl.BlockSpec((seq, cb), lambda b, j: (b, j)),
        out_shape=jax.ShapeDtypeStruct((t, c), BF16),
        scratch_shapes=[pltpu.VMEM((seq + SUBLANE, cb), F32)],
        compiler_params=_params("arbitrary", "arbitrary"),
        name="rglru",
    )(rg, rg, conv_w, conv_b.reshape(1, c), _block_diag_pairs(wa), ba.reshape(1, c),
      _block_diag_pairs(wx), bx.reshape(1, c), lam.reshape(1, c))


def _compress_kernel(x_ref, pos_ref, w1_ref, w2_ref, o_ref, *, ngrp):
    st = CMP_STRIDE
    half = st * HEAD_DIM
    xs = [x_ref[pl.ds(r, ngrp, stride=st), :] for r in range(st)]
    pos = pos_ref[0]
    x0 = jnp.concatenate([xs[r] + pos[r:r + 1, :] for r in range(st)], axis=1).astype(BF16)
    x1 = jnp.concatenate([xs[r] + pos[st + r:st + r + 1, :] for r in range(st)], axis=1).astype(BF16)
    p0 = _dot(x0, w1_ref[0, 0:half, :].astype(BF16))
    p1 = _dot(x1, w1_ref[0, half:2 * half, :].astype(BF16))
    pre = p0 + pltpu.roll(p1, ngrp - 1, axis=0)
    out = _dot(_gelu_tanh(pre).astype(BF16), w2_ref[0].astype(BF16))
    rown = lax.broadcasted_iota(jnp.int32, out.shape, 0)
    o_ref[...] = jnp.where(rown < ngrp - 1, out, 0.0).astype(o_ref.dtype)


def _compress(kcvc, batch, seq, pos, w1, w2):
    assert CMP_BLOCK == 2 * CMP_STRIDE
    g2 = kcvc.shape[1] // HEAD_DIM
    per = g2 // 2
    ngrp = seq // CMP_STRIDE
    return pl.pallas_call(
        functools.partial(_compress_kernel, ngrp=ngrp),
        grid=(batch, g2),
        in_specs=[
            pl.BlockSpec((seq, HEAD_DIM), lambda b, j: (b, j)),
            pl.BlockSpec((1, CMP_BLOCK, HEAD_DIM), lambda b, j: (j // per, 0, 0)),
            pl.BlockSpec((1, CMP_BLOCK * HEAD_DIM, HEAD_DIM), lambda b, j: (j // per, 0, 0)),
            pl.BlockSpec((1, HEAD_DIM, HEAD_DIM), lambda b, j: (j // per, 0, 0)),
        ],
        out_specs=pl.BlockSpec((ngrp, HEAD_DIM), lambda b, j: (b * g2 + j, 0)),
        out_shape=jax.ShapeDtypeStruct((batch * g2 * ngrp, HEAD_DIM), BF16),
        compiler_params=_params("arbitrary", "arbitrary"),
        name="nsa_compress",
    )(kcvc, pos, w1, w2)


def _softmax_rows(s, mask):
    m = jnp.max(jnp.where(mask, s, NEG_BIG), axis=-1, keepdims=True)
    e = jnp.where(mask, jnp.exp(s - m), 0.0)
    return e / jnp.maximum(jnp.sum(e, axis=-1, keepdims=True), 1e-30)


def _nsa_kernel(q_ref, kc_ref, vc_ref, ks_ref, vs_ref, kw_ref, vw_ref, g_ref, o_ref,
                *, seq, hg, n_slc, scale):
    bq, bk, dh = NSA_BQ, NSA_BK, HEAD_DIM
    i = pl.program_id(2)
    q0 = i * bq
    qb = q_ref[...]
    qs = jnp.concatenate([qb[:, h * dh:(h + 1) * dh] for h in range(hg)], axis=0)
    qpos1 = q0 + lax.broadcasted_iota(jnp.int32, (bq, 1), 0)
    qpos = jnp.concatenate([qpos1] * hg, axis=0)

    ncmp = kc_ref.shape[0]
    s_c = _dot_nt(qs, kc_ref[...]) * scale
    n_idx = lax.broadcasted_iota(jnp.int32, (1, ncmp), 1)
    p_c = _softmax_rows(s_c, n_idx * CMP_STRIDE + (CMP_BLOCK - 1) <= qpos)
    o_c = _dot(p_c.astype(BF16), vc_ref[...])

    p_sum = p_c[0:bq, :]
    for h in range(1, hg):
        p_sum = p_sum + p_c[h * bq:(h + 1) * bq, :]
    jj = lax.broadcasted_iota(jnp.int32, (LANE, ncmp), 0)
    cstart = lax.broadcasted_iota(jnp.int32, (LANE, ncmp), 1) * CMP_STRIDE
    overlap_t = jnp.where((cstart < (jj + 1) * SLC_BLOCK) & (cstart + CMP_BLOCK > jj * SLC_BLOCK)
                          & (jj < n_slc), 1.0, 0.0).astype(BF16)
    p_hi, p_lo = _split_bf16(p_sum)
    imp_t = (_dot_nt(overlap_t, p_hi) + _dot_nt(overlap_t, p_lo))[0:n_slc, :]

    blk = lax.broadcasted_iota(jnp.int32, (n_slc, bq), 0)
    qp_l = q0 + lax.broadcasted_iota(jnp.int32, (n_slc, bq), 1)
    cur = qp_l // SLC_BLOCK
    forced = (blk == 0) | (blk == cur) | (blk == cur - 1)
    valid = blk * SLC_BLOCK <= qp_l
    rank = jnp.where(valid, imp_t + FORCE_BONUS * jnp.where(forced, 1.0, 0.0), -jnp.inf)
    ahead = jnp.zeros((n_slc, bq), F32)
    for jp in range(n_slc):
        other = rank[jp:jp + 1, :]
        beats = (other > rank) | ((other == rank) & (blk > jp))
        ahead = ahead + jnp.where(beats, 1.0, 0.0)
    k_sel = min(N_SEL, n_slc)
    sel_t = jnp.where(valid & (ahead < k_sel), 1.0, 0.0)
    sel_t = jnp.concatenate([sel_t, jnp.zeros((LANE - n_slc, bq), F32)], axis=0)
    sel = sel_t.T.astype(BF16)

    def sel_tile(t, c):
        m_run, l_run, acc = c
        k0 = pl.multiple_of(t * bk, bk)
        kt = ks_ref[pl.ds(k0, bk), :]
        vt = vs_ref[pl.ds(k0, bk), :]
        s = _dot_nt(qs, kt) * scale
        kpos = k0 + lax.broadcasted_iota(jnp.int32, (LANE, bk), 1)
        expand = jnp.where(kpos // SLC_BLOCK == lax.broadcasted_iota(jnp.int32, (LANE, bk), 0),
                           1.0, 0.0).astype(BF16)
        picked = _dot(sel, expand)
        ok1 = jnp.where((picked > 0.5) & (kpos[0:1, :] <= qpos1), 1.0, 0.0)
        ok = jnp.concatenate([ok1] * hg, axis=0) > 0.5
        m_new = jnp.maximum(m_run, jnp.max(jnp.where(ok, s, NEG_BIG), axis=-1, keepdims=True))
        alpha = jnp.exp(m_run - m_new)
        p = jnp.where(ok, jnp.exp(s - m_new), 0.0)
        l_new = alpha * l_run + jnp.sum(p, axis=-1, keepdims=True)
        acc = alpha * acc + _dot(p.astype(BF16), vt)
        return m_new, l_new, acc

    rows = hg * bq
    init = (jnp.full((rows, 1), NEG_BIG, F32), jnp.zeros((rows, 1), F32), jnp.zeros((rows, dh), F32))
    n_tiles = (q0 + bq + bk - 1) // bk
    _, l_s, acc_s = lax.fori_loop(0, n_tiles, sel_tile, init)
    o_s = acc_s / jnp.maximum(l_s, 1e-30)

    span = min(WINDOW + bq, seq)
    w0 = pl.multiple_of(jnp.maximum(jnp.minimum(q0 - WINDOW, seq - span), 0), bq)
    kwt = kw_ref[pl.ds(w0, span), :]
    vwt = vw_ref[pl.ds(w0, span), :]
    s_w = _dot_nt(qs, kwt) * scale
    kpos_w = w0 + lax.broadcasted_iota(jnp.int32, (1, span), 1)
    p_w = _softmax_rows(s_w, (kpos_w <= qpos) & (kpos_w > qpos - WINDOW))
    o_w = _dot(p_w.astype(BF16), vwt)

    gates = _sigmoid(g_ref[...])
    outs = []
    for h in range(hg):
        r = slice(h * bq, (h + 1) * bq)
        c = h * N_BRANCH
        outs.append(gates[:, c:c + 1] * o_c[r, :] + gates[:, c + 1:c + 2] * o_s[r, :]
                    + gates[:, c + 2:c + 3] * o_w[r, :])
    o_ref[...] = jnp.concatenate(outs, axis=1).astype(o_ref.dtype)


def _nsa_attention(q, kv4, cmp, gate_logits, batch, seq):
    t, width = q.shape
    g = NSA_KV_HEADS
    hg = width // (g * HEAD_DIM)
    bq = NSA_BQ
    nb = seq // bq
    ncmp = seq // CMP_STRIDE
    n_slc = seq // SLC_BLOCK
    dh = HEAD_DIM
    full = lambda off: pl.BlockSpec((seq, dh), lambda b, gi, i: (b, off * g + gi))
    return pl.pallas_call(
        functools.partial(_nsa_kernel, seq=seq, hg=hg, n_slc=n_slc, scale=dh ** -0.5),
        grid=(batch, g, nb),
        in_specs=[
            pl.BlockSpec((bq, hg * dh), lambda b, gi, i: (b * nb + i, gi)),
            pl.BlockSpec((ncmp, dh), lambda b, gi, i: (b * 2 * g + gi, 0)),
            pl.BlockSpec((ncmp, dh), lambda b, gi, i: (b * 2 * g + g + gi, 0)),
            full(0), full(1), full(2), full(3),
            pl.BlockSpec((bq, LANE), lambda b, gi, i: (b * nb + i, gi)),
        ],
        out_specs=pl.BlockSpec((bq, hg * dh), lambda b, gi, i: (b * nb + i, gi)),
        out_shape=jax.ShapeDtypeStruct((t, width), BF16),
        compiler_params=_params("arbitrary", "arbitrary", "arbitrary"),
        name="nsa_attn",
    )(q, cmp, cmp, kv4, kv4, kv4, kv4, gate_logits)


def _route(logits, n_experts):
    t = logits.shape[0]
    top_v, top_i = lax.top_k(logits, TOP_K)
    probs = jax.nn.softmax(top_v, axis=-1)
    flat_e = top_i.reshape(-1)
    onehot = (flat_e[:, None] == jnp.arange(n_experts)[None, :]).astype(jnp.int32)
    rank = jnp.take_along_axis(jnp.cumsum(onehot, axis=0), flat_e[:, None], axis=1)[:, 0] - 1
    counts = jnp.sum(onehot, axis=0)
    tiles_per = (counts + FFN_BM - 1) // FFN_BM
    tile_end = jnp.cumsum(tiles_per)
    tile_start = tile_end - tiles_per
    n_tiles = (TOP_K * t) // FFN_BM + n_experts
    dest = (tile_start[flat_e] * FFN_BM + rank).astype(jnp.int32)
    tid = jnp.arange(n_tiles)
    used = tile_end[-1]
    te = jnp.minimum(jnp.searchsorted(tile_end, tid, side="right"), n_experts - 1)
    te_last = jnp.searchsorted(tile_end, used - 1, side="right")
    te = jnp.where(tid < used, te, te_last).astype(jnp.int32)
    nv = jnp.clip(counts[te] - (tid - tile_start[te]) * FFN_BM, 0, FFN_BM)
    nv = jnp.where(tid < used, nv, 0).astype(jnp.int32)
    hblk = jnp.minimum(tid, used - 1).astype(jnp.int32)
    src = jnp.zeros((n_tiles * FFN_BM,), jnp.int32).at[dest].set(jnp.arange(TOP_K * t, dtype=jnp.int32) // TOP_K)
    return probs, dest.reshape(t, TOP_K), src, te, nv, hblk


def kernel(x, c, ada_mix_w, ada_mix_b, norm_mix_g, ada_ffn_w, ada_ffn_b, norm_ffn_g, even_in_w, rg_conv_w, rg_conv_b, rg_wa, rg_ba, rg_wx, rg_bx, rg_lambda, even_out_w, dense_w_gate, dense_w_up, dense_w_down, nsa_in_w, cmp_pos_k, cmp_pos_v, cmp_k_w1, cmp_k_w2, cmp_v_w1, cmp_v_w2, nsa_out_w, router_w, moe_w_gate, moe_w_up, moe_w_down, final_norm_g):
    batch, seq, d = x.shape
    t = batch * seq
    depth = ada_mix_w.shape[0]
    xf = x.reshape(t, d)

    c_pad = jnp.pad(c, ((0, (-batch) % SUBLANE), (0, 0)))
    m_mix = _adaln(c_pad, ada_mix_w, ada_mix_b)
    m_ffn = _adaln(c_pad, ada_ffn_w, ada_ffn_b)

    def mods(m, layer):
        v = m[layer, :batch].reshape(batch, 1, 3, d)
        return v[:, :, 0], v[:, :, 1], v[:, :, 2]

    pending = None
    for layer in range(depth):
        j = layer // 2
        shift, scale, gate = mods(m_mix, layer)
        xf, h = _normmod(xf, norm_mix_g[layer], seq, res=pending, mod=(shift, scale))
        if layer % 2 == 0:
            w_in = even_in_w[j]
            sbw = (w_in.shape[1] - 2 * rg_conv_w.shape[2]) // 3
            heads = sbw // HEAD_DIM
            qkv = _matmul([h], w_in, 0, 3 * sbw, BF16)
            rg = _matmul([h], w_in, 3 * sbw, w_in.shape[1] - 3 * sbw, F32)
            o_a = _sb_attention(qkv, batch, seq, heads)
            o_b = _rglru(rg, batch, seq, rg_conv_w[j], rg_conv_b[j], rg_wa[j], rg_ba[j], rg_wx[j],
                         rg_bx[j], rg_lambda[j])
            mix = _matmul([o_a, o_b], even_out_w[j], 0, d, F32)
        else:
            w_in = nsa_in_w[j]
            g = NSA_KV_HEADS
            kvw = g * HEAD_DIM
            nsa_w = nsa_out_w.shape[1]
            hg = nsa_w // kvw
            q = _matmul([h], w_in, 0, nsa_w, BF16)
            kcvc = _matmul([h], w_in, nsa_w, 2 * kvw, F32)
            kv4 = _matmul([h], w_in, nsa_w + 2 * kvw, 4 * kvw, BF16)
            wg = w_in[:, nsa_w + 6 * kvw:].reshape(d, g, hg * N_BRANCH)
            wg = jnp.pad(wg, ((0, 0), (0, 0), (0, LANE - hg * N_BRANCH))).reshape(d, g * LANE)
            gl = _matmul([h], wg, 0, g * LANE, F32)
            cmp = _compress(kcvc, batch, seq, jnp.stack([cmp_pos_k[j], cmp_pos_v[j]]),
                            jnp.stack([cmp_k_w1[j], cmp_v_w1[j]]), jnp.stack([cmp_k_w2[j], cmp_v_w2[j]]))
            o = _nsa_attention(q, kv4, cmp, gl, batch, seq)
            mix = _matmul([o], nsa_out_w[j], 0, d, F32)
        pending = (mix, gate)

        shift, scale, gate = mods(m_ffn, layer)
        xf, h = _normmod(xf, norm_ffn_g[layer], seq, res=pending, mod=(shift, scale))
        if layer % 2 == 0:
            nt = t // FFN_BM
            ffn = _ffn(h, dense_w_gate[j:j + 1], dense_w_up[j:j + 1], dense_w_down[j:j + 1],
                       jnp.zeros((nt,), jnp.int32), jnp.full((nt,), FFN_BM, jnp.int32),
                       jnp.arange(nt, dtype=jnp.int32))
        else:
            n_exp = router_w.shape[2]
            rw = jnp.pad(router_w[j], ((0, 0), (0, LANE - n_exp)))
            logits = _matmul([h], rw, 0, LANE, F32)[:, :n_exp]
            probs, dest, src, te, nv, hblk = _route(logits, n_exp)
            h_sorted = jnp.take(h, src, axis=0)
            y = _ffn(h_sorted, moe_w_gate[j], moe_w_up[j], moe_w_down[j], te, nv, hblk)
            ffn = probs[:, 0:1] * jnp.take(y, dest[:, 0], axis=0) + probs[:, 1:2] * jnp.take(y, dest[:, 1], axis=0)
        pending = (ffn, gate)

    _, out = _normmod(xf, final_norm_g, seq, res=pending, mod=None, out_dtype=F32)
    return out.reshape(batch, seq, d)
```

```python
import functools

import jax
import jax.numpy as jnp
from jax import lax
from jax.experimental import pallas as pl
from jax.experimental.pallas import tpu as pltpu

F32 = jnp.float32
BF16 = jnp.bfloat16

LANE = 128
SUBLANE = 8
VMEM_LIMIT_BYTES = 56 * 1024 * 1024

HEAD_DIM = 128
EPS = 1e-6
RG_C = 8.0
NSA_KV_HEADS = 4
N_BRANCH = 3
CMP_BLOCK = 32
CMP_STRIDE = 16
SLC_BLOCK = 64
N_SEL = 8
WINDOW = 512
FORCE_BONUS = 1e6
TOP_K = 2
NEG_BIG = -1e30

ROW_TILE = 256
MM_BM = 1024
MM_BN = 512
FFN_BM = 2048
FFN_SUB = 512
FFN_NCOL = 4
FFN_BF = 256
GATHER_ROWS = 256
GATHER_CHUNK = 512
SB_BQ = 256
SB_HEADS_PER_STEP = 2
NSA_BQ = 128
NSA_BK = 256
RG_CB = 256
RG_TC = 256


def _params(*sem):
    return pltpu.CompilerParams(dimension_semantics=sem, vmem_limit_bytes=VMEM_LIMIT_BYTES)


def _dot(a, b):
    return jnp.dot(a, b, preferred_element_type=F32)


def _dot_nt(a, b):
    return lax.dot_general(a, b, (((1,), (1,)), ((), ())), preferred_element_type=F32)


def _sigmoid(x):
    return 1.0 / (1.0 + jnp.exp(-x))


def _log_sigmoid(x):
    return jnp.minimum(x, 0.0) - jnp.log1p(jnp.exp(-jnp.abs(x)))


def _gelu_tanh(x):
    return 0.5 * x * (1.0 + jnp.tanh(0.7978845608028654 * (x + 0.044715 * (x * x * x))))


def _split_bf16(x):
    hi = x.astype(BF16)
    lo = (x - hi.astype(F32)).astype(BF16)
    return hi, lo


def _adaln_kernel(c_ref, w_ref, b_ref, o_ref):
    c = c_ref[...]
    s = (c * _sigmoid(c)).astype(BF16)
    o_ref[0] = _dot(s, w_ref[0].astype(BF16)) + b_ref[0]


def _adaln(c_pad, w, b, bn=1024):
    depth, d, n3 = w.shape
    rows = c_pad.shape[0]
    return pl.pallas_call(
        _adaln_kernel,
        grid=(depth, n3 // bn),
        in_specs=[
            pl.BlockSpec((rows, d), lambda l, j: (0, 0)),
            pl.BlockSpec((1, d, bn), lambda l, j: (l, 0, j)),
            pl.BlockSpec((1, 1, bn), lambda l, j: (l, 0, j)),
        ],
        out_specs=pl.BlockSpec((1, rows, bn), lambda l, j: (l, 0, j)),
        out_shape=jax.ShapeDtypeStruct((depth, rows, n3), F32),
        compiler_params=_params("arbitrary", "arbitrary"),
        name="adaln",
    )(c_pad, w, b.reshape(depth, 1, n3))


def _normmod_kernel(*refs, has_res, has_mod):
    it = iter(refs)
    x_ref = next(it)
    if has_res:
        y_ref, gate_ref = next(it), next(it)
    g_ref = next(it)
    if has_mod:
        shift_ref, scale_ref = next(it), next(it)
    if has_res:
        xo_ref = next(it)
    h_ref = next(it)

    x = x_ref[...]
    if has_res:
        x = x + gate_ref[0] * y_ref[...]
        xo_ref[...] = x
    ms = jnp.mean(x * x, axis=-1, keepdims=True)
    h = x * lax.rsqrt(ms + EPS) * g_ref[...]
    if has_mod:
        h = h * (1.0 + scale_ref[0]) + shift_ref[0]
    h_ref[...] = h.astype(h_ref.dtype)


def _normmod(x, g, seq, res=None, mod=None, out_dtype=BF16):
    t, d = x.shape
    bt = ROW_TILE
    per_b = seq // bt
    row = pl.BlockSpec((bt, d), lambda i: (i, 0))
    vec = pl.BlockSpec((1, 1, d), lambda i: (i // per_b, 0, 0))
    args, specs = [x], [row]
    if res is not None:
        args += [res[0], res[1]]
        specs += [row, vec]
    args.append(g.reshape(1, d))
    specs.append(pl.BlockSpec((1, d), lambda i: (0, 0)))
    if mod is not None:
        args += [mod[0], mod[1]]
        specs += [vec, vec]
    out_shape, out_specs = [], []
    if res is not None:
        out_shape.append(jax.ShapeDtypeStruct((t, d), F32))
        out_specs.append(row)
    out_shape.append(jax.ShapeDtypeStruct((t, d), out_dtype))
    out_specs.append(row)
    outs = pl.pallas_call(
        functools.partial(_normmod_kernel, has_res=res is not None, has_mod=mod is not None),
        grid=(t // bt,),
        in_specs=specs,
        out_specs=out_specs,
        out_shape=out_shape,
        compiler_params=_params("arbitrary"),
        name="normmod",
    )(*args)
    return outs if res is not None else (x, outs[0])


def _mm_kernel(*refs, ks):
    a_refs = refs[: len(ks)]
    w_ref, o_ref, wb_ref = refs[len(ks):]

    @pl.when(pl.program_id(1) == 0)
    def _():
        wb_ref[...] = w_ref[...].astype(BF16)

    acc = None
    k0 = 0
    for a_ref, k in zip(a_refs, ks):
        part = _dot(a_ref[...], wb_ref[k0:k0 + k, :])
        acc = part if acc is None else acc + part
        k0 += k
    o_ref[...] = acc.astype(o_ref.dtype)


def _matmul(a_list, w, col0, n, out_dtype, bm=MM_BM, bn=MM_BN):
    m = a_list[0].shape[0]
    ks = tuple(a.shape[1] for a in a_list)
    ktot = sum(ks)
    bm = min(bm, m)
    bn = min(bn, n)
    cb0 = col0 // bn
    a_specs = [pl.BlockSpec((bm, k), lambda j, i: (i, 0)) for k in ks]
    return pl.pallas_call(
        functools.partial(_mm_kernel, ks=ks),
        grid=(n // bn, m // bm),
        in_specs=a_specs + [pl.BlockSpec((ktot, bn), lambda j, i: (0, cb0 + j))],
        out_specs=pl.BlockSpec((bm, bn), lambda j, i: (i, j)),
        out_shape=jax.ShapeDtypeStruct((m, n), out_dtype),
        scratch_shapes=[pltpu.VMEM((ktot, bn), BF16)],
        compiler_params=_params("arbitrary", "arbitrary"),
        name="proj_matmul",
    )(*a_list, w)


def _ffn_kernel(te_ref, nv_ref, hb_ref, h_ref, wg_ref, wu_ref, wd_ref, o_ref, wgb, wub, wdb, *, sub, nsub, ncol):
    del te_ref, hb_ref
    s = pl.program_id(0)
    f = pl.program_id(1)
    nv = nv_ref[s]
    bm, d = o_ref.shape
    nchunk = (nv + sub - 1) // sub

    @pl.when(nv > 0)
    def _():
        wgb[...] = wg_ref[0].astype(BF16)
        wub[...] = wu_ref[0].astype(BF16)
        wdb[...] = wd_ref[0].astype(BF16)

    for c in range(1, nsub + 1):
        rows = c * sub

        @pl.when(nchunk == c)
        def _():
            h = h_ref[0:rows, :]
            g = _dot(h, wgb[...])
            u = _dot(h, wub[...])
            a = (g * _sigmoid(g) * u).astype(BF16)
            for n in range(ncol):
                cols = slice(n * (d // ncol), (n + 1) * (d // ncol))
                y = _dot(a, wdb[:, cols])

                @pl.when(f == 0)
                def _():
                    o_ref[0:rows, cols] = y

                @pl.when(f > 0)
                def _():
                    o_ref[0:rows, cols] += y

            if rows < bm:
                @pl.when(f == 0)
                def _():
                    o_ref[rows:bm, :] = jnp.zeros((bm - rows, d), F32)

    @pl.when(jnp.logical_and(nchunk == 0, f == 0))
    def _():
        o_ref[...] = jnp.zeros((bm, d), F32)


def _ffn(h, w_gate, w_up, w_down, tile_expert, tile_rows, tile_hblk):
    rows, d = h.shape
    _, _, dff = w_gate.shape
    bm, bf, sub = FFN_BM, FFN_BF, FFN_SUB
    nt, nf = rows // bm, dff // bf

    def f_eff(s, f, nv):
        return jnp.where(nv[s] > 0, f, nf - 1)

    grid_spec = pltpu.PrefetchScalarGridSpec(
        num_scalar_prefetch=3,
        grid=(nt, nf),
        in_specs=[
            pl.BlockSpec((bm, d), lambda s, f, te, nv, hb: (hb[s], 0), pipeline_mode=pl.Buffered(1)),
            pl.BlockSpec((1, d, bf), lambda s, f, te, nv, hb: (te[s], 0, f_eff(s, f, nv))),
            pl.BlockSpec((1, d, bf), lambda s, f, te, nv, hb: (te[s], 0, f_eff(s, f, nv))),
            pl.BlockSpec((1, bf, d), lambda s, f, te, nv, hb: (te[s], f_eff(s, f, nv), 0)),
        ],
        out_specs=pl.BlockSpec((bm, d), lambda s, f, te, nv, hb: (s, 0), pipeline_mode=pl.Buffered(1)),
        scratch_shapes=[
            pltpu.VMEM((d, bf), BF16),
            pltpu.VMEM((d, bf), BF16),
            pltpu.VMEM((bf, d), BF16),
        ],
    )
    return pl.pallas_call(
        functools.partial(_ffn_kernel, sub=sub, nsub=bm // sub, ncol=FFN_NCOL),
        grid_spec=grid_spec,
        out_shape=jax.ShapeDtypeStruct((rows, d), F32),
        compiler_params=_params("arbitrary", "arbitrary"),
        name="swiglu_ffn",
    )(tile_expert, tile_rows, tile_hblk, h, w_gate, w_up, w_down)


def _sb_kernel(q_ref, k_ref, v_ref, tri_ref, o_ref, *, bq, nh, scale):
    i = pl.program_id(2)
    dh = HEAD_DIM
    row = lax.broadcasted_iota(jnp.int32, (bq, bq), 0)
    col = lax.broadcasted_iota(jnp.int32, (bq, bq), 1)
    causal = col < row
    qs = [q_ref[:, h * dh:(h + 1) * dh] for h in range(nh)]

    def tile(j, state, diag):
        k0 = pl.multiple_of(j * bq, bq)
        out = []
        for h in range(nh):
            carry, acc = state[2 * h], state[2 * h + 1]
            k = k_ref[pl.ds(k0, bq), h * dh:(h + 1) * dh]
            v = v_ref[pl.ds(k0, bq), h * dh:(h + 1) * dh]
            z = _dot_nt(qs[h], k) * scale
            log_beta = jnp.minimum(z, 0.0) - jnp.log(1.0 + jnp.exp(-jnp.abs(z)))
            log_keep = log_beta - z
            if diag:
                log_keep = jnp.where(causal, log_keep, 0.0)
            hi, lo = _split_bf16(log_keep)
            between = _dot(jnp.concatenate([hi, lo], axis=1), tri_ref[...]) + carry
            w = jnp.exp(log_beta + between)
            if diag:
                w = jnp.where(causal, w, 0.0)
            out.append(carry + jnp.sum(log_keep, axis=-1, keepdims=True))
            out.append(acc + _dot(w.astype(BF16), v))
        return tuple(out)

    state = tuple(jnp.zeros((bq, 1 if n % 2 == 0 else dh), F32) for n in range(2 * nh))
    state = tile(i, state, True)
    state = lax.fori_loop(0, i, lambda t, c: tile(i - 1 - t, c, False), state)
    o_ref[...] = jnp.concatenate([state[2 * h + 1] for h in range(nh)], axis=1).astype(o_ref.dtype)


def _sb_attention(qkv, batch, seq, heads):
    t = qkv.shape[0]
    bq = min(SB_BQ, seq)
    nq = seq // bq
    nh = SB_HEADS_PER_STEP
    hw = nh * HEAD_DIM
    ng = heads // nh
    rj = lax.broadcasted_iota(jnp.int32, (2 * bq, bq), 0)
    cs = lax.broadcasted_iota(jnp.int32, (2 * bq, bq), 1)
    tri = jnp.where((rj % bq) > cs, 1.0, 0.0).astype(BF16)
    return pl.pallas_call(
        functools.partial(_sb_kernel, bq=bq, nh=nh, scale=HEAD_DIM ** -0.5),
        grid=(batch, ng, nq),
        in_specs=[
            pl.BlockSpec((bq, hw), lambda b, h, i: (b * nq + i, h)),
            pl.BlockSpec((seq, hw), lambda b, h, i: (b, ng + h)),
            pl.BlockSpec((seq, hw), lambda b, h, i: (b, 2 * ng + h)),
            pl.BlockSpec((2 * bq, bq), lambda b, h, i: (0, 0)),
        ],
        out_specs=pl.BlockSpec((bq, hw), lambda b, h, i: (b * nq + i, h)),
        out_shape=jax.ShapeDtypeStruct((t, heads * HEAD_DIM), BF16),
        compiler_params=_params("arbitrary", "arbitrary", "arbitrary"),
        name="stickbreak_attn",
    )(qkv, qkv, qkv, tri)


def _rglru_kernel(x_ref, g_ref, cw_ref, cb_ref, wa_ref, ba_ref, wx_ref, bx_ref, lam_ref, o_ref, xp_ref,
                  *, seq, cb, tc, kw):
    pad = SUBLANE
    xp_ref[0:pad, :] = jnp.zeros((pad, cb), F32)
    xp_ref[pad:, :] = x_ref[...]
    log_lam = _log_sigmoid(lam_ref[...])
    sub_iota = lax.broadcasted_iota(jnp.int32, (SUBLANE, cb), 0)
    ngroup = cb // LANE

    def chunk(ci, h):
        t0 = pl.multiple_of(ci * tc, tc)
        win = xp_ref[pl.ds(t0, tc + pad), :]
        xc = cb_ref[...] + cw_ref[0:1, :] * win[pad - kw + 1:pad - kw + 1 + tc, :]
        for k in range(1, kw):
            off = pad - kw + 1 + k
            xc = xc + cw_ref[k:k + 1, :] * win[off:off + tc, :]
        xcb = xc.astype(BF16)
        ra = jnp.concatenate(
            [_dot(xcb[:, q * LANE:(q + 1) * LANE], wa_ref[q].astype(BF16)) for q in range(ngroup)], axis=1)
        rx = jnp.concatenate(
            [_dot(xcb[:, q * LANE:(q + 1) * LANE], wx_ref[q].astype(BF16)) for q in range(ngroup)], axis=1)
        r = _sigmoid(ra + ba_ref[...])
        gi = _sigmoid(rx + bx_ref[...])
        log_a = RG_C * r * log_lam
        a = jnp.exp(log_a)
        u = jnp.sqrt(-jnp.tanh(log_a) * (a * a + 1.0)) * (gi * xc)
        gate = _gelu_tanh(g_ref[pl.ds(t0, tc), :])
        outs = []
        for gidx in range(tc // SUBLANE):
            av = a[gidx * SUBLANE:(gidx + 1) * SUBLANE, :]
            bv = u[gidx * SUBLANE:(gidx + 1) * SUBLANE, :]
            for sh in (1, 2, 4):
                a_s = pltpu.roll(av, sh, axis=0)
                b_s = pltpu.roll(bv, sh, axis=0)
                m = sub_iota >= sh
                bv = jnp.where(m, av * b_s + bv, bv)
                av = jnp.where(m, av * a_s, av)
            hv = av * h + bv
            outs.append(hv)
            h = jnp.broadcast_to(hv[SUBLANE - 1:SUBLANE, :], (SUBLANE, cb))
        hs = jnp.concatenate(outs, axis=0)
        o_ref[pl.ds(t0, tc), :] = (hs * gate).astype(o_ref.dtype)
        return h

    lax.fori_loop(0, seq // tc, chunk, jnp.zeros((SUBLANE, cb), F32))


def _block_diag_pairs(w):
    nblk, c, _ = w.shape
    w2 = w.reshape(nblk // 2, 2, c, c)
    z = jnp.zeros((nblk // 2, c, c), w.dtype)
    top = jnp.concatenate([w2[:, 0], z], axis=2)
    bot = jnp.concatenate([z, w2[:, 1]], axis=2)
    return jnp.concatenate([top, bot], axis=1)


def _rglru(rg, batch, seq, conv_w, conv_b, wa, ba, wx, bx, lam):
    t, c2 = rg.shape
    c = c2 // 2
    cb = min(RG_CB, c)
    tc = min(RG_TC, seq)
    ncb = c // cb
    kw = conv_w.shape[0]
    gpb = cb // LANE
    vec = pl.BlockSpec((1, cb), lambda b, j: (0, j))
    return pl.pallas_call(
        functools.partial(_rglru_kernel, seq=seq, cb=cb, tc=tc, kw=kw),
        grid=(batch, ncb),
        in_specs=[
            pl.BlockSpec((seq, cb), lambda b, j: (b, j)),
            pl.BlockSpec((seq, cb), lambda b, j: (b, ncb + j)),
            pl.BlockSpec((kw, cb), lambda b, j: (0, j)),
            vec,
            pl.BlockSpec((gpb, LANE, LANE), lambda b, j: (j, 0, 0)),
            vec,
            pl.BlockSpec((gpb, LANE, LANE), lambda b, j: (j, 0, 0)),
            vec,
            vec,
        ],
        out_specs=pl.BlockSpec((seq, cb), lambda b, j: (b, j)),
        out_shape=jax.ShapeDtypeStruct((t, c), BF16),
        scratch_shapes=[pltpu.VMEM((seq + SUBLANE, cb), F32)],
        compiler_params=_params("arbitrary", "arbitrary"),
        name="rglru",
    )(rg, rg, conv_w, conv_b.reshape(1, c), _block_diag_pairs(wa), ba.reshape(1, c),
      _block_diag_pairs(wx), bx.reshape(1, c), lam.reshape(1, c))


def _compress_kernel(x_ref, pos_ref, w1_ref, w2_ref, o_ref, *, ngrp):
    st = CMP_STRIDE
    half = st * HEAD_DIM
    xs = [x_ref[pl.ds(r, ngrp, stride=st), :] for r in range(st)]
    pos = pos_ref[0]
    x0 = jnp.concatenate([xs[r] + pos[r:r + 1, :] for r in range(st)], axis=1).astype(BF16)
    x1 = jnp.concatenate([xs[r] + pos[st + r:st + r + 1, :] for r in range(st)], axis=1).astype(BF16)
    p0 = _dot(x0, w1_ref[0, 0:half, :].astype(BF16))
    p1 = _dot(x1, w1_ref[0, half:2 * half, :].astype(BF16))
    pre = p0 + pltpu.roll(p1, ngrp - 1, axis=0)
    out = _dot(_gelu_tanh(pre).astype(BF16), w2_ref[0].astype(BF16))
    rown = lax.broadcasted_iota(jnp.int32, out.shape, 0)
    o_ref[...] = jnp.where(rown < ngrp - 1, out, 0.0).astype(o_ref.dtype)


def _compress(kcvc, batch, seq, pos, w1, w2):
    assert CMP_BLOCK == 2 * CMP_STRIDE
    g2 = kcvc.shape[1] // HEAD_DIM
    per = g2 // 2
    ngrp = seq // CMP_STRIDE
    return pl.pallas_call(
        functools.partial(_compress_kernel, ngrp=ngrp),
        grid=(batch, g2),
        in_specs=[
            pl.BlockSpec((seq, HEAD_DIM), lambda b, j: (b, j)),
            pl.BlockSpec((1, CMP_BLOCK, HEAD_DIM), lambda b, j: (j // per, 0, 0)),
            pl.BlockSpec((1, CMP_BLOCK * HEAD_DIM, HEAD_DIM), lambda b, j: (j // per, 0, 0)),
            pl.BlockSpec((1, HEAD_DIM, HEAD_DIM), lambda b, j: (j // per, 0, 0)),
        ],
        out_specs=pl.BlockSpec((ngrp, HEAD_DIM), lambda b, j: (b * g2 + j, 0)),
        out_shape=jax.ShapeDtypeStruct((batch * g2 * ngrp, HEAD_DIM), BF16),
        compiler_params=_params("arbitrary", "arbitrary"),
        name="nsa_compress",
    )(kcvc, pos, w1, w2)


def _softmax_rows(s, mask):
    m = jnp.max(jnp.where(mask, s, NEG_BIG), axis=-1, keepdims=True)
    e = jnp.where(mask, jnp.exp(s - m), 0.0)
    return e / jnp.maximum(jnp.sum(e, axis=-1, keepdims=True), 1e-30)


def _nsa_kernel(q_ref, kc_ref, vc_ref, ks_ref, vs_ref, kw_ref, vw_ref, g_ref, o_ref,
                *, seq, hg, n_slc, scale):
    bq, bk, dh = NSA_BQ, NSA_BK, HEAD_DIM
    i = pl.program_id(2)
    q0 = i * bq
    qb = q_ref[...]
    qs = jnp.concatenate([qb[:, h * dh:(h + 1) * dh] for h in range(hg)], axis=0)
    qpos1 = q0 + lax.broadcasted_iota(jnp.int32, (bq, 1), 0)
    qpos = jnp.concatenate([qpos1] * hg, axis=0)

    ncmp = kc_ref.shape[0]
    s_c = _dot_nt(qs, kc_ref[...]) * scale
    n_idx = lax.broadcasted_iota(jnp.int32, (1, ncmp), 1)
    p_c = _softmax_rows(s_c, n_idx * CMP_STRIDE + (CMP_BLOCK - 1) <= qpos)
    o_c = _dot(p_c.astype(BF16), vc_ref[...])

    p_sum = p_c[0:bq, :]
    for h in range(1, hg):
        p_sum = p_sum + p_c[h * bq:(h + 1) * bq, :]
    jj = lax.broadcasted_iota(jnp.int32, (LANE, ncmp), 0)
    cstart = lax.broadcasted_iota(jnp.int32, (LANE, ncmp), 1) * CMP_STRIDE
    overlap_t = jnp.where((cstart < (jj + 1) * SLC_BLOCK) & (cstart + CMP_BLOCK > jj * SLC_BLOCK)
                          & (jj < n_slc), 1.0, 0.0).astype(BF16)
    p_hi, p_lo = _split_bf16(p_sum)
    imp_t = (_dot_nt(overlap_t, p_hi) + _dot_nt(overlap_t, p_lo))[0:n_slc, :]

    blk = lax.broadcasted_iota(jnp.int32, (n_slc, bq), 0)
    qp_l = q0 + lax.broadcasted_iota(jnp.int32, (n_slc, bq), 1)
    cur = qp_l // SLC_BLOCK
    forced = (blk == 0) | (blk == cur) | (blk == cur - 1)
    valid = blk * SLC_BLOCK <= qp_l
    rank = jnp.where(valid, imp_t + FORCE_BONUS * jnp.where(forced, 1.0, 0.0), -jnp.inf)
    ahead = jnp.zeros((n_slc, bq), F32)
    for jp in range(n_slc):
        other = rank[jp:jp + 1, :]
        beats = (other > rank) | ((other == rank) & (blk > jp))
        ahead = ahead + jnp.where(beats, 1.0, 0.0)
    k_sel = min(N_SEL, n_slc)
    sel_t = jnp.where(valid & (ahead < k_sel), 1.0, 0.0)
    sel_t = jnp.concatenate([sel_t, jnp.zeros((LANE - n_slc, bq), F32)], axis=0)
    sel = sel_t.T.astype(BF16)

    def sel_tile(t, c):
        m_run, l_run, acc = c
        k0 = pl.multiple_of(t * bk, bk)
        kt = ks_ref[pl.ds(k0, bk), :]
        vt = vs_ref[pl.ds(k0, bk), :]
        s = _dot_nt(qs, kt) * scale
        kpos = k0 + lax.broadcasted_iota(jnp.int32, (LANE, bk), 1)
        expand = jnp.where(kpos // SLC_BLOCK == lax.broadcasted_iota(jnp.int32, (LANE, bk), 0),
                           1.0, 0.0).astype(BF16)
        picked = _dot(sel, expand)
        ok1 = jnp.where((picked > 0.5) & (kpos[0:1, :] <= qpos1), 1.0, 0.0)
        ok = jnp.concatenate([ok1] * hg, axis=0) > 0.5
        m_new = jnp.maximum(m_run, jnp.max(jnp.where(ok, s, NEG_BIG), axis=-1, keepdims=True))
        alpha = jnp.exp(m_run - m_new)
        p = jnp.where(ok, jnp.exp(s - m_new), 0.0)
        l_new = alpha * l_run + jnp.sum(p, axis=-1, keepdims=True)
        acc = alpha * acc + _dot(p.astype(BF16), vt)
        return m_new, l_new, acc

    rows = hg * bq
    init = (jnp.full((rows, 1), NEG_BIG, F32), jnp.zeros((rows, 1), F32), jnp.zeros((rows, dh), F32))
    n_tiles = (q0 + bq + bk - 1) // bk
    _, l_s, acc_s = lax.fori_loop(0, n_tiles, sel_tile, init)
    o_s = acc_s / jnp.maximum(l_s, 1e-30)

    span = min(WINDOW + bq, seq)
    w0 = pl.multiple_of(jnp.maximum(jnp.minimum(q0 - WINDOW, seq - span), 0), bq)
    kwt = kw_ref[pl.ds(w0, span), :]
    vwt = vw_ref[pl.ds(w0, span), :]
    s_w = _dot_nt(qs, kwt) * scale
    kpos_w = w0 + lax.broadcasted_iota(jnp.int32, (1, span), 1)
    p_w = _softmax_rows(s_w, (kpos_w <= qpos) & (kpos_w > qpos - WINDOW))
    o_w = _dot(p_w.astype(BF16), vwt)

    gates = _sigmoid(g_ref[...])
    outs = []
    for h in range(hg):
        r = slice(h * bq, (h + 1) * bq)
        c = h * N_BRANCH
        outs.append(gates[:, c:c + 1] * o_c[r, :] + gates[:, c + 1:c + 2] * o_s[r, :]
                    + gates[:, c + 2:c + 3] * o_w[r, :])
    o_ref[...] = jnp.concatenate(outs, axis=1).astype(o_ref.dtype)


def _nsa_attention(q, kv4, cmp, gate_logits, batch, seq):
    t, width = q.shape
    g = NSA_KV_HEADS
    hg = width // (g * HEAD_DIM)
    bq = NSA_BQ
    nb = seq // bq
    ncmp = seq // CMP_STRIDE
    n_slc = seq // SLC_BLOCK
    dh = HEAD_DIM
    full = lambda off: pl.BlockSpec((seq, dh), lambda b, gi, i: (b, off * g + gi))
    return pl.pallas_call(
        functools.partial(_nsa_kernel, seq=seq, hg=hg, n_slc=n_slc, scale=dh ** -0.5),
        grid=(batch, g, nb),
        in_specs=[
            pl.BlockSpec((bq, hg * dh), lambda b, gi, i: (b * nb + i, gi)),
            pl.BlockSpec((ncmp, dh), lambda b, gi, i: (b * 2 * g + gi, 0)),
            pl.BlockSpec((ncmp, dh), lambda b, gi, i: (b * 2 * g + g + gi, 0)),
            full(0), full(1), full(2), full(3),
            pl.BlockSpec((bq, LANE), lambda b, gi, i: (b * nb + i, gi)),
        ],
        out_specs=pl.BlockSpec((bq, hg * dh), lambda b, gi, i: (b * nb + i, gi)),
        out_shape=jax.ShapeDtypeStruct((t, width), BF16),
        compiler_params=_params("arbitrary", "arbitrary", "arbitrary"),
        name="nsa_attn",
    )(q, cmp, cmp, kv4, kv4, kv4, kv4, gate_logits)


def _gather_kernel(lo_ref, hi_ref, src_ref, h_ref, o_ref, *, chunk):
    i = pl.program_id(0)
    rows = o_ref.shape[0]
    src = src_ref[...]
    lane = lax.broadcasted_iota(jnp.int32, (rows, chunk), 1)
    o_ref[...] = jnp.zeros(o_ref.shape, o_ref.dtype)

    def body(c, carry):
        c0 = pl.multiple_of(c * chunk, chunk)
        onehot = jnp.where(src - c0 == lane, 1.0, 0.0).astype(BF16)
        o_ref[...] += _dot(onehot, h_ref[pl.ds(c0, chunk), :]).astype(o_ref.dtype)
        return carry

    lax.fori_loop(lo_ref[i], hi_ref[i], body, 0)


def _gather_rows(h, src, n_rows):
    t, d = h.shape
    bt, chunk = GATHER_ROWS, GATHER_CHUNK
    nt = n_rows // bt
    src2 = src.reshape(nt, bt)
    live = src2 >= 0
    lo = jnp.min(jnp.where(live, src2, t), axis=1) // chunk
    hi = jnp.where(jnp.any(live, axis=1), jnp.max(src2, axis=1) // chunk + 1, lo)
    grid_spec = pltpu.PrefetchScalarGridSpec(
        num_scalar_prefetch=2,
        grid=(nt,),
        in_specs=[
            pl.BlockSpec((bt, 1), lambda i, lo, hi: (i, 0)),
            pl.BlockSpec((t, d), lambda i, lo, hi: (0, 0), pipeline_mode=pl.Buffered(1)),
        ],
        out_specs=pl.BlockSpec((bt, d), lambda i, lo, hi: (i, 0)),
    )
    return pl.pallas_call(
        functools.partial(_gather_kernel, chunk=chunk),
        grid_spec=grid_spec,
        out_shape=jax.ShapeDtypeStruct((n_rows, d), h.dtype),
        compiler_params=_params("arbitrary"),
        name="moe_dispatch",
    )(jnp.minimum(lo, hi).astype(jnp.int32), hi.astype(jnp.int32), src.reshape(n_rows, 1), h)


def _route(logits, n_experts):
    t = logits.shape[0]
    top_v, top_i = lax.top_k(logits, TOP_K)
    probs = jax.nn.softmax(top_v, axis=-1)
    flat_e = top_i.reshape(-1)
    onehot = (flat_e[:, None] == jnp.arange(n_experts)[None, :]).astype(jnp.int32)
    rank = jnp.take_along_axis(jnp.cumsum(onehot, axis=0), flat_e[:, None], axis=1)[:, 0] - 1
    counts = jnp.sum(onehot, axis=0)
    tiles_per = (counts + FFN_BM - 1) // FFN_BM
    tile_end = jnp.cumsum(tiles_per)
    tile_start = tile_end - tiles_per
    n_tiles = (TOP_K * t) // FFN_BM + n_experts
    dest = (tile_start[flat_e] * FFN_BM + rank).astype(jnp.int32)
    tid = jnp.arange(n_tiles)
    used = tile_end[-1]
    te = jnp.minimum(jnp.searchsorted(tile_end, tid, side="right"), n_experts - 1)
    te_last = jnp.searchsorted(tile_end, used - 1, side="right")
    te = jnp.where(tid < used, te, te_last).astype(jnp.int32)
    nv = jnp.clip(counts[te] - (tid - tile_start[te]) * FFN_BM, 0, FFN_BM)
    nv = jnp.where(tid < used, nv, 0).astype(jnp.int32)
    hblk = jnp.minimum(tid, used - 1).astype(jnp.int32)
    src = jnp.full((n_tiles * FFN_BM,), -1, jnp.int32).at[dest].set(jnp.arange(TOP_K * t, dtype=jnp.int32) // TOP_K)
    return probs, dest.reshape(t, TOP_K), src, te, nv, hblk


def kernel(x, c, ada_mix_w, ada_mix_b, norm_mix_g, ada_ffn_w, ada_ffn_b, norm_ffn_g, even_in_w, rg_conv_w, rg_conv_b, rg_wa, rg_ba, rg_wx, rg_bx, rg_lambda, even_out_w, dense_w_gate, dense_w_up, dense_w_down, nsa_in_w, cmp_pos_k, cmp_pos_v, cmp_k_w1, cmp_k_w2, cmp_v_w1, cmp_v_w2, nsa_out_w, router_w, moe_w_gate, moe_w_up, moe_w_down, final_norm_g):
    batch, seq, d = x.shape
    t = batch * seq
    depth = ada_mix_w.shape[0]
    xf = x.reshape(t, d)

    c_pad = jnp.pad(c, ((0, (-batch) % SUBLANE), (0, 0)))
    m_mix = _adaln(c_pad, ada_mix_w, ada_mix_b)
    m_ffn = _adaln(c_pad, ada_ffn_w, ada_ffn_b)

    def mods(m, layer):
        v = m[layer, :batch].reshape(batch, 1, 3, d)
        return v[:, :, 0], v[:, :, 1], v[:, :, 2]

    pending = None
    for layer in range(depth):
        j = layer // 2
        shift, scale, gate = mods(m_mix, layer)
        xf, h = _normmod(xf, norm_mix_g[layer], seq, res=pending, mod=(shift, scale))
        if layer % 2 == 0:
            w_in = even_in_w[j]
            sbw = (w_in.shape[1] - 2 * rg_conv_w.shape[2]) // 3
            heads = sbw // HEAD_DIM
            qkv = _matmul([h], w_in, 0, 3 * sbw, BF16)
            rg = _matmul([h], w_in, 3 * sbw, w_in.shape[1] - 3 * sbw, F32)
            o_a = _sb_attention(qkv, batch, seq, heads)
            o_b = _rglru(rg, batch, seq, rg_conv_w[j], rg_conv_b[j], rg_wa[j], rg_ba[j], rg_wx[j],
                         rg_bx[j], rg_lambda[j])
            mix = _matmul([o_a, o_b], even_out_w[j], 0, d, F32)
        else:
            w_in = nsa_in_w[j]
            g = NSA_KV_HEADS
            kvw = g * HEAD_DIM
            nsa_w = nsa_out_w.shape[1]
            hg = nsa_w // kvw
            q = _matmul([h], w_in, 0, nsa_w, BF16)
            kcvc = _matmul([h], w_in, nsa_w, 2 * kvw, F32)
            kv4 = _matmul([h], w_in, nsa_w + 2 * kvw, 4 * kvw, BF16)
            wg = w_in[:, nsa_w + 6 * kvw:].reshape(d, g, hg * N_BRANCH)
            wg = jnp.pad(wg, ((0, 0), (0, 0), (0, LANE - hg * N_BRANCH))).reshape(d, g * LANE)
            gl = _matmul([h], wg, 0, g * LANE, F32)
            cmp = _compress(kcvc, batch, seq, jnp.stack([cmp_pos_k[j], cmp_pos_v[j]]),
                            jnp.stack([cmp_k_w1[j], cmp_v_w1[j]]), jnp.stack([cmp_k_w2[j], cmp_v_w2[j]]))
            o = _nsa_attention(q, kv4, cmp, gl, batch, seq)
            mix = _matmul([o], nsa_out_w[j], 0, d, F32)
        pending = (mix, gate)

        shift, scale, gate = mods(m_ffn, layer)
        xf, h = _normmod(xf, norm_ffn_g[layer], seq, res=pending, mod=(shift, scale))
        if layer % 2 == 0:
            nt = t // FFN_BM
            ffn = _ffn(h, dense_w_gate[j:j + 1], dense_w_up[j:j + 1], dense_w_down[j:j + 1],
                       jnp.zeros((nt,), jnp.int32), jnp.full((nt,), FFN_BM, jnp.int32),
                       jnp.arange(nt, dtype=jnp.int32))
        else:
            n_exp = router_w.shape[2]
            rw = jnp.pad(router_w[j], ((0, 0), (0, LANE - n_exp)))
            logits = _matmul([h], rw, 0, LANE, F32)[:, :n_exp]
            probs, dest, src, te, nv, hblk = _route(logits, n_exp)
            h_sorted = _gather_rows(h, src, src.shape[0])
            y = _ffn(h_sorted, moe_w_gate[j], moe_w_up[j], moe_w_down[j], te, nv, hblk)
            ffn = probs[:, 0:1] * jnp.take(y, dest[:, 0], axis=0) + probs[:, 1:2] * jnp.take(y, dest[:, 1], axis=0)
        pending = (ffn, gate)

    _, out = _normmod(xf, final_norm_g, seq, res=pending, mod=None, out_dtype=F32)
    return out.reshape(batch, seq, d)
```

```python
import functools

import jax
import jax.numpy as jnp
from jax import lax
from jax.experimental import pallas as pl
from jax.experimental.pallas import tpu as pltpu

F32 = jnp.float32
BF16 = jnp.bfloat16

LANE = 128
SUBLANE = 8
VMEM_LIMIT_BYTES = 56 * 1024 * 1024

HEAD_DIM = 128
EPS = 1e-6
RG_C = 8.0
NSA_KV_HEADS = 4
N_BRANCH = 3
CMP_BLOCK = 32
CMP_STRIDE = 16
SLC_BLOCK = 64
N_SEL = 8
WINDOW = 512
FORCE_BONUS = 1e6
TOP_K = 2
NEG_BIG = -1e30
LOG2E = 1.4426950408889634
SB_EXP_ZERO = -110.0

ROW_TILE = 256
MM_BM = 1024
MM_BN = 512
FFN_BM = 2048
FFN_SUB = 512
FFN_NCOL = 4
FFN_BF = 256
GATHER_ROWS = 256
GATHER_CHUNK = 512
SB_BQ = 256
SB_HEADS_PER_STEP = 2
NSA_BQ = 128
NSA_BK = 512
RG_CB = 256
RG_TC = 256


def _params(*sem):
    return pltpu.CompilerParams(dimension_semantics=sem, vmem_limit_bytes=VMEM_LIMIT_BYTES)


def _dot(a, b):
    return jnp.dot(a, b, preferred_element_type=F32)


def _dot_nt(a, b):
    return lax.dot_general(a, b, (((1,), (1,)), ((), ())), preferred_element_type=F32)


def _sigmoid(x):
    return 1.0 / (1.0 + jnp.exp(-x))


def _log_sigmoid(x):
    return jnp.minimum(x, 0.0) - jnp.log1p(jnp.exp(-jnp.abs(x)))


def _gelu_tanh(x):
    return 0.5 * x * (1.0 + jnp.tanh(0.7978845608028654 * (x + 0.044715 * (x * x * x))))


def _split_bf16(x):
    hi = x.astype(BF16)
    lo = (x - hi.astype(F32)).astype(BF16)
    return hi, lo


def _adaln_kernel(c_ref, w_ref, b_ref, o_ref):
    c = c_ref[...]
    s = (c * _sigmoid(c)).astype(BF16)
    o_ref[0] = _dot(s, w_ref[0].astype(BF16)) + b_ref[0]


def _adaln(c_pad, w, b, bn=1024):
    depth, d, n3 = w.shape
    rows = c_pad.shape[0]
    return pl.pallas_call(
        _adaln_kernel,
        grid=(depth, n3 // bn),
        in_specs=[
            pl.BlockSpec((rows, d), lambda l, j: (0, 0)),
            pl.BlockSpec((1, d, bn), lambda l, j: (l, 0, j)),
            pl.BlockSpec((1, 1, bn), lambda l, j: (l, 0, j)),
        ],
        out_specs=pl.BlockSpec((1, rows, bn), lambda l, j: (l, 0, j)),
        out_shape=jax.ShapeDtypeStruct((depth, rows, n3), F32),
        compiler_params=_params("arbitrary", "arbitrary"),
        name="adaln",
    )(c_pad, w, b.reshape(depth, 1, n3))


def _normmod_kernel(*refs, has_res, has_mod):
    it = iter(refs)
    x_ref = next(it)
    if has_res:
        y_ref, gate_ref = next(it), next(it)
    g_ref = next(it)
    if has_mod:
        shift_ref, scale_ref = next(it), next(it)
    if has_res:
        xo_ref = next(it)
    h_ref = next(it)

    x = x_ref[...]
    if has_res:
        x = x + gate_ref[0] * y_ref[...]
        xo_ref[...] = x
    ms = jnp.mean(x * x, axis=-1, keepdims=True)
    h = x * lax.rsqrt(ms + EPS) * g_ref[...]
    if has_mod:
        h = h * (1.0 + scale_ref[0]) + shift_ref[0]
    h_ref[...] = h.astype(h_ref.dtype)


def _normmod(x, g, seq, res=None, mod=None, out_dtype=BF16):
    t, d = x.shape
    bt = ROW_TILE
    per_b = seq // bt
    row = pl.BlockSpec((bt, d), lambda i: (i, 0))
    vec = pl.BlockSpec((1, 1, d), lambda i: (i // per_b, 0, 0))
    args, specs = [x], [row]
    if res is not None:
        args += [res[0], res[1]]
        specs += [row, vec]
    args.append(g.reshape(1, d))
    specs.append(pl.BlockSpec((1, d), lambda i: (0, 0)))
    if mod is not None:
        args += [mod[0], mod[1]]
        specs += [vec, vec]
    out_shape, out_specs = [], []
    if res is not None:
        out_shape.append(jax.ShapeDtypeStruct((t, d), F32))
        out_specs.append(row)
    out_shape.append(jax.ShapeDtypeStruct((t, d), out_dtype))
    out_specs.append(row)
    outs = pl.pallas_call(
        functools.partial(_normmod_kernel, has_res=res is not None, has_mod=mod is not None),
        grid=(t // bt,),
        in_specs=specs,
        out_specs=out_specs,
        out_shape=out_shape,
        compiler_params=_params("arbitrary"),
        name="normmod",
    )(*args)
    return outs if res is not None else (x, outs[0])


def _mm_kernel(*refs, ks):
    a_refs = refs[: len(ks)]
    w_ref, o_ref, wb_ref = refs[len(ks):]

    @pl.when(pl.program_id(1) == 0)
    def _():
        wb_ref[...] = w_ref[...].astype(BF16)

    acc = None
    k0 = 0
    for a_ref, k in zip(a_refs, ks):
        part = _dot(a_ref[...], wb_ref[k0:k0 + k, :])
        acc = part if acc is None else acc + part
        k0 += k
    o_ref[...] = acc.astype(o_ref.dtype)


def _matmul(a_list, w, col0, n, out_dtype, bm=MM_BM, bn=MM_BN):
    m = a_list[0].shape[0]
    ks = tuple(a.shape[1] for a in a_list)
    ktot = sum(ks)
    bm = min(bm, m)
    bn = min(bn, n)
    cb0 = col0 // bn
    a_specs = [pl.BlockSpec((bm, k), lambda j, i: (i, 0)) for k in ks]
    return pl.pallas_call(
        functools.partial(_mm_kernel, ks=ks),
        grid=(n // bn, m // bm),
        in_specs=a_specs + [pl.BlockSpec((ktot, bn), lambda j, i: (0, cb0 + j))],
        out_specs=pl.BlockSpec((bm, bn), lambda j, i: (i, j)),
        out_shape=jax.ShapeDtypeStruct((m, n), out_dtype),
        scratch_shapes=[pltpu.VMEM((ktot, bn), BF16)],
        compiler_params=_params("arbitrary", "arbitrary"),
        name="proj_matmul",
    )(*a_list, w)


def _ffn_kernel(te_ref, nv_ref, hb_ref, h_ref, wg_ref, wu_ref, wd_ref, o_ref, wgb, wub, wdb, *, sub, nsub, ncol):
    del te_ref, hb_ref
    s = pl.program_id(0)
    f = pl.program_id(1)
    nv = nv_ref[s]
    bm, d = o_ref.shape
    nchunk = (nv + sub - 1) // sub

    @pl.when(f == 0)
    def _():
        o_ref[...] = jnp.zeros((bm, d), F32)

    @pl.when(nv > 0)
    def _():
        wgb[...] = wg_ref[0].astype(BF16)
        wub[...] = wu_ref[0].astype(BF16)
        wdb[...] = wd_ref[0].astype(BF16)

    for c in range(1, nsub + 1):
        rows = c * sub

        @pl.when(nchunk == c)
        def _():
            h = h_ref[0:rows, :]
            g = _dot(h, wgb[...])
            u = _dot(h, wub[...])
            a = (g * _sigmoid(g) * u).astype(BF16)
            for n in range(ncol):
                cols = slice(n * (d // ncol), (n + 1) * (d // ncol))
                o_ref[0:rows, cols] += _dot(a, wdb[:, cols])


def _ffn(h, w_gate, w_up, w_down, tile_expert, tile_rows, tile_hblk):
    rows, d = h.shape
    _, _, dff = w_gate.shape
    bm, bf, sub = FFN_BM, FFN_BF, FFN_SUB
    nt, nf = rows // bm, dff // bf

    def f_eff(s, f, nv):
        return jnp.where(nv[s] > 0, f, nf - 1)

    grid_spec = pltpu.PrefetchScalarGridSpec(
        num_scalar_prefetch=3,
        grid=(nt, nf),
        in_specs=[
            pl.BlockSpec((bm, d), lambda s, f, te, nv, hb: (hb[s], 0), pipeline_mode=pl.Buffered(1)),
            pl.BlockSpec((1, d, bf), lambda s, f, te, nv, hb: (te[s], 0, f_eff(s, f, nv))),
            pl.BlockSpec((1, d, bf), lambda s, f, te, nv, hb: (te[s], 0, f_eff(s, f, nv))),
            pl.BlockSpec((1, bf, d), lambda s, f, te, nv, hb: (te[s], f_eff(s, f, nv), 0)),
        ],
        out_specs=pl.BlockSpec((bm, d), lambda s, f, te, nv, hb: (s, 0), pipeline_mode=pl.Buffered(1)),
        scratch_shapes=[
            pltpu.VMEM((d, bf), BF16),
            pltpu.VMEM((d, bf), BF16),
            pltpu.VMEM((bf, d), BF16),
        ],
    )
    return pl.pallas_call(
        functools.partial(_ffn_kernel, sub=sub, nsub=bm // sub, ncol=FFN_NCOL),
        grid_spec=grid_spec,
        out_shape=jax.ShapeDtypeStruct((rows, d), F32),
        compiler_params=_params("arbitrary", "arbitrary"),
        name="swiglu_ffn",
    )(tile_expert, tile_rows, tile_hblk, h, w_gate, w_up, w_down)


def _sb_kernel(q_ref, k_ref, v_ref, tri_ref, o_ref, *, bq, nh, scale):
    i = pl.program_id(2)
    dh = HEAD_DIM
    row = lax.broadcasted_iota(jnp.int32, (bq, bq), 0)
    col = lax.broadcasted_iota(jnp.int32, (bq, bq), 1)
    causal = col < row
    qs = [q_ref[:, h * dh:(h + 1) * dh] for h in range(nh)]

    def tile(j, state, diag):
        k0 = pl.multiple_of(j * bq, bq)
        out = []
        for h in range(nh):
            carry, acc = state[2 * h], state[2 * h + 1]
            k = k_ref[pl.ds(k0, bq), h * dh:(h + 1) * dh]
            v = v_ref[pl.ds(k0, bq), h * dh:(h + 1) * dh]
            z = _dot_nt(qs[h], k) * scale
            log_beta = jnp.minimum(z, 0.0) - jnp.log(1.0 + jnp.exp(-jnp.abs(z)))
            log_keep = log_beta - z
            if diag:
                log_keep = jnp.where(causal, log_keep, 0.0)
            hi, lo = _split_bf16(log_keep)
            between = _dot(jnp.concatenate([hi, lo], axis=1), tri_ref[...]) + carry
            w = jnp.exp(log_beta + between)
            if diag:
                w = jnp.where(causal, w, 0.0)
            out.append(carry + jnp.sum(log_keep, axis=-1, keepdims=True))
            out.append(acc + _dot(w.astype(BF16), v))
        return tuple(out)

    state = tuple(jnp.zeros((bq, 1 if n % 2 == 0 else dh), F32) for n in range(2 * nh))
    state = tile(i, state, True)

    def top_carry(st):
        m = jnp.max(st[0])
        for h in range(1, nh):
            m = jnp.maximum(m, jnp.max(st[2 * h]))
        return m

    def cond(c):
        return jnp.logical_and(c[0] < i, c[1] > SB_EXP_ZERO)

    def body(c):
        st = tile(i - 1 - c[0], c[2], False)
        return c[0] + 1, top_carry(st), st

    _, _, state = lax.while_loop(cond, body, (jnp.int32(0), top_carry(state), state))
    o_ref[...] = jnp.concatenate([state[2 * h + 1] for h in range(nh)], axis=1).astype(o_ref.dtype)


def _sb_attention(qkv, batch, seq, heads):
    t = qkv.shape[0]
    bq = min(SB_BQ, seq)
    nq = seq // bq
    nh = SB_HEADS_PER_STEP
    hw = nh * HEAD_DIM
    ng = heads // nh
    rj = lax.broadcasted_iota(jnp.int32, (2 * bq, bq), 0)
    cs = lax.broadcasted_iota(jnp.int32, (2 * bq, bq), 1)
    tri = jnp.where((rj % bq) > cs, 1.0, 0.0).astype(BF16)
    return pl.pallas_call(
        functools.partial(_sb_kernel, bq=bq, nh=nh, scale=HEAD_DIM ** -0.5),
        grid=(batch, ng, nq),
        in_specs=[
            pl.BlockSpec((bq, hw), lambda b, h, i: (b * nq + i, h)),
            pl.BlockSpec((seq, hw), lambda b, h, i: (b, ng + h)),
            pl.BlockSpec((seq, hw), lambda b, h, i: (b, 2 * ng + h)),
            pl.BlockSpec((2 * bq, bq), lambda b, h, i: (0, 0)),
        ],
        out_specs=pl.BlockSpec((bq, hw), lambda b, h, i: (b * nq + i, h)),
        out_shape=jax.ShapeDtypeStruct((t, heads * HEAD_DIM), BF16),
        compiler_params=_params("arbitrary", "arbitrary", "arbitrary"),
        name="stickbreak_attn",
    )(qkv, qkv, qkv, tri)


def _rglru_kernel(x_ref, g_ref, cw_ref, cb_ref, wa_ref, ba_ref, wx_ref, bx_ref, lam_ref, o_ref, xp_ref,
                  *, seq, cb, tc, kw):
    pad = SUBLANE
    xp_ref[0:pad, :] = jnp.zeros((pad, cb), F32)
    xp_ref[pad:, :] = x_ref[...]
    log_lam = _log_sigmoid(lam_ref[...])
    sub_iota = lax.broadcasted_iota(jnp.int32, (SUBLANE, cb), 0)
    ngroup = cb // LANE

    def chunk(ci, h):
        t0 = pl.multiple_of(ci * tc, tc)
        win = xp_ref[pl.ds(t0, tc + pad), :]
        xc = cb_ref[...] + cw_ref[0:1, :] * win[pad - kw + 1:pad - kw + 1 + tc, :]
        for k in range(1, kw):
            off = pad - kw + 1 + k
            xc = xc + cw_ref[k:k + 1, :] * win[off:off + tc, :]
        xcb = xc.astype(BF16)
        ra = jnp.concatenate(
            [_dot(xcb[:, q * LANE:(q + 1) * LANE], wa_ref[q].astype(BF16)) for q in range(ngroup)], axis=1)
        rx = jnp.concatenate(
            [_dot(xcb[:, q * LANE:(q + 1) * LANE], wx_ref[q].astype(BF16)) for q in range(ngroup)], axis=1)
        r = _sigmoid(ra + ba_ref[...])
        gi = _sigmoid(rx + bx_ref[...])
        log_a = RG_C * r * log_lam
        a = jnp.exp(log_a)
        u = jnp.sqrt(-jnp.tanh(log_a) * (a * a + 1.0)) * (gi * xc)
        gate = _gelu_tanh(g_ref[pl.ds(t0, tc), :])
        outs = []
        for gidx in range(tc // SUBLANE):
            av = a[gidx * SUBLANE:(gidx + 1) * SUBLANE, :]
            bv = u[gidx * SUBLANE:(gidx + 1) * SUBLANE, :]
            for sh in (1, 2, 4):
                a_s = pltpu.roll(av, sh, axis=0)
                b_s = pltpu.roll(bv, sh, axis=0)
                m = sub_iota >= sh
                bv = jnp.where(m, av * b_s + bv, bv)
                av = jnp.where(m, av * a_s, av)
            hv = av * h + bv
            outs.append(hv)
            h = jnp.broadcast_to(hv[SUBLANE - 1:SUBLANE, :], (SUBLANE, cb))
        hs = jnp.concatenate(outs, axis=0)
        o_ref[pl.ds(t0, tc), :] = (hs * gate).astype(o_ref.dtype)
        return h

    lax.fori_loop(0, seq // tc, chunk, jnp.zeros((SUBLANE, cb), F32))


def _block_diag_pairs(w):
    nblk, c, _ = w.shape
    w2 = w.reshape(nblk // 2, 2, c, c)
    z = jnp.zeros((nblk // 2, c, c), w.dtype)
    top = jnp.concatenate([w2[:, 0], z], axis=2)
    bot = jnp.concatenate([z, w2[:, 1]], axis=2)
    return jnp.concatenate([top, bot], axis=1)


def _rglru(rg, batch, seq, conv_w, conv_b, wa, ba, wx, bx, lam):
    t, c2 = rg.shape
    c = c2 // 2
    cb = min(RG_CB, c)
    tc = min(RG_TC, seq)
    ncb = c // cb
    kw = conv_w.shape[0]
    gpb = cb // LANE
    vec = pl.BlockSpec((1, cb), lambda b, j: (0, j))
    return pl.pallas_call(
        functools.partial(_rglru_kernel, seq=seq, cb=cb, tc=tc, kw=kw),
        grid=(batch, ncb),
        in_specs=[
            pl.BlockSpec((seq, cb), lambda b, j: (b, j)),
            pl.BlockSpec((seq, cb), lambda b, j: (b, ncb + j)),
            pl.BlockSpec((kw, cb), lambda b, j: (0, j)),
            vec,
            pl.BlockSpec((gpb, LANE, LANE), lambda b, j: (j, 0, 0)),
            vec,
            pl.BlockSpec((gpb, LANE, LANE), lambda b, j: (j, 0, 0)),
            vec,
            vec,
        ],
        out_specs=pl.BlockSpec((seq, cb), lambda b, j: (b, j)),
        out_shape=jax.ShapeDtypeStruct((t, c), BF16),
        scratch_shapes=[pltpu.VMEM((seq + SUBLANE, cb), F32)],
        compiler_params=_params("arbitrary", "arbitrary"),
        name="rglru",
    )(rg, rg, conv_w, conv_b.reshape(1, c), _block_diag_pairs(wa), ba.reshape(1, c),
      _block_diag_pairs(wx), bx.reshape(1, c), lam.reshape(1, c))


def _compress_kernel(x_ref, pos_ref, w1_ref, w2_ref, o_ref, *, ngrp):
    st = CMP_STRIDE
    half = st * HEAD_DIM
    xs = [x_ref[pl.ds(r, ngrp, stride=st), :] for r in range(st)]
    pos = pos_ref[0]
    x0 = jnp.concatenate([xs[r] + pos[r:r + 1, :] for r in range(st)], axis=1).astype(BF16)
    x1 = jnp.concatenate([xs[r] + pos[st + r:st + r + 1, :] for r in range(st)], axis=1).astype(BF16)
    p0 = _dot(x0, w1_ref[0, 0:half, :].astype(BF16))
    p1 = _dot(x1, w1_ref[0, half:2 * half, :].astype(BF16))
    pre = p0 + pltpu.roll(p1, ngrp - 1, axis=0)
    out = _dot(_gelu_tanh(pre).astype(BF16), w2_ref[0].astype(BF16))
    rown = lax.broadcasted_iota(jnp.int32, out.shape, 0)
    o_ref[...] = jnp.where(rown < ngrp - 1, out, 0.0).astype(o_ref.dtype)


def _compress(kcvc, batch, seq, pos, w1, w2):
    assert CMP_BLOCK == 2 * CMP_STRIDE
    g2 = kcvc.shape[1] // HEAD_DIM
    per = g2 // 2
    ngrp = seq // CMP_STRIDE
    return pl.pallas_call(
        functools.partial(_compress_kernel, ngrp=ngrp),
        grid=(batch, g2),
        in_specs=[
            pl.BlockSpec((seq, HEAD_DIM), lambda b, j: (b, j)),
            pl.BlockSpec((1, CMP_BLOCK, HEAD_DIM), lambda b, j: (j // per, 0, 0)),
            pl.BlockSpec((1, CMP_BLOCK * HEAD_DIM, HEAD_DIM), lambda b, j: (j // per, 0, 0)),
            pl.BlockSpec((1, HEAD_DIM, HEAD_DIM), lambda b, j: (j // per, 0, 0)),
        ],
        out_specs=pl.BlockSpec((ngrp, HEAD_DIM), lambda b, j: (b * g2 + j, 0)),
        out_shape=jax.ShapeDtypeStruct((batch * g2 * ngrp, HEAD_DIM), BF16),
        compiler_params=_params("arbitrary", "arbitrary"),
        name="nsa_compress",
    )(kcvc, pos, w1, w2)


def _softmax_parts(s2, bias):
    sb = s2 + bias
    m = jnp.max(sb, axis=-1, keepdims=True)
    e = jnp.exp2(sb - m)
    return e, jnp.sum(e, axis=-1, keepdims=True)


def _nsa_kernel(q_ref, kc_ref, vc_ref, ks_ref, vs_ref, kw_ref, vw_ref, g_ref, ov_ref, ex_ref, o_ref,
                acc_ref, inv_ref, *, seq, hg, n_slc, scale):
    bq, bk, dh = NSA_BQ, NSA_BK, HEAD_DIM
    i = pl.program_id(2)
    q0 = i * bq
    qb = q_ref[...]
    qs = jnp.concatenate([qb[:, h * dh:(h + 1) * dh] for h in range(hg)], axis=0)
    qpos1 = q0 + lax.broadcasted_iota(jnp.int32, (bq, 1), 0)
    qpos = jnp.concatenate([qpos1] * hg, axis=0)

    scale2 = scale * LOG2E
    tile_heads = lambda a: jnp.concatenate([a] * hg, axis=0)

    ncmp = kc_ref.shape[0]
    s_c = _dot_nt(qs, kc_ref[...]) * scale2
    n_idx = lax.broadcasted_iota(jnp.int32, (1, ncmp), 1)
    bias_c = jnp.where(n_idx * CMP_STRIDE + (CMP_BLOCK - 1) <= qpos1, 0.0, NEG_BIG)
    e_c, l_c = _softmax_parts(s_c, tile_heads(bias_c))
    p_c = e_c * jnp.where(qpos >= CMP_BLOCK - 1, 1.0 / l_c, 0.0)
    o_c = _dot(p_c.astype(BF16), vc_ref[...])

    p_sum = p_c[0:bq, :]
    for h in range(1, hg):
        p_sum = p_sum + p_c[h * bq:(h + 1) * bq, :]
    overlap_t = ov_ref[...]
    p_hi, p_lo = _split_bf16(p_sum)
    imp_t = (_dot_nt(overlap_t, p_hi) + _dot_nt(overlap_t, p_lo))[0:n_slc, :]

    blk = lax.broadcasted_iota(jnp.int32, (n_slc, bq), 0)
    qp_l = q0 + lax.broadcasted_iota(jnp.int32, (n_slc, bq), 1)
    cur = qp_l // SLC_BLOCK
    forced = (blk == 0) | (blk == cur) | (blk == cur - 1)
    valid = blk * SLC_BLOCK <= qp_l
    rank = jnp.where(valid, imp_t + FORCE_BONUS * jnp.where(forced, 1.0, 0.0), -jnp.inf)
    ahead = jnp.zeros((n_slc, bq), F32)
    for jp in range(n_slc):
        other = rank[jp:jp + 1, :]
        beats = (other > rank) | ((other == rank) & (blk > jp))
        ahead = ahead + jnp.where(beats, 1.0, 0.0)
    k_sel = min(N_SEL, n_slc)
    sel_t = jnp.where(valid & (ahead < k_sel), 1.0, 0.0)
    sel_t = jnp.concatenate([sel_t, jnp.zeros((LANE - n_slc, bq), F32)], axis=0)
    sel = sel_t.T.astype(BF16)

    n_need = (q0 + bq + bk - 1) // bk
    for v in range(1, seq // bk + 1):
        nk = v * bk

        @pl.when(n_need == v)
        def _():
            s = _dot_nt(qs, ks_ref[0:nk, :]) * scale2
            picked = _dot(sel, ex_ref[:, 0:nk])
            kpos = lax.broadcasted_iota(jnp.int32, (1, nk), 1)
            bias = jnp.where((picked > 0.5) & (kpos <= qpos1), 0.0, NEG_BIG)
            e_s, l_s = _softmax_parts(s, tile_heads(bias))
            acc_ref[...] = _dot(e_s.astype(BF16), vs_ref[0:nk, :])
            inv_ref[...] = 1.0 / l_s

    acc_s = acc_ref[...]
    inv_s = inv_ref[...]

    span = min(WINDOW + bq, seq)
    w0 = pl.multiple_of(jnp.maximum(jnp.minimum(q0 - WINDOW, seq - span), 0), bq)
    kwt = kw_ref[pl.ds(w0, span), :]
    vwt = vw_ref[pl.ds(w0, span), :]
    s_w = _dot_nt(qs, kwt) * scale2
    kpos_w = w0 + lax.broadcasted_iota(jnp.int32, (1, span), 1)
    bias_w = jnp.where((kpos_w <= qpos1) & (kpos_w > qpos1 - WINDOW), 0.0, NEG_BIG)
    e_w, l_w = _softmax_parts(s_w, tile_heads(bias_w))
    acc_w = _dot(e_w.astype(BF16), vwt)
    inv_w = 1.0 / l_w

    gates = _sigmoid(g_ref[...])
    outs = []
    for h in range(hg):
        r = slice(h * bq, (h + 1) * bq)
        c = h * N_BRANCH
        outs.append(gates[:, c:c + 1] * o_c[r, :] + (gates[:, c + 1:c + 2] * inv_s[r, :]) * acc_s[r, :]
                    + (gates[:, c + 2:c + 3] * inv_w[r, :]) * acc_w[r, :])
    o_ref[...] = jnp.concatenate(outs, axis=1).astype(o_ref.dtype)


def _nsa_attention(q, kv4, cmp, gate_logits, batch, seq):
    t, width = q.shape
    g = NSA_KV_HEADS
    hg = width // (g * HEAD_DIM)
    bq = NSA_BQ
    nb = seq // bq
    ncmp = seq // CMP_STRIDE
    n_slc = seq // SLC_BLOCK
    dh = HEAD_DIM
    full = lambda off: pl.BlockSpec((seq, dh), lambda b, gi, i: (b, off * g + gi))
    jj = lax.broadcasted_iota(jnp.int32, (LANE, ncmp), 0)
    cstart = lax.broadcasted_iota(jnp.int32, (LANE, ncmp), 1) * CMP_STRIDE
    overlap_t = jnp.where((cstart < (jj + 1) * SLC_BLOCK) & (cstart + CMP_BLOCK > jj * SLC_BLOCK)
                          & (jj < n_slc), 1.0, 0.0).astype(BF16)
    expand = jnp.where(lax.broadcasted_iota(jnp.int32, (LANE, seq), 1) // SLC_BLOCK
                       == lax.broadcasted_iota(jnp.int32, (LANE, seq), 0), 1.0, 0.0).astype(BF16)
    const = lambda shape: pl.BlockSpec(shape, lambda b, gi, i: (0, 0))
    return pl.pallas_call(
        functools.partial(_nsa_kernel, seq=seq, hg=hg, n_slc=n_slc, scale=dh ** -0.5),
        grid=(batch, g, nb),
        in_specs=[
            pl.BlockSpec((bq, hg * dh), lambda b, gi, i: (b * nb + i, gi)),
            pl.BlockSpec((ncmp, dh), lambda b, gi, i: (b * 2 * g + gi, 0)),
            pl.BlockSpec((ncmp, dh), lambda b, gi, i: (b * 2 * g + g + gi, 0)),
            full(0), full(1), full(2), full(3),
            pl.BlockSpec((bq, LANE), lambda b, gi, i: (b * nb + i, gi)),
            const((LANE, ncmp)),
            const((LANE, seq)),
        ],
        out_specs=pl.BlockSpec((bq, hg * dh), lambda b, gi, i: (b * nb + i, gi)),
        out_shape=jax.ShapeDtypeStruct((t, width), BF16),
        scratch_shapes=[pltpu.VMEM((hg * bq, dh), F32), pltpu.VMEM((hg * bq, 1), F32)],
        compiler_params=_params("arbitrary", "arbitrary", "arbitrary"),
        name="nsa_attn",
    )(q, cmp, cmp, kv4, kv4, kv4, kv4, gate_logits, overlap_t, expand)


def _gather_kernel(lo_ref, hi_ref, src_ref, h_ref, o_ref, *, chunk):
    i = pl.program_id(0)
    rows = o_ref.shape[0]
    src = src_ref[...]
    lane = lax.broadcasted_iota(jnp.int32, (rows, chunk), 1)
    o_ref[...] = jnp.zeros(o_ref.shape, o_ref.dtype)

    def body(c, carry):
        c0 = pl.multiple_of(c * chunk, chunk)
        onehot = jnp.where(src - c0 == lane, 1.0, 0.0).astype(BF16)
        o_ref[...] += _dot(onehot, h_ref[pl.ds(c0, chunk), :]).astype(o_ref.dtype)
        return carry

    lax.fori_loop(lo_ref[i], hi_ref[i], body, 0)


def _gather_rows(h, src, n_rows):
    t, d = h.shape
    bt, chunk = GATHER_ROWS, GATHER_CHUNK
    nt = n_rows // bt
    src2 = src.reshape(nt, bt)
    live = src2 >= 0
    lo = jnp.min(jnp.where(live, src2, t), axis=1) // chunk
    hi = jnp.where(jnp.any(live, axis=1), jnp.max(src2, axis=1) // chunk + 1, lo)
    grid_spec = pltpu.PrefetchScalarGridSpec(
        num_scalar_prefetch=2,
        grid=(nt,),
        in_specs=[
            pl.BlockSpec((bt, 1), lambda i, lo, hi: (i, 0)),
            pl.BlockSpec((t, d), lambda i, lo, hi: (0, 0), pipeline_mode=pl.Buffered(1)),
        ],
        out_specs=pl.BlockSpec((bt, d), lambda i, lo, hi: (i, 0)),
    )
    return pl.pallas_call(
        functools.partial(_gather_kernel, chunk=chunk),
        grid_spec=grid_spec,
        out_shape=jax.ShapeDtypeStruct((n_rows, d), h.dtype),
        compiler_params=_params("arbitrary"),
        name="moe_dispatch",
    )(jnp.minimum(lo, hi).astype(jnp.int32), hi.astype(jnp.int32), src.reshape(n_rows, 1), h)


def _route(logits, n_experts):
    t = logits.shape[0]
    top_v, top_i = lax.top_k(logits, TOP_K)
    probs = jax.nn.softmax(top_v, axis=-1)
    flat_e = top_i.reshape(-1)
    onehot = (flat_e[:, None] == jnp.arange(n_experts)[None, :]).astype(jnp.int32)
    rank = jnp.take_along_axis(jnp.cumsum(onehot, axis=0), flat_e[:, None], axis=1)[:, 0] - 1
    counts = jnp.sum(onehot, axis=0)
    tiles_per = (counts + FFN_BM - 1) // FFN_BM
    tile_end = jnp.cumsum(tiles_per)
    tile_start = tile_end - tiles_per
    n_tiles = (TOP_K * t) // FFN_BM + n_experts
    dest = (tile_start[flat_e] * FFN_BM + rank).astype(jnp.int32)
    tid = jnp.arange(n_tiles)
    used = tile_end[-1]
    te = jnp.minimum(jnp.searchsorted(tile_end, tid, side="right"), n_experts - 1)
    te_last = jnp.searchsorted(tile_end, used - 1, side="right")
    te = jnp.where(tid < used, te, te_last).astype(jnp.int32)
    nv = jnp.clip(counts[te] - (tid - tile_start[te]) * FFN_BM, 0, FFN_BM)
    nv = jnp.where(tid < used, nv, 0).astype(jnp.int32)
    hblk = jnp.minimum(tid, used - 1).astype(jnp.int32)
    src = jnp.full((n_tiles * FFN_BM,), -1, jnp.int32).at[dest].set(jnp.arange(TOP_K * t, dtype=jnp.int32) // TOP_K)
    return probs, dest.reshape(t, TOP_K), src, te, nv, hblk


def kernel(x, c, ada_mix_w, ada_mix_b, norm_mix_g, ada_ffn_w, ada_ffn_b, norm_ffn_g, even_in_w, rg_conv_w, rg_conv_b, rg_wa, rg_ba, rg_wx, rg_bx, rg_lambda, even_out_w, dense_w_gate, dense_w_up, dense_w_down, nsa_in_w, cmp_pos_k, cmp_pos_v, cmp_k_w1, cmp_k_w2, cmp_v_w1, cmp_v_w2, nsa_out_w, router_w, moe_w_gate, moe_w_up, moe_w_down, final_norm_g):
    batch, seq, d = x.shape
    t = batch * seq
    depth = ada_mix_w.shape[0]
    xf = x.reshape(t, d)

    c_pad = jnp.pad(c, ((0, (-batch) % SUBLANE), (0, 0)))
    m_mix = _adaln(c_pad, ada_mix_w, ada_mix_b)
    m_ffn = _adaln(c_pad, ada_ffn_w, ada_ffn_b)

    def mods(m, layer):
        v = m[layer, :batch].reshape(batch, 1, 3, d)
        return v[:, :, 0], v[:, :, 1], v[:, :, 2]

    pending = None
    for layer in range(depth):
        j = layer // 2
        shift, scale, gate = mods(m_mix, layer)
        xf, h = _normmod(xf, norm_mix_g[layer], seq, res=pending, mod=(shift, scale))
        if layer % 2 == 0:
            w_in = even_in_w[j]
            sbw = (w_in.shape[1] - 2 * rg_conv_w.shape[2]) // 3
            heads = sbw // HEAD_DIM
            qkv = _matmul([h], w_in, 0, 3 * sbw, BF16)
            rg = _matmul([h], w_in, 3 * sbw, w_in.shape[1] - 3 * sbw, F32)
            o_a = _sb_attention(qkv, batch, seq, heads)
            o_b = _rglru(rg, batch, seq, rg_conv_w[j], rg_conv_b[j], rg_wa[j], rg_ba[j], rg_wx[j],
                         rg_bx[j], rg_lambda[j])
            mix = _matmul([o_a, o_b], even_out_w[j], 0, d, F32)
        else:
            w_in = nsa_in_w[j]
            g = NSA_KV_HEADS
            kvw = g * HEAD_DIM
            nsa_w = nsa_out_w.shape[1]
            hg = nsa_w // kvw
            q = _matmul([h], w_in, 0, nsa_w, BF16)
            kcvc = _matmul([h], w_in, nsa_w, 2 * kvw, F32)
            kv4 = _matmul([h], w_in, nsa_w + 2 * kvw, 4 * kvw, BF16)
            wg = w_in[:, nsa_w + 6 * kvw:].reshape(d, g, hg * N_BRANCH)
            wg = jnp.pad(wg, ((0, 0), (0, 0), (0, LANE - hg * N_BRANCH))).reshape(d, g * LANE)
            gl = _matmul([h], wg, 0, g * LANE, F32)
            cmp = _compress(kcvc, batch, seq, jnp.stack([cmp_pos_k[j], cmp_pos_v[j]]),
                            jnp.stack([cmp_k_w1[j], cmp_v_w1[j]]), jnp.stack([cmp_k_w2[j], cmp_v_w2[j]]))
            o = _nsa_attention(q, kv4, cmp, gl, batch, seq)
            mix = _matmul([o], nsa_out_w[j], 0, d, F32)
        pending = (mix, gate)

        shift, scale, gate = mods(m_ffn, layer)
        xf, h = _normmod(xf, norm_ffn_g[layer], seq, res=pending, mod=(shift, scale))
        if layer % 2 == 0:
            nt = t // FFN_BM
            ffn = _ffn(h, dense_w_gate[j:j + 1], dense_w_up[j:j + 1], dense_w_down[j:j + 1],
                       jnp.zeros((nt,), jnp.int32), jnp.full((nt,), FFN_BM, jnp.int32),
                       jnp.arange(nt, dtype=jnp.int32))
        else:
            n_exp = router_w.shape[2]
            rw = jnp.pad(router_w[j], ((0, 0), (0, LANE - n_exp)))
            logits = _matmul([h], rw, 0, LANE, F32)[:, :n_exp]
            probs, dest, src, te, nv, hblk = _route(logits, n_exp)
            h_sorted = _gather_rows(h, src, src.shape[0])
            y = _ffn(h_sorted, moe_w_gate[j], moe_w_up[j], moe_w_down[j], te, nv, hblk)
            ffn = probs[:, 0:1] * jnp.take(y, dest[:, 0], axis=0) + probs[:, 1:2] * jnp.take(y, dest[:, 1], axis=0)
        pending = (ffn, gate)

    _, out = _normmod(xf, final_norm_g, seq, res=pending, mod=None, out_dtype=F32)
    return out.reshape(batch, seq, d)
```

```python
import functools

import jax
import jax.numpy as jnp
from jax import lax
from jax.experimental import pallas as pl
from jax.experimental.pallas import tpu as pltpu

F32 = jnp.float32
BF16 = jnp.bfloat16

LANE = 128
SUBLANE = 8
VMEM_LIMIT_BYTES = 56 * 1024 * 1024

HEAD_DIM = 128
EPS = 1e-6
RG_C = 8.0
NSA_KV_HEADS = 4
N_BRANCH = 3
CMP_BLOCK = 32
CMP_STRIDE = 16
SLC_BLOCK = 64
N_SEL = 8
WINDOW = 512
FORCE_BONUS = 1e6
TOP_K = 2
NEG_BIG = -1e30
LOG2E = 1.4426950408889634
SB_EXP_ZERO = -110.0

ROW_TILE = 256
MM_BM = 2048
MM_BN = 512
FFN_BM = 2048
FFN_SUB = 256
FFN_NCOL = 4
FFN_BF = 256
GATHER_ROWS = 256
GATHER_CHUNK = 512
SB_BQ = 256
SB_HEADS_PER_STEP = 2
NSA_BQ = 128
NSA_BK = 512
RG_CB = 256
RG_TC = 256


def _params(*sem):
    return pltpu.CompilerParams(dimension_semantics=sem, vmem_limit_bytes=VMEM_LIMIT_BYTES)


def _dot(a, b):
    return jnp.dot(a, b, preferred_element_type=F32)


def _dot_nt(a, b):
    return lax.dot_general(a, b, (((1,), (1,)), ((), ())), preferred_element_type=F32)


def _sigmoid(x):
    return 1.0 / (1.0 + jnp.exp(-x))


def _log_sigmoid(x):
    return jnp.minimum(x, 0.0) - jnp.log1p(jnp.exp(-jnp.abs(x)))


def _gelu_tanh(x):
    return 0.5 * x * (1.0 + jnp.tanh(0.7978845608028654 * (x + 0.044715 * (x * x * x))))


def _split_bf16(x):
    hi = x.astype(BF16)
    lo = (x - hi.astype(F32)).astype(BF16)
    return hi, lo


def _adaln_kernel(c_ref, w_ref, b_ref, o_ref):
    c = c_ref[...]
    s = (c * _sigmoid(c)).astype(BF16)
    o_ref[0] = _dot(s, w_ref[0].astype(BF16)) + b_ref[0]


def _adaln(c_pad, w, b, bn=1024):
    depth, d, n3 = w.shape
    rows = c_pad.shape[0]
    return pl.pallas_call(
        _adaln_kernel,
        grid=(depth, n3 // bn),
        in_specs=[
            pl.BlockSpec((rows, d), lambda l, j: (0, 0)),
            pl.BlockSpec((1, d, bn), lambda l, j: (l, 0, j)),
            pl.BlockSpec((1, 1, bn), lambda l, j: (l, 0, j)),
        ],
        out_specs=pl.BlockSpec((1, rows, bn), lambda l, j: (l, 0, j)),
        out_shape=jax.ShapeDtypeStruct((depth, rows, n3), F32),
        compiler_params=_params("arbitrary", "arbitrary"),
        name="adaln",
    )(c_pad, w, b.reshape(depth, 1, n3))


def _normmod_kernel(*refs, has_res, has_mod):
    it = iter(refs)
    x_ref = next(it)
    if has_res:
        y_ref, gate_ref = next(it), next(it)
    g_ref = next(it)
    if has_mod:
        shift_ref, scale_ref = next(it), next(it)
    if has_res:
        xo_ref = next(it)
    h_ref = next(it)

    x = x_ref[...]
    if has_res:
        x = x + gate_ref[0] * y_ref[...]
        xo_ref[...] = x
    ms = jnp.mean(x * x, axis=-1, keepdims=True)
    h = x * lax.rsqrt(ms + EPS) * g_ref[...]
    if has_mod:
        h = h * (1.0 + scale_ref[0]) + shift_ref[0]
    h_ref[...] = h.astype(h_ref.dtype)


def _normmod(x, g, seq, res=None, mod=None, out_dtype=BF16):
    t, d = x.shape
    bt = ROW_TILE
    per_b = seq // bt
    row = pl.BlockSpec((bt, d), lambda i: (i, 0))
    vec = pl.BlockSpec((1, 1, d), lambda i: (i // per_b, 0, 0))
    args, specs = [x], [row]
    if res is not None:
        args += [res[0], res[1]]
        specs += [row, vec]
    args.append(g.reshape(1, d))
    specs.append(pl.BlockSpec((1, d), lambda i: (0, 0)))
    if mod is not None:
        args += [mod[0], mod[1]]
        specs += [vec, vec]
    out_shape, out_specs = [], []
    if res is not None:
        out_shape.append(jax.ShapeDtypeStruct((t, d), F32))
        out_specs.append(row)
    out_shape.append(jax.ShapeDtypeStruct((t, d), out_dtype))
    out_specs.append(row)
    outs = pl.pallas_call(
        functools.partial(_normmod_kernel, has_res=res is not None, has_mod=mod is not None),
        grid=(t // bt,),
        in_specs=specs,
        out_specs=out_specs,
        out_shape=out_shape,
        compiler_params=_params("arbitrary"),
        name="normmod",
    )(*args)
    return outs if res is not None else (x, outs[0])


def _mm_kernel(*refs, ks, has_res):
    a_refs = refs[: len(ks)]
    rest = refs[len(ks):]
    w_ref = rest[0]
    if has_res:
        x_ref, gate_ref = rest[1], rest[2]
    o_ref, wb_ref = rest[-2], rest[-1]

    @pl.when(pl.program_id(1) == 0)
    def _():
        wb_ref[...] = w_ref[...].astype(BF16)

    acc = None
    k0 = 0
    for a_ref, k in zip(a_refs, ks):
        part = _dot(a_ref[...], wb_ref[k0:k0 + k, :])
        acc = part if acc is None else acc + part
        k0 += k
    if has_res:
        acc = x_ref[...] + gate_ref[0] * acc
    o_ref[...] = acc.astype(o_ref.dtype)


def _matmul(a_list, w, col0, n, out_dtype, residual=None):
    m = a_list[0].shape[0]
    ks = tuple(a.shape[1] for a in a_list)
    ktot = sum(ks)
    bm = min(MM_BM, m)
    bn = min(MM_BN, n)
    cb0 = col0 // bn
    args = list(a_list) + [w]
    specs = [pl.BlockSpec((bm, k), lambda j, i: (i, 0)) for k in ks]
    specs.append(pl.BlockSpec((ktot, bn), lambda j, i: (0, cb0 + j)))
    if residual is not None:
        x, gate, seq = residual
        bm = min(bm, seq)
        per_b = seq // bm
        specs[: len(ks)] = [pl.BlockSpec((bm, k), lambda j, i: (i, 0)) for k in ks]
        args += [x, gate]
        specs += [pl.BlockSpec((bm, bn), lambda j, i: (i, j)),
                  pl.BlockSpec((1, 1, bn), lambda j, i: (i // per_b, 0, j))]
    return pl.pallas_call(
        functools.partial(_mm_kernel, ks=ks, has_res=residual is not None),
        grid=(n // bn, m // bm),
        in_specs=specs,
        out_specs=pl.BlockSpec((bm, bn), lambda j, i: (i, j)),
        out_shape=jax.ShapeDtypeStruct((m, n), out_dtype),
        scratch_shapes=[pltpu.VMEM((ktot, bn), BF16)],
        compiler_params=_params("arbitrary", "arbitrary"),
        name="proj_matmul",
    )(*args)


def _ffn_kernel(st_ref, se_ref, r0_ref, nr_ref, h_ref, wg_ref, wu_ref, wd_ref, o_ref, wgb, wub, wdb,
                *, sub, nsub, ncol):
    del se_ref
    k = pl.program_id(0)
    f = pl.program_id(1)
    nr = nr_ref[k]
    r0 = pl.multiple_of(r0_ref[k], sub)
    bm, d = o_ref.shape
    nchunk = (nr + sub - 1) // sub
    first = jnp.logical_or(k == 0, st_ref[k] != st_ref[jnp.maximum(k - 1, 0)])

    @pl.when(jnp.logical_and(f == 0, first))
    def _():
        o_ref[...] = jnp.zeros((bm, d), F32)

    @pl.when(nr > 0)
    def _():
        wgb[...] = wg_ref[0].astype(BF16)
        wub[...] = wu_ref[0].astype(BF16)
        wdb[...] = wd_ref[0].astype(BF16)

    for c in range(1, nsub + 1):
        rows = c * sub

        @pl.when(nchunk == c)
        def _():
            h = h_ref[pl.ds(r0, rows), :]
            g = _dot(h, wgb[...])
            u = _dot(h, wub[...])
            a = (g * _sigmoid(g) * u).astype(BF16)
            for n in range(ncol):
                cols = slice(n * (d // ncol), (n + 1) * (d // ncol))
                o_ref[pl.ds(r0, rows), cols] += _dot(a, wdb[:, cols])


def _ffn(h, w_gate, w_up, w_down, seg_tile, seg_expert, seg_row0, seg_rows):
    rows, d = h.shape
    _, _, dff = w_gate.shape
    bm, bf, sub = FFN_BM, FFN_BF, FFN_SUB
    nf = dff // bf
    nseg = seg_tile.shape[0]

    def f_eff(k, f, nr):
        return jnp.where(nr[k] > 0, f, nf - 1)

    grid_spec = pltpu.PrefetchScalarGridSpec(
        num_scalar_prefetch=4,
        grid=(nseg, nf),
        in_specs=[
            pl.BlockSpec((bm, d), lambda k, f, st, se, r0, nr: (st[k], 0), pipeline_mode=pl.Buffered(1)),
            pl.BlockSpec((1, d, bf), lambda k, f, st, se, r0, nr: (se[k], 0, f_eff(k, f, nr))),
            pl.BlockSpec((1, d, bf), lambda k, f, st, se, r0, nr: (se[k], 0, f_eff(k, f, nr))),
            pl.BlockSpec((1, bf, d), lambda k, f, st, se, r0, nr: (se[k], f_eff(k, f, nr), 0)),
        ],
        out_specs=pl.BlockSpec((bm, d), lambda k, f, st, se, r0, nr: (st[k], 0), pipeline_mode=pl.Buffered(1)),
        scratch_shapes=[
            pltpu.VMEM((d, bf), BF16),
            pltpu.VMEM((d, bf), BF16),
            pltpu.VMEM((bf, d), BF16),
        ],
    )
    return pl.pallas_call(
        functools.partial(_ffn_kernel, sub=sub, nsub=bm // sub, ncol=FFN_NCOL),
        grid_spec=grid_spec,
        out_shape=jax.ShapeDtypeStruct((rows, d), F32),
        compiler_params=_params("arbitrary", "arbitrary"),
        name="swiglu_ffn",
    )(seg_tile, seg_expert, seg_row0, seg_rows, h, w_gate, w_up, w_down)


def _sb_kernel(q_ref, k_ref, v_ref, tri_ref, o_ref, *, bq, nh, scale):
    i = pl.program_id(2)
    dh = HEAD_DIM
    row = lax.broadcasted_iota(jnp.int32, (bq, bq), 0)
    col = lax.broadcasted_iota(jnp.int32, (bq, bq), 1)
    causal = col < row
    qs = [q_ref[:, h * dh:(h + 1) * dh] for h in range(nh)]

    def tile(j, state, diag):
        k0 = pl.multiple_of(j * bq, bq)
        out = []
        for h in range(nh):
            carry, acc = state[2 * h], state[2 * h + 1]
            k = k_ref[pl.ds(k0, bq), h * dh:(h + 1) * dh]
            v = v_ref[pl.ds(k0, bq), h * dh:(h + 1) * dh]
            z = _dot_nt(qs[h], k) * scale
            log_beta = jnp.minimum(z, 0.0) - jnp.log(1.0 + jnp.exp(-jnp.abs(z)))
            log_keep = log_beta - z
            if diag:
                log_keep = jnp.where(causal, log_keep, 0.0)
            hi, lo = _split_bf16(log_keep)
            between = _dot(jnp.concatenate([hi, lo], axis=1), tri_ref[...]) + carry
            w = jnp.exp(log_beta + between)
            if diag:
                w = jnp.where(causal, w, 0.0)
            out.append(carry + jnp.sum(log_keep, axis=-1, keepdims=True))
            out.append(acc + _dot(w.astype(BF16), v))
        return tuple(out)

    state = tuple(jnp.zeros((bq, 1 if n % 2 == 0 else dh), F32) for n in range(2 * nh))
    state = tile(i, state, True)

    def top_carry(st):
        m = jnp.max(st[0])
        for h in range(1, nh):
            m = jnp.maximum(m, jnp.max(st[2 * h]))
        return m

    def cond(c):
        return jnp.logical_and(c[0] < i, c[1] > SB_EXP_ZERO)

    def body(c):
        st = tile(i - 1 - c[0], c[2], False)
        return c[0] + 1, top_carry(st), st

    _, _, state = lax.while_loop(cond, body, (jnp.int32(0), top_carry(state), state))
    o_ref[...] = jnp.concatenate([state[2 * h + 1] for h in range(nh)], axis=1).astype(o_ref.dtype)


def _sb_attention(qkv, batch, seq, heads):
    t = qkv.shape[0]
    bq = min(SB_BQ, seq)
    nq = seq // bq
    nh = SB_HEADS_PER_STEP
    hw = nh * HEAD_DIM
    ng = heads // nh
    rj = lax.broadcasted_iota(jnp.int32, (2 * bq, bq), 0)
    cs = lax.broadcasted_iota(jnp.int32, (2 * bq, bq), 1)
    tri = jnp.where((rj % bq) > cs, 1.0, 0.0).astype(BF16)
    return pl.pallas_call(
        functools.partial(_sb_kernel, bq=bq, nh=nh, scale=HEAD_DIM ** -0.5),
        grid=(batch, ng, nq),
        in_specs=[
            pl.BlockSpec((bq, hw), lambda b, h, i: (b * nq + i, h)),
            pl.BlockSpec((seq, hw), lambda b, h, i: (b, ng + h)),
            pl.BlockSpec((seq, hw), lambda b, h, i: (b, 2 * ng + h)),
            pl.BlockSpec((2 * bq, bq), lambda b, h, i: (0, 0)),
        ],
        out_specs=pl.BlockSpec((bq, hw), lambda b, h, i: (b * nq + i, h)),
        out_shape=jax.ShapeDtypeStruct((t, heads * HEAD_DIM), BF16),
        compiler_params=_params("arbitrary", "arbitrary", "arbitrary"),
        name="stickbreak_attn",
    )(qkv, qkv, qkv, tri)


def _rglru_kernel(x_ref, g_ref, cw_ref, cb_ref, wa_ref, ba_ref, wx_ref, bx_ref, lam_ref, o_ref, xp_ref,
                  *, seq, cb, tc, kw):
    pad = SUBLANE
    xp_ref[0:pad, :] = jnp.zeros((pad, cb), F32)
    xp_ref[pad:, :] = x_ref[...]
    log_lam = _log_sigmoid(lam_ref[...])
    sub_iota = lax.broadcasted_iota(jnp.int32, (SUBLANE, cb), 0)
    ngroup = cb // LANE

    def chunk(ci, h):
        t0 = pl.multiple_of(ci * tc, tc)
        win = xp_ref[pl.ds(t0, tc + pad), :]
        xc = cb_ref[...] + cw_ref[0:1, :] * win[pad - kw + 1:pad - kw + 1 + tc, :]
        for k in range(1, kw):
            off = pad - kw + 1 + k
            xc = xc + cw_ref[k:k + 1, :] * win[off:off + tc, :]
        xcb = xc.astype(BF16)
        ra = jnp.concatenate(
            [_dot(xcb[:, q * LANE:(q + 1) * LANE], wa_ref[q].astype(BF16)) for q in range(ngroup)], axis=1)
        rx = jnp.concatenate(
            [_dot(xcb[:, q * LANE:(q + 1) * LANE], wx_ref[q].astype(BF16)) for q in range(ngroup)], axis=1)
        r = _sigmoid(ra + ba_ref[...])
        gi = _sigmoid(rx + bx_ref[...])
        log_a = RG_C * r * log_lam
        a = jnp.exp(log_a)
        u = jnp.sqrt(-jnp.tanh(log_a) * (a * a + 1.0)) * (gi * xc)
        gate = _gelu_tanh(g_ref[pl.ds(t0, tc), :])
        outs = []
        for gidx in range(tc // SUBLANE):
            av = a[gidx * SUBLANE:(gidx + 1) * SUBLANE, :]
            bv = u[gidx * SUBLANE:(gidx + 1) * SUBLANE, :]
            for sh in (1, 2, 4):
                a_s = pltpu.roll(av, sh, axis=0)
                b_s = pltpu.roll(bv, sh, axis=0)
                m = sub_iota >= sh
                bv = jnp.where(m, av * b_s + bv, bv)
                av = jnp.where(m, av * a_s, av)
            hv = av * h + bv
            outs.append(hv)
            h = jnp.broadcast_to(hv[SUBLANE - 1:SUBLANE, :], (SUBLANE, cb))
        hs = jnp.concatenate(outs, axis=0)
        o_ref[pl.ds(t0, tc), :] = (hs * gate).astype(o_ref.dtype)
        return h

    lax.fori_loop(0, seq // tc, chunk, jnp.zeros((SUBLANE, cb), F32))


def _block_diag_pairs(w):
    nblk, c, _ = w.shape
    w2 = w.reshape(nblk // 2, 2, c, c)
    z = jnp.zeros((nblk // 2, c, c), w.dtype)
    top = jnp.concatenate([w2[:, 0], z], axis=2)
    bot = jnp.concatenate([z, w2[:, 1]], axis=2)
    return jnp.concatenate([top, bot], axis=1)


def _rglru(rg, batch, seq, conv_w, conv_b, wa, ba, wx, bx, lam):
    t, c2 = rg.shape
    c = c2 // 2
    cb = min(RG_CB, c)
    tc = min(RG_TC, seq)
    ncb = c // cb
    kw = conv_w.shape[0]
    gpb = cb // LANE
    vec = pl.BlockSpec((1, cb), lambda b, j: (0, j))
    return pl.pallas_call(
        functools.partial(_rglru_kernel, seq=seq, cb=cb, tc=tc, kw=kw),
        grid=(batch, ncb),
        in_specs=[
            pl.BlockSpec((seq, cb), lambda b, j: (b, j)),
            pl.BlockSpec((seq, cb), lambda b, j: (b, ncb + j)),
            pl.BlockSpec((kw, cb), lambda b, j: (0, j)),
            vec,
            pl.BlockSpec((gpb, LANE, LANE), lambda b, j: (j, 0, 0)),
            vec,
            pl.BlockSpec((gpb, LANE, LANE), lambda b, j: (j, 0, 0)),
            vec,
            vec,
        ],
        out_specs=pl.BlockSpec((seq, cb), lambda b, j: (b, j)),
        out_shape=jax.ShapeDtypeStruct((t, c), BF16),
        scratch_shapes=[pltpu.VMEM((seq + SUBLANE, cb), F32)],
        compiler_params=_params("arbitrary", "arbitrary"),
        name="rglru",
    )(rg, rg, conv_w, conv_b.reshape(1, c), _block_diag_pairs(wa), ba.reshape(1, c),
      _block_diag_pairs(wx), bx.reshape(1, c), lam.reshape(1, c))


def _compress_kernel(x_ref, pos_ref, w1_ref, w2_ref, o_ref, *, ngrp):
    st = CMP_STRIDE
    half = st * HEAD_DIM
    xs = [x_ref[pl.ds(r, ngrp, stride=st), :] for r in range(st)]
    pos = pos_ref[0]
    x0 = jnp.concatenate([xs[r] + pos[r:r + 1, :] for r in range(st)], axis=1).astype(BF16)
    x1 = jnp.concatenate([xs[r] + pos[st + r:st + r + 1, :] for r in range(st)], axis=1).astype(BF16)
    p0 = _dot(x0, w1_ref[0, 0:half, :].astype(BF16))
    p1 = _dot(x1, w1_ref[0, half:2 * half, :].astype(BF16))
    pre = p0 + pltpu.roll(p1, ngrp - 1, axis=0)
    out = _dot(_gelu_tanh(pre).astype(BF16), w2_ref[0].astype(BF16))
    rown = lax.broadcasted_iota(jnp.int32, out.shape, 0)
    o_ref[...] = jnp.where(rown < ngrp - 1, out, 0.0).astype(o_ref.dtype)


def _compress(kcvc, batch, seq, pos, w1, w2):
    assert CMP_BLOCK == 2 * CMP_STRIDE
    g2 = kcvc.shape[1] // HEAD_DIM
    per = g2 // 2
    ngrp = seq // CMP_STRIDE
    return pl.pallas_call(
        functools.partial(_compress_kernel, ngrp=ngrp),
        grid=(batch, g2),
        in_specs=[
            pl.BlockSpec((seq, HEAD_DIM), lambda b, j: (b, j)),
            pl.BlockSpec((1, CMP_BLOCK, HEAD_DIM), lambda b, j: (j // per, 0, 0)),
            pl.BlockSpec((1, CMP_BLOCK * HEAD_DIM, HEAD_DIM), lambda b, j: (j // per, 0, 0)),
            pl.BlockSpec((1, HEAD_DIM, HEAD_DIM), lambda b, j: (j // per, 0, 0)),
        ],
        out_specs=pl.BlockSpec((ngrp, HEAD_DIM), lambda b, j: (b * g2 + j, 0)),
        out_shape=jax.ShapeDtypeStruct((batch * g2 * ngrp, HEAD_DIM), BF16),
        compiler_params=_params("arbitrary", "arbitrary"),
        name="nsa_compress",
    )(kcvc, pos, w1, w2)


def _softmax_parts(s2, bias):
    sb = s2 + bias
    m = jnp.max(sb, axis=-1, keepdims=True)
    e = jnp.exp2(sb - m)
    return e, jnp.sum(e, axis=-1, keepdims=True)


def _nsa_kernel(q_ref, kc_ref, vc_ref, ks_ref, vs_ref, kw_ref, vw_ref, g_ref, ov_ref, ex_ref, o_ref,
                acc_ref, inv_ref, *, seq, hg, n_slc, scale):
    bq, bk, dh = NSA_BQ, NSA_BK, HEAD_DIM
    i = pl.program_id(2)
    q0 = i * bq
    qb = q_ref[...]
    qs = jnp.concatenate([qb[:, h * dh:(h + 1) * dh] for h in range(hg)], axis=0)
    qpos1 = q0 + lax.broadcasted_iota(jnp.int32, (bq, 1), 0)
    qpos = jnp.concatenate([qpos1] * hg, axis=0)

    scale2 = scale * LOG2E
    tile_heads = lambda a: jnp.concatenate([a] * hg, axis=0)

    ncmp = kc_ref.shape[0]
    s_c = _dot_nt(qs, kc_ref[...]) * scale2
    n_idx = lax.broadcasted_iota(jnp.int32, (1, ncmp), 1)
    bias_c = jnp.where(n_idx * CMP_STRIDE + (CMP_BLOCK - 1) <= qpos1, 0.0, NEG_BIG)
    e_c, l_c = _softmax_parts(s_c, tile_heads(bias_c))
    p_c = e_c * jnp.where(qpos >= CMP_BLOCK - 1, 1.0 / l_c, 0.0)
    o_c = _dot(p_c.astype(BF16), vc_ref[...])

    p_sum = p_c[0:bq, :]
    for h in range(1, hg):
        p_sum = p_sum + p_c[h * bq:(h + 1) * bq, :]
    overlap_t = ov_ref[...]
    p_hi, p_lo = _split_bf16(p_sum)
    imp_t = (_dot_nt(overlap_t, p_hi) + _dot_nt(overlap_t, p_lo))[0:n_slc, :]

    blk = lax.broadcasted_iota(jnp.int32, (n_slc, bq), 0)
    qp_l = q0 + lax.broadcasted_iota(jnp.int32, (n_slc, bq), 1)
    cur = qp_l // SLC_BLOCK
    forced = (blk == 0) | (blk == cur) | (blk == cur - 1)
    valid = blk * SLC_BLOCK <= qp_l
    rank = jnp.where(valid, imp_t + FORCE_BONUS * jnp.where(forced, 1.0, 0.0), -jnp.inf)
    ahead = jnp.zeros((n_slc, bq), F32)
    for jp in range(n_slc):
        other = rank[jp:jp + 1, :]
        beats = (other > rank) | ((other == rank) & (blk > jp))
        ahead = ahead + jnp.where(beats, 1.0, 0.0)
    k_sel = min(N_SEL, n_slc)
    sel_t = jnp.where(valid & (ahead < k_sel), 1.0, 0.0)
    sel_t = jnp.concatenate([sel_t, jnp.zeros((LANE - n_slc, bq), F32)], axis=0)
    sel = sel_t.T.astype(BF16)

    n_need = (q0 + bq + bk - 1) // bk
    for v in range(1, seq // bk + 1):
        nk = v * bk

        @pl.when(n_need == v)
        def _():
            s = _dot_nt(qs, ks_ref[0:nk, :]) * scale2
            picked = _dot(sel, ex_ref[:, 0:nk])
            kpos = lax.broadcasted_iota(jnp.int32, (1, nk), 1)
            bias = jnp.where((picked > 0.5) & (kpos <= qpos1), 0.0, NEG_BIG)
            e_s, l_s = _softmax_parts(s, tile_heads(bias))
            acc_ref[...] = _dot(e_s.astype(BF16), vs_ref[0:nk, :])
            inv_ref[...] = 1.0 / l_s

    acc_s = acc_ref[...]
    inv_s = inv_ref[...]

    span = min(WINDOW + bq, seq)
    w0 = pl.multiple_of(jnp.maximum(jnp.minimum(q0 - WINDOW, seq - span), 0), bq)
    kwt = kw_ref[pl.ds(w0, span), :]
    vwt = vw_ref[pl.ds(w0, span), :]
    s_w = _dot_nt(qs, kwt) * scale2
    kpos_w = w0 + lax.broadcasted_iota(jnp.int32, (1, span), 1)
    bias_w = jnp.where((kpos_w <= qpos1) & (kpos_w > qpos1 - WINDOW), 0.0, NEG_BIG)
    e_w, l_w = _softmax_parts(s_w, tile_heads(bias_w))
    acc_w = _dot(e_w.astype(BF16), vwt)
    inv_w = 1.0 / l_w

    gates = _sigmoid(g_ref[...])
    outs = []
    for h in range(hg):
        r = slice(h * bq, (h + 1) * bq)
        c = h * N_BRANCH
        outs.append(gates[:, c:c + 1] * o_c[r, :] + (gates[:, c + 1:c + 2] * inv_s[r, :]) * acc_s[r, :]
                    + (gates[:, c + 2:c + 3] * inv_w[r, :]) * acc_w[r, :])
    o_ref[...] = jnp.concatenate(outs, axis=1).astype(o_ref.dtype)


def _nsa_attention(q, kv4, cmp, gate_logits, batch, seq):
    t, width = q.shape
    g = NSA_KV_HEADS
    hg = width // (g * HEAD_DIM)
    bq = NSA_BQ
    nb = seq // bq
    ncmp = seq // CMP_STRIDE
    n_slc = seq // SLC_BLOCK
    dh = HEAD_DIM
    full = lambda off: pl.BlockSpec((seq, dh), lambda b, gi, i: (b, off * g + gi))
    jj = lax.broadcasted_iota(jnp.int32, (LANE, ncmp), 0)
    cstart = lax.broadcasted_iota(jnp.int32, (LANE, ncmp), 1) * CMP_STRIDE
    overlap_t = jnp.where((cstart < (jj + 1) * SLC_BLOCK) & (cstart + CMP_BLOCK > jj * SLC_BLOCK)
                          & (jj < n_slc), 1.0, 0.0).astype(BF16)
    expand = jnp.where(lax.broadcasted_iota(jnp.int32, (LANE, seq), 1) // SLC_BLOCK
                       == lax.broadcasted_iota(jnp.int32, (LANE, seq), 0), 1.0, 0.0).astype(BF16)
    const = lambda shape: pl.BlockSpec(shape, lambda b, gi, i: (0, 0))
    return pl.pallas_call(
        functools.partial(_nsa_kernel, seq=seq, hg=hg, n_slc=n_slc, scale=dh ** -0.5),
        grid=(batch, g, nb),
        in_specs=[
            pl.BlockSpec((bq, hg * dh), lambda b, gi, i: (b * nb + i, gi)),
            pl.BlockSpec((ncmp, dh), lambda b, gi, i: (b * 2 * g + gi, 0)),
            pl.BlockSpec((ncmp, dh), lambda b, gi, i: (b * 2 * g + g + gi, 0)),
            full(0), full(1), full(2), full(3),
            pl.BlockSpec((bq, LANE), lambda b, gi, i: (b * nb + i, gi)),
            const((LANE, ncmp)),
            const((LANE, seq)),
        ],
        out_specs=pl.BlockSpec((bq, hg * dh), lambda b, gi, i: (b * nb + i, gi)),
        out_shape=jax.ShapeDtypeStruct((t, width), BF16),
        scratch_shapes=[pltpu.VMEM((hg * bq, dh), F32), pltpu.VMEM((hg * bq, 1), F32)],
        compiler_params=_params("arbitrary", "arbitrary", "arbitrary"),
        name="nsa_attn",
    )(q, cmp, cmp, kv4, kv4, kv4, kv4, gate_logits, overlap_t, expand)


def _gather_kernel(lo_ref, hi_ref, src_ref, h_ref, o_ref, *, chunk):
    i = pl.program_id(0)
    rows = o_ref.shape[0]
    src = src_ref[...]
    lane = lax.broadcasted_iota(jnp.int32, (rows, chunk), 1)
    o_ref[...] = jnp.zeros(o_ref.shape, o_ref.dtype)

    def body(c, carry):
        c0 = pl.multiple_of(c * chunk, chunk)
        onehot = jnp.where(src - c0 == lane, 1.0, 0.0).astype(BF16)
        o_ref[...] += _dot(onehot, h_ref[pl.ds(c0, chunk), :]).astype(o_ref.dtype)
        return carry

    lax.fori_loop(lo_ref[i], hi_ref[i], body, 0)


def _gather_rows(h, src, n_rows):
    t, d = h.shape
    bt, chunk = GATHER_ROWS, GATHER_CHUNK
    nt = n_rows // bt
    src2 = src.reshape(nt, bt)
    live = src2 >= 0
    lo = jnp.min(jnp.where(live, src2, t), axis=1) // chunk
    hi = jnp.where(jnp.any(live, axis=1), jnp.max(src2, axis=1) // chunk + 1, lo)
    grid_spec = pltpu.PrefetchScalarGridSpec(
        num_scalar_prefetch=2,
        grid=(nt,),
        in_specs=[
            pl.BlockSpec((bt, 1), lambda i, lo, hi: (i, 0)),
            pl.BlockSpec((t, d), lambda i, lo, hi: (0, 0), pipeline_mode=pl.Buffered(1)),
        ],
        out_specs=pl.BlockSpec((bt, d), lambda i, lo, hi: (i, 0)),
    )
    return pl.pallas_call(
        functools.partial(_gather_kernel, chunk=chunk),
        grid_spec=grid_spec,
        out_shape=jax.ShapeDtypeStruct((n_rows, d), h.dtype),
        compiler_params=_params("arbitrary"),
        name="moe_dispatch",
    )(jnp.minimum(lo, hi).astype(jnp.int32), hi.astype(jnp.int32), src.reshape(n_rows, 1), h)


def _router_kernel(l_ref, o_ref, *, n_exp):
    x = l_ref[...]
    lane = lax.broadcasted_iota(jnp.int32, x.shape, 1)
    xm = jnp.where(lane < n_exp, x, -jnp.inf)
    v0 = jnp.max(xm, axis=-1, keepdims=True)
    i0 = jnp.min(jnp.where(xm == v0, lane, LANE), axis=-1, keepdims=True)
    xm = jnp.where(lane == i0, -jnp.inf, xm)
    v1 = jnp.max(xm, axis=-1, keepdims=True)
    i1 = jnp.min(jnp.where(xm == v1, lane, LANE), axis=-1, keepdims=True)
    e = jnp.exp(v1 - v0)
    p0 = 1.0 / (1.0 + e)
    out = jnp.where(lane == 0, p0, jnp.where(lane == 1, e * p0, jnp.where(
        lane == 2, i0.astype(F32), jnp.where(lane == 3, i1.astype(F32), 0.0))))
    o_ref[...] = out


def _router_top2(logits, n_exp):
    t = logits.shape[0]
    bt = ROW_TILE
    out = pl.pallas_call(
        functools.partial(_router_kernel, n_exp=n_exp),
        grid=(t // bt,),
        in_specs=[pl.BlockSpec((bt, LANE), lambda i: (i, 0))],
        out_specs=pl.BlockSpec((bt, LANE), lambda i: (i, 0)),
        out_shape=jax.ShapeDtypeStruct((t, LANE), F32),
        compiler_params=_params("arbitrary"),
        name="router_top2",
    )(logits)
    return out[:, 0:TOP_K], out[:, TOP_K:2 * TOP_K].astype(jnp.int32)


def _route(probs, top_i, n_experts):
    t = top_i.shape[0]
    flat_e = top_i.reshape(-1)
    onehot = (flat_e[:, None] == jnp.arange(n_experts)[None, :]).astype(jnp.int32)
    rank = jnp.take_along_axis(jnp.cumsum(onehot, axis=0), flat_e[:, None], axis=1)[:, 0] - 1
    counts = jnp.sum(onehot, axis=0)
    padded = (counts + FFN_SUB - 1) // FFN_SUB * FFN_SUB
    end = jnp.cumsum(padded)
    start = end - padded
    n_rows = -(-(TOP_K * t + n_experts * (FFN_SUB - 1)) // FFN_BM) * FFN_BM
    n_tiles = n_rows // FFN_BM
    dest = (start[flat_e] + rank).astype(jnp.int32)
    src = jnp.full((n_rows,), -1, jnp.int32).at[dest].set(jnp.arange(TOP_K * t, dtype=jnp.int32) // TOP_K)
    cuts = jnp.sort(jnp.concatenate([start, jnp.arange(n_tiles) * FFN_BM]))
    nxt = jnp.concatenate([cuts[1:], jnp.full((1,), n_rows, cuts.dtype)])
    seg_rows = jnp.maximum(jnp.minimum(nxt, end[-1]) - cuts, 0)
    seg_tile = jnp.minimum(cuts // FFN_BM, n_tiles - 1)
    seg_expert = jnp.minimum(jnp.searchsorted(end, cuts, side="right"), n_experts - 1)
    i32 = lambda a: a.astype(jnp.int32)
    return probs, dest.reshape(t, TOP_K), src, i32(seg_tile), i32(seg_expert), i32(cuts % FFN_BM), i32(seg_rows)


def kernel(x, c, ada_mix_w, ada_mix_b, norm_mix_g, ada_ffn_w, ada_ffn_b, norm_ffn_g, even_in_w, rg_conv_w, rg_conv_b, rg_wa, rg_ba, rg_wx, rg_bx, rg_lambda, even_out_w, dense_w_gate, dense_w_up, dense_w_down, nsa_in_w, cmp_pos_k, cmp_pos_v, cmp_k_w1, cmp_k_w2, cmp_v_w1, cmp_v_w2, nsa_out_w, router_w, moe_w_gate, moe_w_up, moe_w_down, final_norm_g):
    batch, seq, d = x.shape
    t = batch * seq
    depth = ada_mix_w.shape[0]
    xf = x.reshape(t, d)

    c_pad = jnp.pad(c, ((0, (-batch) % SUBLANE), (0, 0)))
    m_mix = _adaln(c_pad, ada_mix_w, ada_mix_b)
    m_ffn = _adaln(c_pad, ada_ffn_w, ada_ffn_b)

    def mods(m, layer):
        v = m[layer, :batch].reshape(batch, 1, 3, d)
        return v[:, :, 0], v[:, :, 1], v[:, :, 2]

    pending = None
    for layer in range(depth):
        j = layer // 2
        shift, scale, gate = mods(m_mix, layer)
        xf, h = _normmod(xf, norm_mix_g[layer], seq, res=pending, mod=(shift, scale))
        if layer % 2 == 0:
            w_in = even_in_w[j]
            sbw = (w_in.shape[1] - 2 * rg_conv_w.shape[2]) // 3
            heads = sbw // HEAD_DIM
            qkv = _matmul([h], w_in, 0, 3 * sbw, BF16)
            rg = _matmul([h], w_in, 3 * sbw, w_in.shape[1] - 3 * sbw, F32)
            o_a = _sb_attention(qkv, batch, seq, heads)
            o_b = _rglru(rg, batch, seq, rg_conv_w[j], rg_conv_b[j], rg_wa[j], rg_ba[j], rg_wx[j],
                         rg_bx[j], rg_lambda[j])
            xf = _matmul([o_a, o_b], even_out_w[j], 0, d, F32, residual=(xf, gate, seq))
        else:
            w_in = nsa_in_w[j]
            g = NSA_KV_HEADS
            kvw = g * HEAD_DIM
            nsa_w = nsa_out_w.shape[1]
            hg = nsa_w // kvw
            q = _matmul([h], w_in, 0, nsa_w, BF16)
            kcvc = _matmul([h], w_in, nsa_w, 2 * kvw, F32)
            kv4 = _matmul([h], w_in, nsa_w + 2 * kvw, 4 * kvw, BF16)
            wg = w_in[:, nsa_w + 6 * kvw:].reshape(d, g, hg * N_BRANCH)
            wg = jnp.pad(wg, ((0, 0), (0, 0), (0, LANE - hg * N_BRANCH))).reshape(d, g * LANE)
            gl = _matmul([h], wg, 0, g * LANE, F32)
            cmp = _compress(kcvc, batch, seq, jnp.stack([cmp_pos_k[j], cmp_pos_v[j]]),
                            jnp.stack([cmp_k_w1[j], cmp_v_w1[j]]), jnp.stack([cmp_k_w2[j], cmp_v_w2[j]]))
            o = _nsa_attention(q, kv4, cmp, gl, batch, seq)
            xf = _matmul([o], nsa_out_w[j], 0, d, F32, residual=(xf, gate, seq))
        pending = None

        shift, scale, gate = mods(m_ffn, layer)
        xf, h = _normmod(xf, norm_ffn_g[layer], seq, res=pending, mod=(shift, scale))
        if layer % 2 == 0:
            nt = t // FFN_BM
            ffn = _ffn(h, dense_w_gate[j:j + 1], dense_w_up[j:j + 1], dense_w_down[j:j + 1],
                       jnp.arange(nt, dtype=jnp.int32), jnp.zeros((nt,), jnp.int32),
                       jnp.zeros((nt,), jnp.int32), jnp.full((nt,), FFN_BM, jnp.int32))
        else:
            n_exp = router_w.shape[2]
            rw = jnp.pad(router_w[j], ((0, 0), (0, LANE - n_exp)))
            probs, top_i = _router_top2(_matmul([h], rw, 0, LANE, F32), n_exp)
            probs, dest, src, seg_tile, seg_expert, seg_row0, seg_rows = _route(probs, top_i, n_exp)
            h_sorted = _gather_rows(h, src, src.shape[0])
            y = _ffn(h_sorted, moe_w_gate[j], moe_w_up[j], moe_w_down[j], seg_tile, seg_expert, seg_row0,
                     seg_rows)
            ffn = probs[:, 0:1] * jnp.take(y, dest[:, 0], axis=0) + probs[:, 1:2] * jnp.take(y, dest[:, 1], axis=0)
        pending = (ffn, gate)

    _, out = _normmod(xf, final_norm_g, seq, res=pending, mod=None, out_dtype=F32)
    return out.reshape(batch, seq, d)
```

```python
import functools

import jax
import jax.numpy as jnp
from jax import lax
from jax.experimental import pallas as pl
from jax.experimental.pallas import tpu as pltpu

F32 = jnp.float32
BF16 = jnp.bfloat16

LANE = 128
SUBLANE = 8
VMEM_LIMIT_BYTES = 56 * 1024 * 1024

HEAD_DIM = 128
EPS = 1e-6
RG_C = 8.0
NSA_KV_HEADS = 4
N_BRANCH = 3
CMP_BLOCK = 32
CMP_STRIDE = 16
SLC_BLOCK = 64
N_SEL = 8
WINDOW = 512
FORCE_BONUS = 1e6
TOP_K = 2
NEG_BIG = -1e30
LOG2E = 1.4426950408889634
SB_EXP_ZERO = -110.0

ROW_TILE = 512
COMBINE_ROWS = 256
MM_BM = 2048
MM_BN = 512
FFN_BM = 2048
FFN_SUB = 256
FFN_NCOL = 4
FFN_BF = 256
GATHER_ROWS = 256
GATHER_CHUNK = 512
SB_BQ = 256
SB_HEADS_PER_STEP = 4
NSA_BQ = 128
NSA_BK = 512
RG_CB = 256
RG_TC = 256


def _params(*sem):
    return pltpu.CompilerParams(dimension_semantics=sem, vmem_limit_bytes=VMEM_LIMIT_BYTES)


def _dot(a, b):
    return jnp.dot(a, b, preferred_element_type=F32)


def _dot_nt(a, b):
    return lax.dot_general(a, b, (((1,), (1,)), ((), ())), preferred_element_type=F32)


def _sigmoid(x):
    return 1.0 / (1.0 + jnp.exp(-x))


def _log_sigmoid(x):
    return jnp.minimum(x, 0.0) - jnp.log1p(jnp.exp(-jnp.abs(x)))


def _gelu_tanh(x):
    return 0.5 * x * (1.0 + jnp.tanh(0.7978845608028654 * (x + 0.044715 * (x * x * x))))


def _split_bf16(x):
    hi = x.astype(BF16)
    lo = (x - hi.astype(F32)).astype(BF16)
    return hi, lo


def _adaln_kernel(c_ref, w_ref, b_ref, o_ref):
    c = c_ref[...]
    s = (c * _sigmoid(c)).astype(BF16)
    o_ref[0] = _dot(s, w_ref[0].astype(BF16)) + b_ref[0]


def _adaln(c_pad, w, b, bn=1024):
    depth, d, n3 = w.shape
    rows = c_pad.shape[0]
    return pl.pallas_call(
        _adaln_kernel,
        grid=(depth, n3 // bn),
        in_specs=[
            pl.BlockSpec((rows, d), lambda l, j: (0, 0)),
            pl.BlockSpec((1, d, bn), lambda l, j: (l, 0, j)),
            pl.BlockSpec((1, 1, bn), lambda l, j: (l, 0, j)),
        ],
        out_specs=pl.BlockSpec((1, rows, bn), lambda l, j: (l, 0, j)),
        out_shape=jax.ShapeDtypeStruct((depth, rows, n3), F32),
        compiler_params=_params("arbitrary", "arbitrary"),
        name="adaln",
    )(c_pad, w, b.reshape(depth, 1, n3))


def _normmod_kernel(*refs, has_res, has_mod):
    it = iter(refs)
    x_ref = next(it)
    if has_res:
        y_ref, gate_ref = next(it), next(it)
    g_ref = next(it)
    if has_mod:
        shift_ref, scale_ref = next(it), next(it)
    if has_res:
        xo_ref = next(it)
    h_ref = next(it)

    x = x_ref[...]
    if has_res:
        x = x + gate_ref[0] * y_ref[...]
        xo_ref[...] = x
    ms = jnp.mean(x * x, axis=-1, keepdims=True)
    h = x * lax.rsqrt(ms + EPS) * g_ref[...]
    if has_mod:
        h = h * (1.0 + scale_ref[0]) + shift_ref[0]
    h_ref[...] = h.astype(h_ref.dtype)


def _normmod(x, g, seq, res=None, mod=None, out_dtype=BF16):
    t, d = x.shape
    bt = ROW_TILE
    per_b = seq // bt
    row = pl.BlockSpec((bt, d), lambda i: (i, 0))
    vec = pl.BlockSpec((1, 1, d), lambda i: (i // per_b, 0, 0))
    args, specs = [x], [row]
    if res is not None:
        args += [res[0], res[1]]
        specs += [row, vec]
    args.append(g.reshape(1, d))
    specs.append(pl.BlockSpec((1, d), lambda i: (0, 0)))
    if mod is not None:
        args += [mod[0], mod[1]]
        specs += [vec, vec]
    out_shape, out_specs = [], []
    if res is not None:
        out_shape.append(jax.ShapeDtypeStruct((t, d), F32))
        out_specs.append(row)
    out_shape.append(jax.ShapeDtypeStruct((t, d), out_dtype))
    out_specs.append(row)
    outs = pl.pallas_call(
        functools.partial(_normmod_kernel, has_res=res is not None, has_mod=mod is not None),
        grid=(t // bt,),
        in_specs=specs,
        out_specs=out_specs,
        out_shape=out_shape,
        compiler_params=_params("arbitrary"),
        name="normmod",
    )(*args)
    return outs if res is not None else (x, outs[0])


def _combine_norm_kernel(dest_ref, *refs, has_mod, bt, n_steps):
    it = iter(refs)
    x_ref, y_hbm, p_ref, gate_ref, g_ref = (next(it) for _ in range(5))
    if has_mod:
        shift_ref, scale_ref, xo_ref = next(it), next(it), next(it)
    h_ref, buf, sem = next(it), next(it), next(it)
    i = pl.program_id(0)

    def row_copy(src_row, slot, k, r):
        return pltpu.make_async_copy(y_hbm.at[pl.ds(src_row, 1), :], buf.at[slot, k, pl.ds(r, 1), :],
                                     sem.at[slot])

    def start_tile(tile, slot):
        base = tile * (bt * TOP_K)

        def body(r, carry):
            for k in range(TOP_K):
                row_copy(dest_ref[base + r * TOP_K + k], slot, k, r).start()
            return carry

        lax.fori_loop(0, bt, body, 0)

    @pl.when(i == 0)
    def _():
        start_tile(0, 0)

    @pl.when(i + 1 < n_steps)
    def _():
        start_tile(i + 1, (i + 1) % 2)

    slot = i % 2

    def wait_row(r, carry):
        for k in range(TOP_K):
            row_copy(0, slot, k, r).wait()
        return carry

    lax.fori_loop(0, bt, wait_row, 0)
    p = p_ref[...]
    y = p[:, 0:1] * buf[slot, 0]
    for k in range(1, TOP_K):
        y = y + p[:, k:k + 1] * buf[slot, k]
    x = x_ref[...] + gate_ref[0] * y
    ms = jnp.mean(x * x, axis=-1, keepdims=True)
    h = x * lax.rsqrt(ms + EPS) * g_ref[...]
    if has_mod:
        xo_ref[...] = x
        h = h * (1.0 + scale_ref[0]) + shift_ref[0]
    h_ref[...] = h.astype(h_ref.dtype)


def _combine_norm(x, y, dest, probs, gate, g, seq, mod=None, out_dtype=BF16):
    t, d = x.shape
    bt = COMBINE_ROWS
    n_steps = t // bt
    per_b = seq // bt
    row = pl.BlockSpec((bt, d), lambda i, dst: (i, 0))
    vec = pl.BlockSpec((1, 1, d), lambda i, dst: (i // per_b, 0, 0))
    args = [x, y, probs, gate, g.reshape(1, d)]
    specs = [row, pl.BlockSpec(memory_space=pl.ANY), pl.BlockSpec((bt, TOP_K), lambda i, dst: (i, 0)), vec,
             pl.BlockSpec((1, d), lambda i, dst: (0, 0))]
    out_shape, out_specs = [], []
    if mod is not None:
        args += [mod[0], mod[1]]
        specs += [vec, vec]
        out_shape.append(jax.ShapeDtypeStruct((t, d), F32))
        out_specs.append(row)
    out_shape.append(jax.ShapeDtypeStruct((t, d), out_dtype))
    out_specs.append(row)
    grid_spec = pltpu.PrefetchScalarGridSpec(
        num_scalar_prefetch=1,
        grid=(n_steps,),
        in_specs=specs,
        out_specs=out_specs,
        scratch_shapes=[pltpu.VMEM((2, TOP_K, bt, d), F32), pltpu.SemaphoreType.DMA((2,))],
    )
    outs = pl.pallas_call(
        functools.partial(_combine_norm_kernel, has_mod=mod is not None, bt=bt, n_steps=n_steps),
        grid_spec=grid_spec,
        out_shape=out_shape,
        compiler_params=_params("arbitrary"),
        name="moe_combine_norm",
    )(dest.reshape(-1), *args)
    return (outs[0], outs[1]) if mod is not None else (None, outs[0])


def _mm_kernel(*refs, ks, has_res):
    a_refs = refs[: len(ks)]
    rest = refs[len(ks):]
    w_ref = rest[0]
    if has_res:
        x_ref, gate_ref = rest[1], rest[2]
    o_ref, wb_ref = rest[-2], rest[-1]

    @pl.when(pl.program_id(1) == 0)
    def _():
        wb_ref[...] = w_ref[...].astype(BF16)

    acc = None
    k0 = 0
    for a_ref, k in zip(a_refs, ks):
        part = _dot(a_ref[...], wb_ref[k0:k0 + k, :])
        acc = part if acc is None else acc + part
        k0 += k
    if has_res:
        acc = x_ref[...] + gate_ref[0] * acc
    o_ref[...] = acc.astype(o_ref.dtype)


def _matmul(a_list, w, col0, n, out_dtype, residual=None):
    m = a_list[0].shape[0]
    ks = tuple(a.shape[1] for a in a_list)
    ktot = sum(ks)
    bm = min(MM_BM, m)
    bn = min(MM_BN, n)
    cb0 = col0 // bn
    args = list(a_list) + [w]
    specs = [pl.BlockSpec((bm, k), lambda j, i: (i, 0)) for k in ks]
    specs.append(pl.BlockSpec((ktot, bn), lambda j, i: (0, cb0 + j)))
    if residual is not None:
        x, gate, seq = residual
        bm = min(bm, seq)
        per_b = seq // bm
        specs[: len(ks)] = [pl.BlockSpec((bm, k), lambda j, i: (i, 0)) for k in ks]
        args += [x, gate]
        specs += [pl.BlockSpec((bm, bn), lambda j, i: (i, j)),
                  pl.BlockSpec((1, 1, bn), lambda j, i: (i // per_b, 0, j))]
    return pl.pallas_call(
        functools.partial(_mm_kernel, ks=ks, has_res=residual is not None),
        grid=(n // bn, m // bm),
        in_specs=specs,
        out_specs=pl.BlockSpec((bm, bn), lambda j, i: (i, j)),
        out_shape=jax.ShapeDtypeStruct((m, n), out_dtype),
        scratch_shapes=[pltpu.VMEM((ktot, bn), BF16)],
        compiler_params=_params("arbitrary", "arbitrary"),
        name="proj_matmul",
    )(*args)


def _ffn_kernel(st_ref, se_ref, r0_ref, nr_ref, h_ref, wg_ref, wu_ref, wd_ref, o_ref, wgb, wub, wdb,
                *, sub, nsub, ncol):
    del se_ref
    k = pl.program_id(0)
    f = pl.program_id(1)
    nr = nr_ref[k]
    r0 = pl.multiple_of(r0_ref[k], sub)
    bm, d = o_ref.shape
    nchunk = (nr + sub - 1) // sub
    first = jnp.logical_or(k == 0, st_ref[k] != st_ref[jnp.maximum(k - 1, 0)])

    @pl.when(jnp.logical_and(f == 0, first))
    def _():
        o_ref[...] = jnp.zeros((bm, d), F32)

    @pl.when(nr > 0)
    def _():
        wgb[...] = wg_ref[0].astype(BF16)
        wub[...] = wu_ref[0].astype(BF16)
        wdb[...] = wd_ref[0].astype(BF16)

    for c in range(1, nsub + 1):
        rows = c * sub

        @pl.when(nchunk == c)
        def _():
            h = h_ref[pl.ds(r0, rows), :]
            g = _dot(h, wgb[...])
            u = _dot(h, wub[...])
            a = (g * _sigmoid(g) * u).astype(BF16)
            for n in range(ncol):
                cols = slice(n * (d // ncol), (n + 1) * (d // ncol))
                o_ref[pl.ds(r0, rows), cols] += _dot(a, wdb[:, cols])


def _ffn(h, w_gate, w_up, w_down, seg_tile, seg_expert, seg_row0, seg_rows):
    rows, d = h.shape
    _, _, dff = w_gate.shape
    bm, bf, sub = FFN_BM, FFN_BF, FFN_SUB
    nf = dff // bf
    nseg = seg_tile.shape[0]

    def f_eff(k, f, nr):
        return jnp.where(nr[k] > 0, f, nf - 1)

    grid_spec = pltpu.PrefetchScalarGridSpec(
        num_scalar_prefetch=4,
        grid=(nseg, nf),
        in_specs=[
            pl.BlockSpec((bm, d), lambda k, f, st, se, r0, nr: (st[k], 0), pipeline_mode=pl.Buffered(1)),
            pl.BlockSpec((1, d, bf), lambda k, f, st, se, r0, nr: (se[k], 0, f_eff(k, f, nr))),
            pl.BlockSpec((1, d, bf), lambda k, f, st, se, r0, nr: (se[k], 0, f_eff(k, f, nr))),
            pl.BlockSpec((1, bf, d), lambda k, f, st, se, r0, nr: (se[k], f_eff(k, f, nr), 0)),
        ],
        out_specs=pl.BlockSpec((bm, d), lambda k, f, st, se, r0, nr: (st[k], 0), pipeline_mode=pl.Buffered(1)),
        scratch_shapes=[
            pltpu.VMEM((d, bf), BF16),
            pltpu.VMEM((d, bf), BF16),
            pltpu.VMEM((bf, d), BF16),
        ],
    )
    return pl.pallas_call(
        functools.partial(_ffn_kernel, sub=sub, nsub=bm // sub, ncol=FFN_NCOL),
        grid_spec=grid_spec,
        out_shape=jax.ShapeDtypeStruct((rows, d), F32),
        compiler_params=_params("arbitrary", "arbitrary"),
        name="swiglu_ffn",
    )(seg_tile, seg_expert, seg_row0, seg_rows, h, w_gate, w_up, w_down)


def _sb_kernel(q_ref, k_ref, v_ref, tri_ref, o_ref, *, bq, nh, scale):
    i = pl.program_id(2)
    dh = HEAD_DIM
    row = lax.broadcasted_iota(jnp.int32, (bq, bq), 0)
    col = lax.broadcasted_iota(jnp.int32, (bq, bq), 1)
    causal = col < row
    qs = [q_ref[:, h * dh:(h + 1) * dh] for h in range(nh)]

    def tile(j, state, diag):
        k0 = pl.multiple_of(j * bq, bq)
        out = []
        for h in range(nh):
            carry, acc = state[2 * h], state[2 * h + 1]
            k = k_ref[pl.ds(k0, bq), h * dh:(h + 1) * dh]
            v = v_ref[pl.ds(k0, bq), h * dh:(h + 1) * dh]
            z = _dot_nt(qs[h], k) * scale
            log_beta = jnp.minimum(z, 0.0) - jnp.log(1.0 + jnp.exp(-jnp.abs(z)))
            log_keep = log_beta - z
            if diag:
                log_keep = jnp.where(causal, log_keep, 0.0)
            hi, lo = _split_bf16(log_keep)
            between = _dot(jnp.concatenate([hi, lo], axis=1), tri_ref[...]) + carry
            w = jnp.exp(log_beta + between)
            if diag:
                w = jnp.where(causal, w, 0.0)
            out.append(carry + jnp.sum(log_keep, axis=-1, keepdims=True))
            out.append(acc + _dot(w.astype(BF16), v))
        return tuple(out)

    state = tuple(jnp.zeros((bq, 1 if n % 2 == 0 else dh), F32) for n in range(2 * nh))
    state = tile(i, state, True)

    def top_carry(st):
        m = jnp.max(st[0])
        for h in range(1, nh):
            m = jnp.maximum(m, jnp.max(st[2 * h]))
        return m

    def cond(c):
        return jnp.logical_and(c[0] < i, c[1] > SB_EXP_ZERO)

    def body(c):
        st = tile(i - 1 - c[0], c[2], False)
        return c[0] + 1, top_carry(st), st

    _, _, state = lax.while_loop(cond, body, (jnp.int32(0), top_carry(state), state))
    o_ref[...] = jnp.concatenate([state[2 * h + 1] for h in range(nh)], axis=1).astype(o_ref.dtype)


def _sb_attention(qkv, batch, seq, heads):
    t = qkv.shape[0]
    bq = min(SB_BQ, seq)
    nq = seq // bq
    nh = SB_HEADS_PER_STEP
    hw = nh * HEAD_DIM
    ng = heads // nh
    rj = lax.broadcasted_iota(jnp.int32, (2 * bq, bq), 0)
    cs = lax.broadcasted_iota(jnp.int32, (2 * bq, bq), 1)
    tri = jnp.where((rj % bq) > cs, 1.0, 0.0).astype(BF16)
    return pl.pallas_call(
        functools.partial(_sb_kernel, bq=bq, nh=nh, scale=HEAD_DIM ** -0.5),
        grid=(batch, ng, nq),
        in_specs=[
            pl.BlockSpec((bq, hw), lambda b, h, i: (b * nq + i, h)),
            pl.BlockSpec((seq, hw), lambda b, h, i: (b, ng + h)),
            pl.BlockSpec((seq, hw), lambda b, h, i: (b, 2 * ng + h)),
            pl.BlockSpec((2 * bq, bq), lambda b, h, i: (0, 0)),
        ],
        out_specs=pl.BlockSpec((bq, hw), lambda b, h, i: (b * nq + i, h)),
        out_shape=jax.ShapeDtypeStruct((t, heads * HEAD_DIM), BF16),
        compiler_params=_params("arbitrary", "arbitrary", "arbitrary"),
        name="stickbreak_attn",
    )(qkv, qkv, qkv, tri)


def _rglru_kernel(x_ref, g_ref, cw_ref, cb_ref, wa_ref, ba_ref, wx_ref, bx_ref, lam_ref, o_ref, xp_ref,
                  *, seq, cb, tc, kw):
    pad = SUBLANE
    xp_ref[0:pad, :] = jnp.zeros((pad, cb), F32)
    xp_ref[pad:, :] = x_ref[...]
    log_lam = _log_sigmoid(lam_ref[...])
    sub_iota = lax.broadcasted_iota(jnp.int32, (SUBLANE, cb), 0)
    ngroup = cb // LANE

    def chunk(ci, h):
        t0 = pl.multiple_of(ci * tc, tc)
        win = xp_ref[pl.ds(t0, tc + pad), :]
        xc = cb_ref[...] + cw_ref[0:1, :] * win[pad - kw + 1:pad - kw + 1 + tc, :]
        for k in range(1, kw):
            off = pad - kw + 1 + k
            xc = xc + cw_ref[k:k + 1, :] * win[off:off + tc, :]
        xcb = xc.astype(BF16)
        ra = jnp.concatenate(
            [_dot(xcb[:, q * LANE:(q + 1) * LANE], wa_ref[q].astype(BF16)) for q in range(ngroup)], axis=1)
        rx = jnp.concatenate(
            [_dot(xcb[:, q * LANE:(q + 1) * LANE], wx_ref[q].astype(BF16)) for q in range(ngroup)], axis=1)
        r = _sigmoid(ra + ba_ref[...])
        gi = _sigmoid(rx + bx_ref[...])
        log_a = RG_C * r * log_lam
        a = jnp.exp(log_a)
        u = jnp.sqrt(-jnp.tanh(log_a) * (a * a + 1.0)) * (gi * xc)
        gate = _gelu_tanh(g_ref[pl.ds(t0, tc), :])
        outs = []
        for gidx in range(tc // SUBLANE):
            av = a[gidx * SUBLANE:(gidx + 1) * SUBLANE, :]
            bv = u[gidx * SUBLANE:(gidx + 1) * SUBLANE, :]
            for sh in (1, 2, 4):
                a_s = pltpu.roll(av, sh, axis=0)
                b_s = pltpu.roll(bv, sh, axis=0)
                m = sub_iota >= sh
                bv = jnp.where(m, av * b_s + bv, bv)
                av = jnp.where(m, av * a_s, av)
            hv = av * h + bv
            outs.append(hv)
            h = jnp.broadcast_to(hv[SUBLANE - 1:SUBLANE, :], (SUBLANE, cb))
        hs = jnp.concatenate(outs, axis=0)
        o_ref[pl.ds(t0, tc), :] = (hs * gate).astype(o_ref.dtype)
        return h

    lax.fori_loop(0, seq // tc, chunk, jnp.zeros((SUBLANE, cb), F32))


def _block_diag_pairs(w):
    nblk, c, _ = w.shape
    w2 = w.reshape(nblk // 2, 2, c, c)
    z = jnp.zeros((nblk // 2, c, c), w.dtype)
    top = jnp.concatenate([w2[:, 0], z], axis=2)
    bot = jnp.concatenate([z, w2[:, 1]], axis=2)
    return jnp.concatenate([top, bot], axis=1)


def _rglru(rg, batch, seq, conv_w, conv_b, wa, ba, wx, bx, lam):
    t, c2 = rg.shape
    c = c2 // 2
    cb = min(RG_CB, c)
    tc = min(RG_TC, seq)
    ncb = c // cb
    kw = conv_w.shape[0]
    gpb = cb // LANE
    vec = pl.BlockSpec((1, cb), lambda b, j: (0, j))
    return pl.pallas_call(
        functools.partial(_rglru_kernel, seq=seq, cb=cb, tc=tc, kw=kw),
        grid=(batch, ncb),
        in_specs=[
            pl.BlockSpec((seq, cb), lambda b, j: (b, j)),
            pl.BlockSpec((seq, cb), lambda b, j: (b, ncb + j)),
            pl.BlockSpec((kw, cb), lambda b, j: (0, j)),
            vec,
            pl.BlockSpec((gpb, LANE, LANE), lambda b, j: (j, 0, 0)),
            vec,
            pl.BlockSpec((gpb, LANE, LANE), lambda b, j: (j, 0, 0)),
            vec,
            vec,
        ],
        out_specs=pl.BlockSpec((seq, cb), lambda b, j: (b, j)),
        out_shape=jax.ShapeDtypeStruct((t, c), BF16),
        scratch_shapes=[pltpu.VMEM((seq + SUBLANE, cb), F32)],
        compiler_params=_params("arbitrary", "arbitrary"),
        name="rglru",
    )(rg, rg, conv_w, conv_b.reshape(1, c), _block_diag_pairs(wa), ba.reshape(1, c),
      _block_diag_pairs(wx), bx.reshape(1, c), lam.reshape(1, c))


def _compress_kernel(x_ref, pos_ref, w1_ref, w2_ref, o_ref, *, ngrp):
    st = CMP_STRIDE
    half = st * HEAD_DIM
    xs = [x_ref[pl.ds(r, ngrp, stride=st), :] for r in range(st)]
    pos = pos_ref[0]
    x0 = jnp.concatenate([xs[r] + pos[r:r + 1, :] for r in range(st)], axis=1).astype(BF16)
    x1 = jnp.concatenate([xs[r] + pos[st + r:st + r + 1, :] for r in range(st)], axis=1).astype(BF16)
    p0 = _dot(x0, w1_ref[0, 0:half, :].astype(BF16))
    p1 = _dot(x1, w1_ref[0, half:2 * half, :].astype(BF16))
    pre = p0 + pltpu.roll(p1, ngrp - 1, axis=0)
    out = _dot(_gelu_tanh(pre).astype(BF16), w2_ref[0].astype(BF16))
    rown = lax.broadcasted_iota(jnp.int32, out.shape, 0)
    o_ref[...] = jnp.where(rown < ngrp - 1, out, 0.0).astype(o_ref.dtype)


def _compress(kcvc, batch, seq, pos, w1, w2):
    assert CMP_BLOCK == 2 * CMP_STRIDE
    g2 = kcvc.shape[1] // HEAD_DIM
    per = g2 // 2
    ngrp = seq // CMP_STRIDE
    return pl.pallas_call(
        functools.partial(_compress_kernel, ngrp=ngrp),
        grid=(batch, g2),
        in_specs=[
            pl.BlockSpec((seq, HEAD_DIM), lambda b, j: (b, j)),
            pl.BlockSpec((1, CMP_BLOCK, HEAD_DIM), lambda b, j: (j // per, 0, 0)),
            pl.BlockSpec((1, CMP_BLOCK * HEAD_DIM, HEAD_DIM), lambda b, j: (j // per, 0, 0)),
            pl.BlockSpec((1, HEAD_DIM, HEAD_DIM), lambda b, j: (j // per, 0, 0)),
        ],
        out_specs=pl.BlockSpec((ngrp, HEAD_DIM), lambda b, j: (b * g2 + j, 0)),
        out_shape=jax.ShapeDtypeStruct((batch * g2 * ngrp, HEAD_DIM), BF16),
        compiler_params=_params("arbitrary", "arbitrary"),
        name="nsa_compress",
    )(kcvc, pos, w1, w2)


def _softmax_parts(s2, bias):
    sb = s2 + bias
    m = jnp.max(sb, axis=-1, keepdims=True)
    e = jnp.exp2(sb - m)
    return e, jnp.sum(e, axis=-1, keepdims=True)


def _nsa_kernel(q_ref, kc_ref, vc_ref, ks_ref, vs_ref, kw_ref, vw_ref, g_ref, ov_ref, ex_ref, o_ref,
                acc_ref, inv_ref, *, seq, hg, n_slc, scale):
    bq, bk, dh = NSA_BQ, NSA_BK, HEAD_DIM
    i = pl.program_id(2)
    q0 = i * bq
    qb = q_ref[...]
    qs = jnp.concatenate([qb[:, h * dh:(h + 1) * dh] for h in range(hg)], axis=0)
    qpos1 = q0 + lax.broadcasted_iota(jnp.int32, (bq, 1), 0)
    qpos = jnp.concatenate([qpos1] * hg, axis=0)

    scale2 = scale * LOG2E
    tile_heads = lambda a: jnp.concatenate([a] * hg, axis=0)

    ncmp = kc_ref.shape[0]
    s_c = _dot_nt(qs, kc_ref[...]) * scale2
    n_idx = lax.broadcasted_iota(jnp.int32, (1, ncmp), 1)
    bias_c = jnp.where(n_idx * CMP_STRIDE + (CMP_BLOCK - 1) <= qpos1, 0.0, NEG_BIG)
    e_c, l_c = _softmax_parts(s_c, tile_heads(bias_c))
    p_c = e_c * jnp.where(qpos >= CMP_BLOCK - 1, 1.0 / l_c, 0.0)
    o_c = _dot(p_c.astype(BF16), vc_ref[...])

    p_sum = p_c[0:bq, :]
    for h in range(1, hg):
        p_sum = p_sum + p_c[h * bq:(h + 1) * bq, :]
    overlap_t = ov_ref[...]
    p_hi, p_lo = _split_bf16(p_sum)
    imp_t = (_dot_nt(overlap_t, p_hi) + _dot_nt(overlap_t, p_lo))[0:n_slc, :]

    blk = lax.broadcasted_iota(jnp.int32, (n_slc, bq), 0)
    qp_l = q0 + lax.broadcasted_iota(jnp.int32, (n_slc, bq), 1)
    cur = qp_l // SLC_BLOCK
    forced = (blk == 0) | (blk == cur) | (blk == cur - 1)
    valid = blk * SLC_BLOCK <= qp_l
    rank = jnp.where(valid, imp_t + FORCE_BONUS * jnp.where(forced, 1.0, 0.0), -jnp.inf)
    ahead = jnp.zeros((n_slc, bq), F32)
    for jp in range(n_slc):
        other = rank[jp:jp + 1, :]
        beats = (other > rank) | ((other == rank) & (blk > jp))
        ahead = ahead + jnp.where(beats, 1.0, 0.0)
    k_sel = min(N_SEL, n_slc)
    sel_t = jnp.where(valid & (ahead < k_sel), 1.0, 0.0)
    sel_t = jnp.concatenate([sel_t, jnp.zeros((LANE - n_slc, bq), F32)], axis=0)
    sel = sel_t.T.astype(BF16)

    n_need = (q0 + bq + bk - 1) // bk
    for v in range(1, seq // bk + 1):
        nk = v * bk

        @pl.when(n_need == v)
        def _():
            s = _dot_nt(qs, ks_ref[0:nk, :]) * scale2
            picked = _dot(sel, ex_ref[:, 0:nk])
            kpos = lax.broadcasted_iota(jnp.int32, (1, nk), 1)
            bias = jnp.where((picked > 0.5) & (kpos <= qpos1), 0.0, NEG_BIG)
            e_s, l_s = _softmax_parts(s, tile_heads(bias))
            acc_ref[...] = _dot(e_s.astype(BF16), vs_ref[0:nk, :])
            inv_ref[...] = 1.0 / l_s

    acc_s = acc_ref[...]
    inv_s = inv_ref[...]

    span = min(WINDOW + bq, seq)
    w0 = pl.multiple_of(jnp.maximum(jnp.minimum(q0 - WINDOW, seq - span), 0), bq)
    kwt = kw_ref[pl.ds(w0, span), :]
    vwt = vw_ref[pl.ds(w0, span), :]
    s_w = _dot_nt(qs, kwt) * scale2
    kpos_w = w0 + lax.broadcasted_iota(jnp.int32, (1, span), 1)
    bias_w = jnp.where((kpos_w <= qpos1) & (kpos_w > qpos1 - WINDOW), 0.0, NEG_BIG)
    e_w, l_w = _softmax_parts(s_w, tile_heads(bias_w))
    acc_w = _dot(e_w.astype(BF16), vwt)
    inv_w = 1.0 / l_w

    gates = _sigmoid(g_ref[...])
    outs = []
    for h in range(hg):
        r = slice(h * bq, (h + 1) * bq)
        c = h * N_BRANCH
        outs.append(gates[:, c:c + 1] * o_c[r, :] + (gates[:, c + 1:c + 2] * inv_s[r, :]) * acc_s[r, :]
                    + (gates[:, c + 2:c + 3] * inv_w[r, :]) * acc_w[r, :])
    o_ref[...] = jnp.concatenate(outs, axis=1).astype(o_ref.dtype)


def _nsa_attention(q, kv4, cmp, gate_logits, batch, seq):
    t, width = q.shape
    g = NSA_KV_HEADS
    hg = width // (g * HEAD_DIM)
    bq = NSA_BQ
    nb = seq // bq
    ncmp = seq // CMP_STRIDE
    n_slc = seq // SLC_BLOCK
    dh = HEAD_DIM
    full = lambda off: pl.BlockSpec((seq, dh), lambda b, gi, i: (b, off * g + gi))
    jj = lax.broadcasted_iota(jnp.int32, (LANE, ncmp), 0)
    cstart = lax.broadcasted_iota(jnp.int32, (LANE, ncmp), 1) * CMP_STRIDE
    overlap_t = jnp.where((cstart < (jj + 1) * SLC_BLOCK) & (cstart + CMP_BLOCK > jj * SLC_BLOCK)
                          & (jj < n_slc), 1.0, 0.0).astype(BF16)
    expand = jnp.where(lax.broadcasted_iota(jnp.int32, (LANE, seq), 1) // SLC_BLOCK
                       == lax.broadcasted_iota(jnp.int32, (LANE, seq), 0), 1.0, 0.0).astype(BF16)
    const = lambda shape: pl.BlockSpec(shape, lambda b, gi, i: (0, 0))
    return pl.pallas_call(
        functools.partial(_nsa_kernel, seq=seq, hg=hg, n_slc=n_slc, scale=dh ** -0.5),
        grid=(batch, g, nb),
        in_specs=[
            pl.BlockSpec((bq, hg * dh), lambda b, gi, i: (b * nb + i, gi)),
            pl.BlockSpec((ncmp, dh), lambda b, gi, i: (b * 2 * g + gi, 0)),
            pl.BlockSpec((ncmp, dh), lambda b, gi, i: (b * 2 * g + g + gi, 0)),
            full(0), full(1), full(2), full(3),
            pl.BlockSpec((bq, LANE), lambda b, gi, i: (b * nb + i, gi)),
            const((LANE, ncmp)),
            const((LANE, seq)),
        ],
        out_specs=pl.BlockSpec((bq, hg * dh), lambda b, gi, i: (b * nb + i, gi)),
        out_shape=jax.ShapeDtypeStruct((t, width), BF16),
        scratch_shapes=[pltpu.VMEM((hg * bq, dh), F32), pltpu.VMEM((hg * bq, 1), F32)],
        compiler_params=_params("arbitrary", "arbitrary", "arbitrary"),
        name="nsa_attn",
    )(q, cmp, cmp, kv4, kv4, kv4, kv4, gate_logits, overlap_t, expand)


def _gather_kernel(lo_ref, hi_ref, src_ref, h_ref, o_ref, *, chunk):
    i = pl.program_id(0)
    rows = o_ref.shape[0]
    src = src_ref[...]
    lane = lax.broadcasted_iota(jnp.int32, (rows, chunk), 1)
    o_ref[...] = jnp.zeros(o_ref.shape, o_ref.dtype)

    def body(c, carry):
        c0 = pl.multiple_of(c * chunk, chunk)
        onehot = jnp.where(src - c0 == lane, 1.0, 0.0).astype(BF16)
        o_ref[...] += _dot(onehot, h_ref[pl.ds(c0, chunk), :]).astype(o_ref.dtype)
        return carry

    lax.fori_loop(lo_ref[i], hi_ref[i], body, 0)


def _gather_rows(h, src, n_rows):
    t, d = h.shape
    bt, chunk = GATHER_ROWS, GATHER_CHUNK
    nt = n_rows // bt
    src2 = src.reshape(nt, bt)
    live = src2 >= 0
    lo = jnp.min(jnp.where(live, src2, t), axis=1) // chunk
    hi = jnp.where(jnp.any(live, axis=1), jnp.max(src2, axis=1) // chunk + 1, lo)
    grid_spec = pltpu.PrefetchScalarGridSpec(
        num_scalar_prefetch=2,
        grid=(nt,),
        in_specs=[
            pl.BlockSpec((bt, 1), lambda i, lo, hi: (i, 0)),
            pl.BlockSpec((t, d), lambda i, lo, hi: (0, 0), pipeline_mode=pl.Buffered(1)),
        ],
        out_specs=pl.BlockSpec((bt, d), lambda i, lo, hi: (i, 0)),
    )
    return pl.pallas_call(
        functools.partial(_gather_kernel, chunk=chunk),
        grid_spec=grid_spec,
        out_shape=jax.ShapeDtypeStruct((n_rows, d), h.dtype),
        compiler_params=_params("arbitrary"),
        name="moe_dispatch",
    )(jnp.minimum(lo, hi).astype(jnp.int32), hi.astype(jnp.int32), src.reshape(n_rows, 1), h)


def _router_kernel(l_ref, o_ref, *, n_exp):
    x = l_ref[...]
    lane = lax.broadcasted_iota(jnp.int32, x.shape, 1)
    xm = jnp.where(lane < n_exp, x, -jnp.inf)
    v0 = jnp.max(xm, axis=-1, keepdims=True)
    i0 = jnp.min(jnp.where(xm == v0, lane, LANE), axis=-1, keepdims=True)
    xm = jnp.where(lane == i0, -jnp.inf, xm)
    v1 = jnp.max(xm, axis=-1, keepdims=True)
    i1 = jnp.min(jnp.where(xm == v1, lane, LANE), axis=-1, keepdims=True)
    e = jnp.exp(v1 - v0)
    p0 = 1.0 / (1.0 + e)
    out = jnp.where(lane == 0, p0, jnp.where(lane == 1, e * p0, jnp.where(
        lane == 2, i0.astype(F32), jnp.where(lane == 3, i1.astype(F32), 0.0))))
    o_ref[...] = out


def _router_top2(logits, n_exp):
    t = logits.shape[0]
    bt = ROW_TILE
    out = pl.pallas_call(
        functools.partial(_router_kernel, n_exp=n_exp),
        grid=(t // bt,),
        in_specs=[pl.BlockSpec((bt, LANE), lambda i: (i, 0))],
        out_specs=pl.BlockSpec((bt, LANE), lambda i: (i, 0)),
        out_shape=jax.ShapeDtypeStruct((t, LANE), F32),
        compiler_params=_params("arbitrary"),
        name="router_top2",
    )(logits)
    return out[:, 0:TOP_K], out[:, TOP_K:2 * TOP_K].astype(jnp.int32)


def _route(probs, top_i, n_experts):
    t = top_i.shape[0]
    flat_e = top_i.reshape(-1)
    onehot = (flat_e[:, None] == jnp.arange(n_experts)[None, :]).astype(jnp.int32)
    rank = jnp.take_along_axis(jnp.cumsum(onehot, axis=0), flat_e[:, None], axis=1)[:, 0] - 1
    counts = jnp.sum(onehot, axis=0)
    padded = (counts + FFN_SUB - 1) // FFN_SUB * FFN_SUB
    end = jnp.cumsum(padded)
    start = end - padded
    n_rows = -(-(TOP_K * t + n_experts * (FFN_SUB - 1)) // FFN_BM) * FFN_BM
    n_tiles = n_rows // FFN_BM
    dest = (start[flat_e] + rank).astype(jnp.int32)
    src = jnp.full((n_rows,), -1, jnp.int32).at[dest].set(jnp.arange(TOP_K * t, dtype=jnp.int32) // TOP_K)
    cuts = jnp.sort(jnp.concatenate([start, jnp.arange(n_tiles) * FFN_BM]))
    nxt = jnp.concatenate([cuts[1:], jnp.full((1,), n_rows, cuts.dtype)])
    seg_rows = jnp.maximum(jnp.minimum(nxt, end[-1]) - cuts, 0)
    seg_tile = jnp.minimum(cuts // FFN_BM, n_tiles - 1)
    seg_expert = jnp.minimum(jnp.searchsorted(end, cuts, side="right"), n_experts - 1)
    i32 = lambda a: a.astype(jnp.int32)
    return probs, dest.reshape(t, TOP_K), src, i32(seg_tile), i32(seg_expert), i32(cuts % FFN_BM), i32(seg_rows)


def kernel(x, c, ada_mix_w, ada_mix_b, norm_mix_g, ada_ffn_w, ada_ffn_b, norm_ffn_g, even_in_w, rg_conv_w, rg_conv_b, rg_wa, rg_ba, rg_wx, rg_bx, rg_lambda, even_out_w, dense_w_gate, dense_w_up, dense_w_down, nsa_in_w, cmp_pos_k, cmp_pos_v, cmp_k_w1, cmp_k_w2, cmp_v_w1, cmp_v_w2, nsa_out_w, router_w, moe_w_gate, moe_w_up, moe_w_down, final_norm_g):
    batch, seq, d = x.shape
    t = batch * seq
    depth = ada_mix_w.shape[0]
    xf = x.reshape(t, d)

    c_pad = jnp.pad(c, ((0, (-batch) % SUBLANE), (0, 0)))
    m_mix = _adaln(c_pad, ada_mix_w, ada_mix_b)
    m_ffn = _adaln(c_pad, ada_ffn_w, ada_ffn_b)

    def mods(m, layer):
        v = m[layer, :batch].reshape(batch, 1, 3, d)
        return v[:, :, 0], v[:, :, 1], v[:, :, 2]

    def norm_step(xf, pending, g, mod, out_dtype=BF16):
        if pending is not None and pending[0] == "experts":
            _, y, dest, probs, gate = pending
            return _combine_norm(xf, y, dest, probs, gate, g, seq, mod=mod, out_dtype=out_dtype)
        res = None if pending is None else pending[1:]
        return _normmod(xf, g, seq, res=res, mod=mod, out_dtype=out_dtype)

    pending = None
    for layer in range(depth):
        j = layer // 2
        shift, scale, gate = mods(m_mix, layer)
        xf, h = norm_step(xf, pending, norm_mix_g[layer], (shift, scale))
        if layer % 2 == 0:
            w_in = even_in_w[j]
            sbw = (w_in.shape[1] - 2 * rg_conv_w.shape[2]) // 3
            heads = sbw // HEAD_DIM
            qkv = _matmul([h], w_in, 0, 3 * sbw, BF16)
            rg = _matmul([h], w_in, 3 * sbw, w_in.shape[1] - 3 * sbw, F32)
            o_a = _sb_attention(qkv, batch, seq, heads)
            o_b = _rglru(rg, batch, seq, rg_conv_w[j], rg_conv_b[j], rg_wa[j], rg_ba[j], rg_wx[j],
                         rg_bx[j], rg_lambda[j])
            xf = _matmul([o_a, o_b], even_out_w[j], 0, d, F32, residual=(xf, gate, seq))
        else:
            w_in = nsa_in_w[j]
            g = NSA_KV_HEADS
            kvw = g * HEAD_DIM
            nsa_w = nsa_out_w.shape[1]
            hg = nsa_w // kvw
            q = _matmul([h], w_in, 0, nsa_w, BF16)
            kcvc = _matmul([h], w_in, nsa_w, 2 * kvw, F32)
            kv4 = _matmul([h], w_in, nsa_w + 2 * kvw, 4 * kvw, BF16)
            wg = w_in[:, nsa_w + 6 * kvw:].reshape(d, g, hg * N_BRANCH)
            wg = jnp.pad(wg, ((0, 0), (0, 0), (0, LANE - hg * N_BRANCH))).reshape(d, g * LANE)
            gl = _matmul([h], wg, 0, g * LANE, F32)
            cmp = _compress(kcvc, batch, seq, jnp.stack([cmp_pos_k[j], cmp_pos_v[j]]),
                            jnp.stack([cmp_k_w1[j], cmp_v_w1[j]]), jnp.stack([cmp_k_w2[j], cmp_v_w2[j]]))
            o = _nsa_attention(q, kv4, cmp, gl, batch, seq)
            xf = _matmul([o], nsa_out_w[j], 0, d, F32, residual=(xf, gate, seq))
        pending = None

        shift, scale, gate = mods(m_ffn, layer)
        xf, h = norm_step(xf, pending, norm_ffn_g[layer], (shift, scale))
        if layer % 2 == 0:
            nt = t // FFN_BM
            ffn = _ffn(h, dense_w_gate[j:j + 1], dense_w_up[j:j + 1], dense_w_down[j:j + 1],
                       jnp.arange(nt, dtype=jnp.int32), jnp.zeros((nt,), jnp.int32),
                       jnp.zeros((nt,), jnp.int32), jnp.full((nt,), FFN_BM, jnp.int32))
            pending = ("dense", ffn, gate)
        else:
            n_exp = router_w.shape[2]
            rw = jnp.pad(router_w[j], ((0, 0), (0, LANE - n_exp)))
            probs, top_i = _router_top2(_matmul([h], rw, 0, LANE, F32), n_exp)
            probs, dest, src, seg_tile, seg_expert, seg_row0, seg_rows = _route(probs, top_i, n_exp)
            h_sorted = _gather_rows(h, src, src.shape[0])
            y = _ffn(h_sorted, moe_w_gate[j], moe_w_up[j], moe_w_down[j], seg_tile, seg_expert, seg_row0,
                     seg_rows)
            pending = ("experts", y, dest, probs, gate)

    _, out = norm_step(xf, pending, final_norm_g, None, F32)
    return out.reshape(batch, seq, d)
```

```python
import functools

import jax
import jax.numpy as jnp
from jax import lax
from jax.experimental import pallas as pl
from jax.experimental.pallas import tpu as pltpu

F32 = jnp.float32
BF16 = jnp.bfloat16

LANE = 128
SUBLANE = 8
VMEM_LIMIT_BYTES = 56 * 1024 * 1024

HEAD_DIM = 128
EPS = 1e-6
RG_C = 8.0
NSA_KV_HEADS = 4
N_BRANCH = 3
CMP_BLOCK = 32
CMP_STRIDE = 16
SLC_BLOCK = 64
N_SEL = 8
WINDOW = 512
FORCE_BONUS = 1e6
TOP_K = 2
NEG_BIG = -1e30
LOG2E = 1.4426950408889634
SB_EXP_ZERO = -110.0

ROW_TILE = 512
COMBINE_ROWS = 256
MM_BM = 2048
MM_BN = 512
FFN_BM = 2048
FFN_SUB = 256
FFN_NCOL = 4
FFN_BF = 256
GATHER_ROWS = 512
GATHER_CHUNK = 256
SB_BQ = 256
SB_HEADS_PER_STEP = 4
NSA_BQ = 128
NSA_BK = 512
RG_CB = 256
RG_TC = 256


def _params(*sem):
    return pltpu.CompilerParams(dimension_semantics=sem, vmem_limit_bytes=VMEM_LIMIT_BYTES)


def _dot(a, b):
    return jnp.dot(a, b, preferred_element_type=F32)


def _dot_nt(a, b):
    return lax.dot_general(a, b, (((1,), (1,)), ((), ())), preferred_element_type=F32)


def _sigmoid(x):
    return 1.0 / (1.0 + jnp.exp(-x))


def _log_sigmoid(x):
    return jnp.minimum(x, 0.0) - jnp.log1p(jnp.exp(-jnp.abs(x)))


def _gelu_tanh(x):
    return 0.5 * x * (1.0 + jnp.tanh(0.7978845608028654 * (x + 0.044715 * (x * x * x))))


def _split_bf16(x):
    hi = x.astype(BF16)
    lo = (x - hi.astype(F32)).astype(BF16)
    return hi, lo


def _adaln_kernel(c_ref, w_ref, b_ref, o_ref):
    c = c_ref[...]
    s = (c * _sigmoid(c)).astype(BF16)
    o_ref[0] = _dot(s, w_ref[0].astype(BF16)) + b_ref[0]


def _adaln(c_pad, w, b, bn=1024):
    depth, d, n3 = w.shape
    rows = c_pad.shape[0]
    return pl.pallas_call(
        _adaln_kernel,
        grid=(depth, n3 // bn),
        in_specs=[
            pl.BlockSpec((rows, d), lambda l, j: (0, 0)),
            pl.BlockSpec((1, d, bn), lambda l, j: (l, 0, j)),
            pl.BlockSpec((1, 1, bn), lambda l, j: (l, 0, j)),
        ],
        out_specs=pl.BlockSpec((1, rows, bn), lambda l, j: (l, 0, j)),
        out_shape=jax.ShapeDtypeStruct((depth, rows, n3), F32),
        compiler_params=_params("arbitrary", "arbitrary"),
        name="adaln",
    )(c_pad, w, b.reshape(depth, 1, n3))


def _normmod_kernel(*refs, has_res, has_mod):
    it = iter(refs)
    x_ref = next(it)
    if has_res:
        y_ref, gate_ref = next(it), next(it)
    g_ref = next(it)
    if has_mod:
        shift_ref, scale_ref = next(it), next(it)
    if has_res:
        xo_ref = next(it)
    h_ref = next(it)

    x = x_ref[...]
    if has_res:
        x = x + gate_ref[0] * y_ref[...]
        xo_ref[...] = x
    ms = jnp.mean(x * x, axis=-1, keepdims=True)
    h = x * lax.rsqrt(ms + EPS) * g_ref[...]
    if has_mod:
        h = h * (1.0 + scale_ref[0]) + shift_ref[0]
    h_ref[...] = h.astype(h_ref.dtype)


def _normmod(x, g, seq, res=None, mod=None, out_dtype=BF16):
    t, d = x.shape
    bt = ROW_TILE
    per_b = seq // bt
    row = pl.BlockSpec((bt, d), lambda i: (i, 0))
    vec = pl.BlockSpec((1, 1, d), lambda i: (i // per_b, 0, 0))
    args, specs = [x], [row]
    if res is not None:
        args += [res[0], res[1]]
        specs += [row, vec]
    args.append(g.reshape(1, d))
    specs.append(pl.BlockSpec((1, d), lambda i: (0, 0)))
    if mod is not None:
        args += [mod[0], mod[1]]
        specs += [vec, vec]
    out_shape, out_specs = [], []
    if res is not None:
        out_shape.append(jax.ShapeDtypeStruct((t, d), F32))
        out_specs.append(row)
    out_shape.append(jax.ShapeDtypeStruct((t, d), out_dtype))
    out_specs.append(row)
    outs = pl.pallas_call(
        functools.partial(_normmod_kernel, has_res=res is not None, has_mod=mod is not None),
        grid=(t // bt,),
        in_specs=specs,
        out_specs=out_specs,
        out_shape=out_shape,
        compiler_params=_params("arbitrary"),
        name="normmod",
    )(*args)
    return outs if res is not None else (x, outs[0])


def _combine_norm_kernel(dest_ref, *refs, has_mod, bt, n_steps):
    it = iter(refs)
    x_ref, y_hbm, p_ref, gate_ref, g_ref = (next(it) for _ in range(5))
    if has_mod:
        shift_ref, scale_ref, xo_ref = next(it), next(it), next(it)
    h_ref, buf, sem = next(it), next(it), next(it)
    i = pl.program_id(0)

    def row_copy(src_row, slot, k, r):
        return pltpu.make_async_copy(y_hbm.at[pl.ds(src_row, 1), :], buf.at[slot, k, pl.ds(r, 1), :],
                                     sem.at[slot])

    def start_tile(tile, slot):
        base = tile * (bt * TOP_K)

        def body(r, carry):
            for k in range(TOP_K):
                row_copy(dest_ref[base + r * TOP_K + k], slot, k, r).start()
            return carry

        lax.fori_loop(0, bt, body, 0)

    @pl.when(i == 0)
    def _():
        start_tile(0, 0)

    @pl.when(i + 1 < n_steps)
    def _():
        start_tile(i + 1, (i + 1) % 2)

    slot = i % 2

    def wait_row(r, carry):
        for k in range(TOP_K):
            row_copy(0, slot, k, r).wait()
        return carry

    lax.fori_loop(0, bt, wait_row, 0)
    p = p_ref[...]
    y = p[:, 0:1] * buf[slot, 0]
    for k in range(1, TOP_K):
        y = y + p[:, k:k + 1] * buf[slot, k]
    x = x_ref[...] + gate_ref[0] * y
    ms = jnp.mean(x * x, axis=-1, keepdims=True)
    h = x * lax.rsqrt(ms + EPS) * g_ref[...]
    if has_mod:
        xo_ref[...] = x
        h = h * (1.0 + scale_ref[0]) + shift_ref[0]
    h_ref[...] = h.astype(h_ref.dtype)


def _combine_norm(x, y, dest, probs, gate, g, seq, mod=None, out_dtype=BF16):
    t, d = x.shape
    bt = COMBINE_ROWS
    n_steps = t // bt
    per_b = seq // bt
    row = pl.BlockSpec((bt, d), lambda i, dst: (i, 0))
    vec = pl.BlockSpec((1, 1, d), lambda i, dst: (i // per_b, 0, 0))
    args = [x, y, probs, gate, g.reshape(1, d)]
    specs = [row, pl.BlockSpec(memory_space=pl.ANY), pl.BlockSpec((bt, TOP_K), lambda i, dst: (i, 0)), vec,
             pl.BlockSpec((1, d), lambda i, dst: (0, 0))]
    out_shape, out_specs = [], []
    if mod is not None:
        args += [mod[0], mod[1]]
        specs += [vec, vec]
        out_shape.append(jax.ShapeDtypeStruct((t, d), F32))
        out_specs.append(row)
    out_shape.append(jax.ShapeDtypeStruct((t, d), out_dtype))
    out_specs.append(row)
    grid_spec = pltpu.PrefetchScalarGridSpec(
        num_scalar_prefetch=1,
        grid=(n_steps,),
        in_specs=specs,
        out_specs=out_specs,
        scratch_shapes=[pltpu.VMEM((2, TOP_K, bt, d), F32), pltpu.SemaphoreType.DMA((2,))],
    )
    outs = pl.pallas_call(
        functools.partial(_combine_norm_kernel, has_mod=mod is not None, bt=bt, n_steps=n_steps),
        grid_spec=grid_spec,
        out_shape=out_shape,
        compiler_params=_params("arbitrary"),
        name="moe_combine_norm",
    )(dest.reshape(-1), *args)
    return (outs[0], outs[1]) if mod is not None else (None, outs[0])


def _mm_kernel(*refs, ks, has_res):
    a_refs = refs[: len(ks)]
    rest = refs[len(ks):]
    w_ref = rest[0]
    if has_res:
        x_ref, gate_ref = rest[1], rest[2]
    o_ref, wb_ref = rest[-2], rest[-1]

    @pl.when(pl.program_id(1) == 0)
    def _():
        wb_ref[...] = w_ref[...].astype(BF16)

    acc = None
    k0 = 0
    for a_ref, k in zip(a_refs, ks):
        part = _dot(a_ref[...], wb_ref[k0:k0 + k, :])
        acc = part if acc is None else acc + part
        k0 += k
    if has_res:
        acc = x_ref[...] + gate_ref[0] * acc
    o_ref[...] = acc.astype(o_ref.dtype)


def _matmul(a_list, w, col0, n, out_dtype, residual=None):
    m = a_list[0].shape[0]
    ks = tuple(a.shape[1] for a in a_list)
    ktot = sum(ks)
    bm = min(MM_BM, m)
    bn = min(MM_BN, n)
    cb0 = col0 // bn
    args = list(a_list) + [w]
    specs = [pl.BlockSpec((bm, k), lambda j, i: (i, 0)) for k in ks]
    specs.append(pl.BlockSpec((ktot, bn), lambda j, i: (0, cb0 + j)))
    if residual is not None:
        x, gate, seq = residual
        bm = min(bm, seq)
        per_b = seq // bm
        specs[: len(ks)] = [pl.BlockSpec((bm, k), lambda j, i: (i, 0)) for k in ks]
        args += [x, gate]
        specs += [pl.BlockSpec((bm, bn), lambda j, i: (i, j)),
                  pl.BlockSpec((1, 1, bn), lambda j, i: (i // per_b, 0, j))]
    return pl.pallas_call(
        functools.partial(_mm_kernel, ks=ks, has_res=residual is not None),
        grid=(n // bn, m // bm),
        in_specs=specs,
        out_specs=pl.BlockSpec((bm, bn), lambda j, i: (i, j)),
        out_shape=jax.ShapeDtypeStruct((m, n), out_dtype),
        scratch_shapes=[pltpu.VMEM((ktot, bn), BF16)],
        compiler_params=_params("arbitrary", "arbitrary"),
        name="proj_matmul",
    )(*args)


def _ffn_kernel(st_ref, se_ref, r0_ref, nr_ref, h_ref, wg_ref, wu_ref, wd_ref, o_ref, wgb, wub, wdb,
                *, sub, nsub, ncol):
    del se_ref
    k = pl.program_id(0)
    f = pl.program_id(1)
    nr = nr_ref[k]
    r0 = pl.multiple_of(r0_ref[k], sub)
    bm, d = o_ref.shape
    nchunk = (nr + sub - 1) // sub
    first = jnp.logical_or(k == 0, st_ref[k] != st_ref[jnp.maximum(k - 1, 0)])

    @pl.when(jnp.logical_and(f == 0, first))
    def _():
        o_ref[...] = jnp.zeros((bm, d), F32)

    for c in range(1, nsub + 1):
        rows = c * sub

        @pl.when(nchunk == c)
        def _():
            h = h_ref[pl.ds(r0, rows), :]
            wgb[...] = wg_ref[0].astype(BF16)
            g = _dot(h, wgb[...])
            wub[...] = wu_ref[0].astype(BF16)
            u = _dot(h, wub[...])
            wdb[...] = wd_ref[0].astype(BF16)
            a = (g * _sigmoid(g) * u).astype(BF16)
            for n in range(ncol):
                cols = slice(n * (d // ncol), (n + 1) * (d // ncol))
                o_ref[pl.ds(r0, rows), cols] += _dot(a, wdb[:, cols])


def _ffn(h, w_gate, w_up, w_down, seg_tile, seg_expert, seg_row0, seg_rows):
    rows, d = h.shape
    _, _, dff = w_gate.shape
    bm, bf, sub = FFN_BM, FFN_BF, FFN_SUB
    nf = dff // bf
    nseg = seg_tile.shape[0]

    def f_eff(k, f, nr):
        return jnp.where(nr[k] > 0, f, nf - 1)

    grid_spec = pltpu.PrefetchScalarGridSpec(
        num_scalar_prefetch=4,
        grid=(nseg, nf),
        in_specs=[
            pl.BlockSpec((bm, d), lambda k, f, st, se, r0, nr: (st[k], 0), pipeline_mode=pl.Buffered(1)),
            pl.BlockSpec((1, d, bf), lambda k, f, st, se, r0, nr: (se[k], 0, f_eff(k, f, nr))),
            pl.BlockSpec((1, d, bf), lambda k, f, st, se, r0, nr: (se[k], 0, f_eff(k, f, nr))),
            pl.BlockSpec((1, bf, d), lambda k, f, st, se, r0, nr: (se[k], f_eff(k, f, nr), 0)),
        ],
        out_specs=pl.BlockSpec((bm, d), lambda k, f, st, se, r0, nr: (st[k], 0), pipeline_mode=pl.Buffered(1)),
        scratch_shapes=[
            pltpu.VMEM((d, bf), BF16),
            pltpu.VMEM((d, bf), BF16),
            pltpu.VMEM((bf, d), BF16),
        ],
    )
    return pl.pallas_call(
        functools.partial(_ffn_kernel, sub=sub, nsub=bm // sub, ncol=FFN_NCOL),
        grid_spec=grid_spec,
        out_shape=jax.ShapeDtypeStruct((rows, d), F32),
        compiler_params=_params("arbitrary", "arbitrary"),
        name="swiglu_ffn",
    )(seg_tile, seg_expert, seg_row0, seg_rows, h, w_gate, w_up, w_down)


def _sb_kernel(q_ref, k_ref, v_ref, tri_ref, tri2_ref, o_ref, *, bq, nh, scale):
    i = pl.program_id(2)
    dh = HEAD_DIM
    qs = [q_ref[:, h * dh:(h + 1) * dh] for h in range(nh)]

    def tile(k0, kwid, state, tri, shift):
        if shift is not None:
            row = lax.broadcasted_iota(jnp.int32, (bq, kwid), 0)
            col = lax.broadcasted_iota(jnp.int32, (bq, kwid), 1)
            causal = col < row + shift
        out = []
        for h in range(nh):
            carry, acc = state[2 * h], state[2 * h + 1]
            k = k_ref[pl.ds(k0, kwid), h * dh:(h + 1) * dh]
            v = v_ref[pl.ds(k0, kwid), h * dh:(h + 1) * dh]
            z = _dot_nt(qs[h], k) * scale
            log_beta = jnp.minimum(z, 0.0) - jnp.log(1.0 + jnp.exp(-jnp.abs(z)))
            log_keep = log_beta - z
            if shift is not None:
                log_keep = jnp.where(causal, log_keep, 0.0)
            hi, lo = _split_bf16(log_keep)
            between = _dot(jnp.concatenate([hi, lo], axis=1), tri[...]) + carry
            w = jnp.exp(log_beta + between)
            if shift is not None:
                w = jnp.where(causal, w, 0.0)
            out.append(carry + jnp.sum(log_keep, axis=-1, keepdims=True))
            out.append(acc + _dot(w.astype(BF16), v))
        return tuple(out)

    state0 = tuple(jnp.zeros((bq, 1 if n % 2 == 0 else dh), F32) for n in range(2 * nh))
    state = lax.cond(
        i == 0,
        lambda: tile(0, bq, state0, tri_ref, 0),
        lambda: tile(pl.multiple_of((i - 1) * bq, bq), 2 * bq, state0, tri2_ref, bq))
    n_left = jnp.maximum(i - 1, 0)

    def top_carry(st):
        m = jnp.max(st[0])
        for h in range(1, nh):
            m = jnp.maximum(m, jnp.max(st[2 * h]))
        return m

    def cond(c):
        return jnp.logical_and(c[0] < n_left, c[1] > SB_EXP_ZERO)

    def body(c):
        st = tile(pl.multiple_of((n_left - 1 - c[0]) * bq, bq), bq, c[2], tri_ref, None)
        return c[0] + 1, top_carry(st), st

    _, _, state = lax.while_loop(cond, body, (jnp.int32(0), top_carry(state), state))
    o_ref[...] = jnp.concatenate([state[2 * h + 1] for h in range(nh)], axis=1).astype(o_ref.dtype)


def _sb_attention(qkv, batch, seq, heads):
    t = qkv.shape[0]
    bq = min(SB_BQ, seq)
    nq = seq // bq
    nh = SB_HEADS_PER_STEP
    hw = nh * HEAD_DIM
    ng = heads // nh
    def suffix_op(width):
        rj = lax.broadcasted_iota(jnp.int32, (2 * width, width), 0)
        cs = lax.broadcasted_iota(jnp.int32, (2 * width, width), 1)
        return jnp.where((rj % width) > cs, 1.0, 0.0).astype(BF16)

    return pl.pallas_call(
        functools.partial(_sb_kernel, bq=bq, nh=nh, scale=HEAD_DIM ** -0.5),
        grid=(batch, ng, nq),
        in_specs=[
            pl.BlockSpec((bq, hw), lambda b, h, i: (b * nq + i, h)),
            pl.BlockSpec((seq, hw), lambda b, h, i: (b, ng + h)),
            pl.BlockSpec((seq, hw), lambda b, h, i: (b, 2 * ng + h)),
            pl.BlockSpec((2 * bq, bq), lambda b, h, i: (0, 0)),
            pl.BlockSpec((4 * bq, 2 * bq), lambda b, h, i: (0, 0)),
        ],
        out_specs=pl.BlockSpec((bq, hw), lambda b, h, i: (b * nq + i, h)),
        out_shape=jax.ShapeDtypeStruct((t, heads * HEAD_DIM), BF16),
        compiler_params=_params("arbitrary", "arbitrary", "arbitrary"),
        name="stickbreak_attn",
    )(qkv, qkv, qkv, suffix_op(bq), suffix_op(2 * bq))


def _rglru_kernel(x_ref, g_ref, cw_ref, cb_ref, wa_ref, ba_ref, wx_ref, bx_ref, lam_ref, o_ref, xp_ref,
                  *, seq, cb, tc, kw):
    pad = SUBLANE
    xp_ref[0:pad, :] = jnp.zeros((pad, cb), F32)
    xp_ref[pad:, :] = x_ref[...]
    log_lam = _log_sigmoid(lam_ref[...])
    sub_iota = lax.broadcasted_iota(jnp.int32, (SUBLANE, cb), 0)
    ngroup = cb // LANE

    def chunk(ci, h):
        t0 = pl.multiple_of(ci * tc, tc)
        win = xp_ref[pl.ds(t0, tc + pad), :]
        xc = cb_ref[...] + cw_ref[0:1, :] * win[pad - kw + 1:pad - kw + 1 + tc, :]
        for k in range(1, kw):
            off = pad - kw + 1 + k
            xc = xc + cw_ref[k:k + 1, :] * win[off:off + tc, :]
        xcb = xc.astype(BF16)
        ra = jnp.concatenate(
            [_dot(xcb[:, q * LANE:(q + 1) * LANE], wa_ref[q].astype(BF16)) for q in range(ngroup)], axis=1)
        rx = jnp.concatenate(
            [_dot(xcb[:, q * LANE:(q + 1) * LANE], wx_ref[q].astype(BF16)) for q in range(ngroup)], axis=1)
        r = _sigmoid(ra + ba_ref[...])
        gi = _sigmoid(rx + bx_ref[...])
        log_a = RG_C * r * log_lam
        a = jnp.exp(log_a)
        u = jnp.sqrt(-jnp.tanh(log_a) * (a * a + 1.0)) * (gi * xc)
        gate = _gelu_tanh(g_ref[pl.ds(t0, tc), :])
        outs = []
        for gidx in range(tc // SUBLANE):
            av = a[gidx * SUBLANE:(gidx + 1) * SUBLANE, :]
            bv = u[gidx * SUBLANE:(gidx + 1) * SUBLANE, :]
            for sh in (1, 2, 4):
                a_s = pltpu.roll(av, sh, axis=0)
                b_s = pltpu.roll(bv, sh, axis=0)
                m = sub_iota >= sh
                bv = jnp.where(m, av * b_s + bv, bv)
                av = jnp.where(m, av * a_s, av)
            hv = av * h + bv
            outs.append(hv)
            h = jnp.broadcast_to(hv[SUBLANE - 1:SUBLANE, :], (SUBLANE, cb))
        hs = jnp.concatenate(outs, axis=0)
        o_ref[pl.ds(t0, tc), :] = (hs * gate).astype(o_ref.dtype)
        return h

    lax.fori_loop(0, seq // tc, chunk, jnp.zeros((SUBLANE, cb), F32))


def _block_diag_pairs(w):
    nblk, c, _ = w.shape
    w2 = w.reshape(nblk // 2, 2, c, c)
    z = jnp.zeros((nblk // 2, c, c), w.dtype)
    top = jnp.concatenate([w2[:, 0], z], axis=2)
    bot = jnp.concatenate([z, w2[:, 1]], axis=2)
    return jnp.concatenate([top, bot], axis=1)


def _rglru(rg, batch, seq, conv_w, conv_b, wa, ba, wx, bx, lam):
    t, c2 = rg.shape
    c = c2 // 2
    cb = min(RG_CB, c)
    tc = min(RG_TC, seq)
    ncb = c // cb
    kw = conv_w.shape[0]
    gpb = cb // LANE
    vec = pl.BlockSpec((1, cb), lambda b, j: (0, j))
    return pl.pallas_call(
        functools.partial(_rglru_kernel, seq=seq, cb=cb, tc=tc, kw=kw),
        grid=(batch, ncb),
        in_specs=[
            pl.BlockSpec((seq, cb), lambda b, j: (b, j)),
            pl.BlockSpec((seq, cb), lambda b, j: (b, ncb + j)),
            pl.BlockSpec((kw, cb), lambda b, j: (0, j)),
            vec,
            pl.BlockSpec((gpb, LANE, LANE), lambda b, j: (j, 0, 0)),
            vec,
            pl.BlockSpec((gpb, LANE, LANE), lambda b, j: (j, 0, 0)),
            vec,
            vec,
        ],
        out_specs=pl.BlockSpec((seq, cb), lambda b, j: (b, j)),
        out_shape=jax.ShapeDtypeStruct((t, c), BF16),
        scratch_shapes=[pltpu.VMEM((seq + SUBLANE, cb), F32)],
        compiler_params=_params("arbitrary", "arbitrary"),
        name="rglru",
    )(rg, rg, conv_w, conv_b.reshape(1, c), _block_diag_pairs(wa), ba.reshape(1, c),
      _block_diag_pairs(wx), bx.reshape(1, c), lam.reshape(1, c))


def _compress_kernel(x_ref, pos_ref, w1_ref, w2_ref, o_ref, *, ngrp):
    st = CMP_STRIDE
    half = st * HEAD_DIM
    xs = [x_ref[pl.ds(r, ngrp, stride=st), :] for r in range(st)]
    pos = pos_ref[0]
    x0 = jnp.concatenate([xs[r] + pos[r:r + 1, :] for r in range(st)], axis=1).astype(BF16)
    x1 = jnp.concatenate([xs[r] + pos[st + r:st + r + 1, :] for r in range(st)], axis=1).astype(BF16)
    p0 = _dot(x0, w1_ref[0, 0:half, :].astype(BF16))
    p1 = _dot(x1, w1_ref[0, half:2 * half, :].astype(BF16))
    pre = p0 + pltpu.roll(p1, ngrp - 1, axis=0)
    out = _dot(_gelu_tanh(pre).astype(BF16), w2_ref[0].astype(BF16))
    rown = lax.broadcasted_iota(jnp.int32, out.shape, 0)
    o_ref[...] = jnp.where(rown < ngrp - 1, out, 0.0).astype(o_ref.dtype)


def _compress(kcvc, batch, seq, pos, w1, w2):
    assert CMP_BLOCK == 2 * CMP_STRIDE
    g2 = kcvc.shape[1] // HEAD_DIM
    per = g2 // 2
    ngrp = seq // CMP_STRIDE
    return pl.pallas_call(
        functools.partial(_compress_kernel, ngrp=ngrp),
        grid=(batch, g2),
        in_specs=[
            pl.BlockSpec((seq, HEAD_DIM), lambda b, j: (b, j)),
            pl.BlockSpec((1, CMP_BLOCK, HEAD_DIM), lambda b, j: (j // per, 0, 0)),
            pl.BlockSpec((1, CMP_BLOCK * HEAD_DIM, HEAD_DIM), lambda b, j: (j // per, 0, 0)),
            pl.BlockSpec((1, HEAD_DIM, HEAD_DIM), lambda b, j: (j // per, 0, 0)),
        ],
        out_specs=pl.BlockSpec((ngrp, HEAD_DIM), lambda b, j: (b * g2 + j, 0)),
        out_shape=jax.ShapeDtypeStruct((batch * g2 * ngrp, HEAD_DIM), BF16),
        compiler_params=_params("arbitrary", "arbitrary"),
        name="nsa_compress",
    )(kcvc, pos, w1, w2)


def _softmax_parts(s2, bias):
    sb = s2 + bias
    m = jnp.max(sb, axis=-1, keepdims=True)
    e = jnp.exp2(sb - m)
    return e, jnp.sum(e, axis=-1, keepdims=True)


def _nsa_kernel(q_ref, kc_ref, vc_ref, ks_ref, vs_ref, kw_ref, vw_ref, g_ref, ov_ref, ex_ref, o_ref,
                acc_ref, inv_ref, *, seq, hg, n_slc, scale):
    bq, bk, dh = NSA_BQ, NSA_BK, HEAD_DIM
    i = pl.program_id(2)
    q0 = i * bq
    qb = q_ref[...]
    qs = jnp.concatenate([qb[:, h * dh:(h + 1) * dh] for h in range(hg)], axis=0)
    qpos1 = q0 + lax.broadcasted_iota(jnp.int32, (bq, 1), 0)
    qpos = jnp.concatenate([qpos1] * hg, axis=0)

    scale2 = scale * LOG2E
    tile_heads = lambda a: jnp.concatenate([a] * hg, axis=0)

    ncmp = kc_ref.shape[0]
    s_c = _dot_nt(qs, kc_ref[...]) * scale2
    n_idx = lax.broadcasted_iota(jnp.int32, (1, ncmp), 1)
    bias_c = jnp.where(n_idx * CMP_STRIDE + (CMP_BLOCK - 1) <= qpos1, 0.0, NEG_BIG)
    e_c, l_c = _softmax_parts(s_c, tile_heads(bias_c))
    p_c = e_c * jnp.where(qpos >= CMP_BLOCK - 1, 1.0 / l_c, 0.0)
    o_c = _dot(p_c.astype(BF16), vc_ref[...])

    p_sum = p_c[0:bq, :]
    for h in range(1, hg):
        p_sum = p_sum + p_c[h * bq:(h + 1) * bq, :]
    overlap_t = ov_ref[...]
    p_hi, p_lo = _split_bf16(p_sum)
    imp_t = (_dot_nt(overlap_t, p_hi) + _dot_nt(overlap_t, p_lo))[0:n_slc, :]

    blk = lax.broadcasted_iota(jnp.int32, (n_slc, bq), 0)
    qp_l = q0 + lax.broadcasted_iota(jnp.int32, (n_slc, bq), 1)
    cur = qp_l // SLC_BLOCK
    forced = (blk == 0) | (blk == cur) | (blk == cur - 1)
    valid = blk * SLC_BLOCK <= qp_l
    rank = jnp.where(valid, imp_t + FORCE_BONUS * jnp.where(forced, 1.0, 0.0), -jnp.inf)
    ahead = jnp.zeros((n_slc, bq), F32)
    for jp in range(n_slc):
        other = rank[jp:jp + 1, :]
        beats = (other > rank) | ((other == rank) & (blk > jp))
        ahead = ahead + jnp.where(beats, 1.0, 0.0)
    k_sel = min(N_SEL, n_slc)
    sel_t = jnp.where(valid & (ahead < k_sel), 1.0, 0.0)
    sel_t = jnp.concatenate([sel_t, jnp.zeros((LANE - n_slc, bq), F32)], axis=0)
    sel = sel_t.T.astype(BF16)

    n_need = (q0 + bq + bk - 1) // bk
    for v in range(1, seq // bk + 1):
        nk = v * bk

        @pl.when(n_need == v)
        def _():
            s = _dot_nt(qs, ks_ref[0:nk, :]) * scale2
            picked = _dot(sel, ex_ref[:, 0:nk])
            kpos = lax.broadcasted_iota(jnp.int32, (1, nk), 1)
            bias = jnp.where((picked > 0.5) & (kpos <= qpos1), 0.0, NEG_BIG)
            e_s, l_s = _softmax_parts(s, tile_heads(bias))
            acc_ref[...] = _dot(e_s.astype(BF16), vs_ref[0:nk, :])
            inv_ref[...] = 1.0 / l_s

    acc_s = acc_ref[...]
    inv_s = inv_ref[...]

    span = min(WINDOW + bq, seq)
    w0 = pl.multiple_of(jnp.maximum(jnp.minimum(q0 - WINDOW, seq - span), 0), bq)
    kwt = kw_ref[pl.ds(w0, span), :]
    vwt = vw_ref[pl.ds(w0, span), :]
    s_w = _dot_nt(qs, kwt) * scale2
    kpos_w = w0 + lax.broadcasted_iota(jnp.int32, (1, span), 1)
    bias_w = jnp.where((kpos_w <= qpos1) & (kpos_w > qpos1 - WINDOW), 0.0, NEG_BIG)
    e_w, l_w = _softmax_parts(s_w, tile_heads(bias_w))
    acc_w = _dot(e_w.astype(BF16), vwt)
    inv_w = 1.0 / l_w

    gates = _sigmoid(g_ref[...])
    outs = []
    for h in range(hg):
        r = slice(h * bq, (h + 1) * bq)
        c = h * N_BRANCH
        outs.append(gates[:, c:c + 1] * o_c[r, :] + (gates[:, c + 1:c + 2] * inv_s[r, :]) * acc_s[r, :]
                    + (gates[:, c + 2:c + 3] * inv_w[r, :]) * acc_w[r, :])
    o_ref[...] = jnp.concatenate(outs, axis=1).astype(o_ref.dtype)


def _nsa_attention(q, kv4, cmp, gate_logits, batch, seq):
    t, width = q.shape
    g = NSA_KV_HEADS
    hg = width // (g * HEAD_DIM)
    bq = NSA_BQ
    nb = seq // bq
    ncmp = seq // CMP_STRIDE
    n_slc = seq // SLC_BLOCK
    dh = HEAD_DIM
    full = lambda off: pl.BlockSpec((seq, dh), lambda b, gi, i: (b, off * g + gi))
    jj = lax.broadcasted_iota(jnp.int32, (LANE, ncmp), 0)
    cstart = lax.broadcasted_iota(jnp.int32, (LANE, ncmp), 1) * CMP_STRIDE
    overlap_t = jnp.where((cstart < (jj + 1) * SLC_BLOCK) & (cstart + CMP_BLOCK > jj * SLC_BLOCK)
                          & (jj < n_slc), 1.0, 0.0).astype(BF16)
    expand = jnp.where(lax.broadcasted_iota(jnp.int32, (LANE, seq), 1) // SLC_BLOCK
                       == lax.broadcasted_iota(jnp.int32, (LANE, seq), 0), 1.0, 0.0).astype(BF16)
    const = lambda shape: pl.BlockSpec(shape, lambda b, gi, i: (0, 0))
    return pl.pallas_call(
        functools.partial(_nsa_kernel, seq=seq, hg=hg, n_slc=n_slc, scale=dh ** -0.5),
        grid=(batch, g, nb),
        in_specs=[
            pl.BlockSpec((bq, hg * dh), lambda b, gi, i: (b * nb + i, gi)),
            pl.BlockSpec((ncmp, dh), lambda b, gi, i: (b * 2 * g + gi, 0)),
            pl.BlockSpec((ncmp, dh), lambda b, gi, i: (b * 2 * g + g + gi, 0)),
            full(0), full(1), full(2), full(3),
            pl.BlockSpec((bq, LANE), lambda b, gi, i: (b * nb + i, gi)),
            const((LANE, ncmp)),
            const((LANE, seq)),
        ],
        out_specs=pl.BlockSpec((bq, hg * dh), lambda b, gi, i: (b * nb + i, gi)),
        out_shape=jax.ShapeDtypeStruct((t, width), BF16),
        scratch_shapes=[pltpu.VMEM((hg * bq, dh), F32), pltpu.VMEM((hg * bq, 1), F32)],
        compiler_params=_params("arbitrary", "arbitrary", "arbitrary"),
        name="nsa_attn",
    )(q, cmp, cmp, kv4, kv4, kv4, kv4, gate_logits, overlap_t, expand)


def _gather_kernel(lo_ref, hi_ref, src_ref, h_ref, o_ref, *, chunk):
    i = pl.program_id(0)
    rows = o_ref.shape[0]
    src = src_ref[...]
    lane = lax.broadcasted_iota(jnp.int32, (rows, chunk), 1)
    o_ref[...] = jnp.zeros(o_ref.shape, o_ref.dtype)

    def body(c, carry):
        c0 = pl.multiple_of(c * chunk, chunk)
        onehot = jnp.where(src - c0 == lane, 1.0, 0.0).astype(BF16)
        o_ref[...] += _dot(onehot, h_ref[pl.ds(c0, chunk), :]).astype(o_ref.dtype)
        return carry

    lax.fori_loop(lo_ref[i], hi_ref[i], body, 0)


def _gather_rows(h, src, n_rows):
    t, d = h.shape
    bt, chunk = GATHER_ROWS, GATHER_CHUNK
    nt = n_rows // bt
    src2 = src.reshape(nt, bt)
    live = src2 >= 0
    lo = jnp.min(jnp.where(live, src2, t), axis=1) // chunk
    hi = jnp.where(jnp.any(live, axis=1), jnp.max(src2, axis=1) // chunk + 1, lo)
    grid_spec = pltpu.PrefetchScalarGridSpec(
        num_scalar_prefetch=2,
        grid=(nt,),
        in_specs=[
            pl.BlockSpec((bt, 1), lambda i, lo, hi: (i, 0)),
            pl.BlockSpec((t, d), lambda i, lo, hi: (0, 0), pipeline_mode=pl.Buffered(1)),
        ],
        out_specs=pl.BlockSpec((bt, d), lambda i, lo, hi: (i, 0)),
    )
    return pl.pallas_call(
        functools.partial(_gather_kernel, chunk=chunk),
        grid_spec=grid_spec,
        out_shape=jax.ShapeDtypeStruct((n_rows, d), h.dtype),
        compiler_params=_params("arbitrary"),
        name="moe_dispatch",
    )(jnp.minimum(lo, hi).astype(jnp.int32), hi.astype(jnp.int32), src.reshape(n_rows, 1), h)


def _router_kernel(l_ref, o_ref, *, n_exp):
    x = l_ref[...]
    lane = lax.broadcasted_iota(jnp.int32, x.shape, 1)
    xm = jnp.where(lane < n_exp, x, -jnp.inf)
    v0 = jnp.max(xm, axis=-1, keepdims=True)
    i0 = jnp.min(jnp.where(xm == v0, lane, LANE), axis=-1, keepdims=True)
    xm = jnp.where(lane == i0, -jnp.inf, xm)
    v1 = jnp.max(xm, axis=-1, keepdims=True)
    i1 = jnp.min(jnp.where(xm == v1, lane, LANE), axis=-1, keepdims=True)
    e = jnp.exp(v1 - v0)
    p0 = 1.0 / (1.0 + e)
    out = jnp.where(lane == 0, p0, jnp.where(lane == 1, e * p0, jnp.where(
        lane == 2, i0.astype(F32), jnp.where(lane == 3, i1.astype(F32), 0.0))))
    o_ref[...] = out


def _router_top2(logits, n_exp):
    t = logits.shape[0]
    bt = ROW_TILE
    out = pl.pallas_call(
        functools.partial(_router_kernel, n_exp=n_exp),
        grid=(t // bt,),
        in_specs=[pl.BlockSpec((bt, LANE), lambda i: (i, 0))],
        out_specs=pl.BlockSpec((bt, LANE), lambda i: (i, 0)),
        out_shape=jax.ShapeDtypeStruct((t, LANE), F32),
        compiler_params=_params("arbitrary"),
        name="router_top2",
    )(logits)
    return out[:, 0:TOP_K], out[:, TOP_K:2 * TOP_K].astype(jnp.int32)


def _route(probs, top_i, n_experts):
    t = top_i.shape[0]
    flat_e = top_i.reshape(-1)
    onehot = (flat_e[:, None] == jnp.arange(n_experts)[None, :]).astype(jnp.int32)
    rank = jnp.take_along_axis(jnp.cumsum(onehot, axis=0), flat_e[:, None], axis=1)[:, 0] - 1
    counts = jnp.sum(onehot, axis=0)
    padded = (counts + FFN_SUB - 1) // FFN_SUB * FFN_SUB
    end = jnp.cumsum(padded)
    start = end - padded
    n_rows = -(-(TOP_K * t + n_experts * (FFN_SUB - 1)) // FFN_BM) * FFN_BM
    n_tiles = n_rows // FFN_BM
    dest = (start[flat_e] + rank).astype(jnp.int32)
    src = jnp.full((n_rows,), -1, jnp.int32).at[dest].set(jnp.arange(TOP_K * t, dtype=jnp.int32) // TOP_K)
    cuts = jnp.sort(jnp.concatenate([start, jnp.arange(n_tiles) * FFN_BM]))
    nxt = jnp.concatenate([cuts[1:], jnp.full((1,), n_rows, cuts.dtype)])
    seg_rows = jnp.maximum(jnp.minimum(nxt, end[-1]) - cuts, 0)
    seg_tile = jnp.minimum(cuts // FFN_BM, n_tiles - 1)
    seg_expert = jnp.minimum(jnp.searchsorted(end, cuts, side="right"), n_experts - 1)
    i32 = lambda a: a.astype(jnp.int32)
    return probs, dest.reshape(t, TOP_K), src, i32(seg_tile), i32(seg_expert), i32(cuts % FFN_BM), i32(seg_rows)


def kernel(x, c, ada_mix_w, ada_mix_b, norm_mix_g, ada_ffn_w, ada_ffn_b, norm_ffn_g, even_in_w, rg_conv_w, rg_conv_b, rg_wa, rg_ba, rg_wx, rg_bx, rg_lambda, even_out_w, dense_w_gate, dense_w_up, dense_w_down, nsa_in_w, cmp_pos_k, cmp_pos_v, cmp_k_w1, cmp_k_w2, cmp_v_w1, cmp_v_w2, nsa_out_w, router_w, moe_w_gate, moe_w_up, moe_w_down, final_norm_g):
    batch, seq, d = x.shape
    t = batch * seq
    depth = ada_mix_w.shape[0]
    xf = x.reshape(t, d)

    c_pad = jnp.pad(c, ((0, (-batch) % SUBLANE), (0, 0)))
    m_mix = _adaln(c_pad, ada_mix_w, ada_mix_b)
    m_ffn = _adaln(c_pad, ada_ffn_w, ada_ffn_b)

    def mods(m, layer):
        v = m[layer, :batch].reshape(batch, 1, 3, d)
        return v[:, :, 0], v[:, :, 1], v[:, :, 2]

    def norm_step(xf, pending, g, mod, out_dtype=BF16):
        if pending is not None and pending[0] == "experts":
            _, y, dest, probs, gate = pending
            return _combine_norm(xf, y, dest, probs, gate, g, seq, mod=mod, out_dtype=out_dtype)
        res = None if pending is None else pending[1:]
        return _normmod(xf, g, seq, res=res, mod=mod, out_dtype=out_dtype)

    pending = None
    for layer in range(depth):
        j = layer // 2
        shift, scale, gate = mods(m_mix, layer)
        xf, h = norm_step(xf, pending, norm_mix_g[layer], (shift, scale))
        if layer % 2 == 0:
            w_in = even_in_w[j]
            sbw = (w_in.shape[1] - 2 * rg_conv_w.shape[2]) // 3
            heads = sbw // HEAD_DIM
            qkv = _matmul([h], w_in, 0, 3 * sbw, BF16)
            rg = _matmul([h], w_in, 3 * sbw, w_in.shape[1] - 3 * sbw, F32)
            o_a = _sb_attention(qkv, batch, seq, heads)
            o_b = _rglru(rg, batch, seq, rg_conv_w[j], rg_conv_b[j], rg_wa[j], rg_ba[j], rg_wx[j],
                         rg_bx[j], rg_lambda[j])
            xf = _matmul([o_a, o_b], even_out_w[j], 0, d, F32, residual=(xf, gate, seq))
        else:
            w_in = nsa_in_w[j]
            g = NSA_KV_HEADS
            kvw = g * HEAD_DIM
            nsa_w = nsa_out_w.shape[1]
            hg = nsa_w // kvw
            q = _matmul([h], w_in, 0, nsa_w, BF16)
            kcvc = _matmul([h], w_in, nsa_w, 2 * kvw, F32)
            kv4 = _matmul([h], w_in, nsa_w + 2 * kvw, 4 * kvw, BF16)
            wg = w_in[:, nsa_w + 6 * kvw:].reshape(d, g, hg * N_BRANCH)
            wg = jnp.pad(wg, ((0, 0), (0, 0), (0, LANE - hg * N_BRANCH))).reshape(d, g * LANE)
            gl = _matmul([h], wg, 0, g * LANE, F32)
            cmp = _compress(kcvc, batch, seq, jnp.stack([cmp_pos_k[j], cmp_pos_v[j]]),
                            jnp.stack([cmp_k_w1[j], cmp_v_w1[j]]), jnp.stack([cmp_k_w2[j], cmp_v_w2[j]]))
            o = _nsa_attention(q, kv4, cmp, gl, batch, seq)
            xf = _matmul([o], nsa_out_w[j], 0, d, F32, residual=(xf, gate, seq))
        pending = None

        shift, scale, gate = mods(m_ffn, layer)
        xf, h = norm_step(xf, pending, norm_ffn_g[layer], (shift, scale))
        if layer % 2 == 0:
            nt = t // FFN_BM
            ffn = _ffn(h, dense_w_gate[j:j + 1], dense_w_up[j:j + 1], dense_w_down[j:j + 1],
                       jnp.arange(nt, dtype=jnp.int32), jnp.zeros((nt,), jnp.int32),
                       jnp.zeros((nt,), jnp.int32), jnp.full((nt,), FFN_BM, jnp.int32))
            pending = ("dense", ffn, gate)
        else:
            n_exp = router_w.shape[2]
            rw = jnp.pad(router_w[j], ((0, 0), (0, LANE - n_exp)))
            probs, top_i = _router_top2(_matmul([h], rw, 0, LANE, F32), n_exp)
            probs, dest, src, seg_tile, seg_expert, seg_row0, seg_rows = _route(probs, top_i, n_exp)
            h_sorted = _gather_rows(h, src, src.shape[0])
            y = _ffn(h_sorted, moe_w_gate[j], moe_w_up[j], moe_w_down[j], seg_tile, seg_expert, seg_row0,
                     seg_rows)
            pending = ("experts", y, dest, probs, gate)

    _, out = norm_step(xf, pending, final_norm_g, None, F32)
    return out.reshape(batch, seq, d)
```

```python
import functools

import jax
import jax.numpy as jnp
from jax import lax
from jax.experimental import pallas as pl
from jax.experimental.pallas import tpu as pltpu

F32 = jnp.float32
BF16 = jnp.bfloat16

LANE = 128
SUBLANE = 8
VMEM_LIMIT_BYTES = 56 * 1024 * 1024

HEAD_DIM = 128
EPS = 1e-6
RG_C = 8.0
NSA_KV_HEADS = 4
N_BRANCH = 3
CMP_BLOCK = 32
CMP_STRIDE = 16
SLC_BLOCK = 64
N_SEL = 8
WINDOW = 512
FORCE_BONUS = 1e6
TOP_K = 2
NEG_BIG = -1e30
LOG2E = 1.4426950408889634
SB_EXP_ZERO = -110.0

ROW_TILE = 512
COMBINE_ROWS = 256
MM_BM = 2048
MM_BN = 512
FFN_BM = 2048
FFN_SUB = 256
FFN_NCOL = 4
FFN_BF = 256
GATHER_ROWS = 256
GATHER_CHUNK = 512
SB_BQ = 256
SB_HEADS_PER_STEP = 4
NSA_BQ = 128
NSA_BK = 512
RG_CB = 256
RG_TC = 256


def _params(*sem):
    return pltpu.CompilerParams(dimension_semantics=sem, vmem_limit_bytes=VMEM_LIMIT_BYTES)


def _dot(a, b):
    return jnp.dot(a, b, preferred_element_type=F32)


def _dot_nt(a, b):
    return lax.dot_general(a, b, (((1,), (1,)), ((), ())), preferred_element_type=F32)


def _sigmoid(x):
    return 1.0 / (1.0 + jnp.exp(-x))


def _log_sigmoid(x):
    return jnp.minimum(x, 0.0) - jnp.log1p(jnp.exp(-jnp.abs(x)))


def _gelu_tanh(x):
    return 0.5 * x * (1.0 + jnp.tanh(0.7978845608028654 * (x + 0.044715 * (x * x * x))))


def _split_bf16(x):
    hi = x.astype(BF16)
    lo = (x - hi.astype(F32)).astype(BF16)
    return hi, lo


def _adaln_kernel(c_ref, w_ref, b_ref, o_ref):
    c = c_ref[...]
    s = (c * _sigmoid(c)).astype(BF16)
    o_ref[0] = _dot(s, w_ref[0].astype(BF16)) + b_ref[0]


def _adaln(c_pad, w, b, bn=1024):
    depth, d, n3 = w.shape
    rows = c_pad.shape[0]
    return pl.pallas_call(
        _adaln_kernel,
        grid=(depth, n3 // bn),
        in_specs=[
            pl.BlockSpec((rows, d), lambda l, j: (0, 0)),
            pl.BlockSpec((1, d, bn), lambda l, j: (l, 0, j)),
            pl.BlockSpec((1, 1, bn), lambda l, j: (l, 0, j)),
        ],
        out_specs=pl.BlockSpec((1, rows, bn), lambda l, j: (l, 0, j)),
        out_shape=jax.ShapeDtypeStruct((depth, rows, n3), F32),
        compiler_params=_params("arbitrary", "arbitrary"),
        name="adaln",
    )(c_pad, w, b.reshape(depth, 1, n3))


def _normmod_kernel(*refs, has_res, has_mod):
    it = iter(refs)
    x_ref = next(it)
    if has_res:
        y_ref, gate_ref = next(it), next(it)
    g_ref = next(it)
    if has_mod:
        shift_ref, scale_ref = next(it), next(it)
    if has_res:
        xo_ref = next(it)
    h_ref = next(it)

    x = x_ref[...]
    if has_res:
        x = x + gate_ref[0] * y_ref[...]
        xo_ref[...] = x
    ms = jnp.mean(x * x, axis=-1, keepdims=True)
    h = x * lax.rsqrt(ms + EPS) * g_ref[...]
    if has_mod:
        h = h * (1.0 + scale_ref[0]) + shift_ref[0]
    h_ref[...] = h.astype(h_ref.dtype)


def _normmod(x, g, seq, res=None, mod=None, out_dtype=BF16):
    t, d = x.shape
    bt = ROW_TILE
    per_b = seq // bt
    row = pl.BlockSpec((bt, d), lambda i: (i, 0))
    vec = pl.BlockSpec((1, 1, d), lambda i: (i // per_b, 0, 0))
    args, specs = [x], [row]
    if res is not None:
        args += [res[0], res[1]]
        specs += [row, vec]
    args.append(g.reshape(1, d))
    specs.append(pl.BlockSpec((1, d), lambda i: (0, 0)))
    if mod is not None:
        args += [mod[0], mod[1]]
        specs += [vec, vec]
    out_shape, out_specs = [], []
    if res is not None:
        out_shape.append(jax.ShapeDtypeStruct((t, d), F32))
        out_specs.append(row)
    out_shape.append(jax.ShapeDtypeStruct((t, d), out_dtype))
    out_specs.append(row)
    outs = pl.pallas_call(
        functools.partial(_normmod_kernel, has_res=res is not None, has_mod=mod is not None),
        grid=(t // bt,),
        in_specs=specs,
        out_specs=out_specs,
        out_shape=out_shape,
        compiler_params=_params("arbitrary"),
        name="normmod",
    )(*args)
    return outs if res is not None else (x, outs[0])


def _combine_norm_kernel(dest_ref, *refs, has_mod, bt, n_steps):
    it = iter(refs)
    x_ref, y_hbm, p_ref, gate_ref, g_ref = (next(it) for _ in range(5))
    if has_mod:
        shift_ref, scale_ref, xo_ref = next(it), next(it), next(it)
    h_ref, buf, sem = next(it), next(it), next(it)
    i = pl.program_id(0)

    def row_copy(src_row, slot, k, r):
        return pltpu.make_async_copy(y_hbm.at[pl.ds(src_row, 1), :], buf.at[slot, k, pl.ds(r, 1), :],
                                     sem.at[slot])

    def start_tile(tile, slot):
        base = tile * (bt * TOP_K)

        def body(r, carry):
            for k in range(TOP_K):
                row_copy(dest_ref[base + r * TOP_K + k], slot, k, r).start()
            return carry

        lax.fori_loop(0, bt, body, 0, unroll=8)

    @pl.when(i == 0)
    def _():
        start_tile(0, 0)

    @pl.when(i + 1 < n_steps)
    def _():
        start_tile(i + 1, (i + 1) % 2)

    slot = i % 2

    def wait_row(r, carry):
        for k in range(TOP_K):
            row_copy(0, slot, k, r).wait()
        return carry

    lax.fori_loop(0, bt, wait_row, 0, unroll=8)
    p = p_ref[...]
    y = p[:, 0:1] * buf[slot, 0]
    for k in range(1, TOP_K):
        y = y + p[:, k:k + 1] * buf[slot, k]
    x = x_ref[...] + gate_ref[0] * y
    ms = jnp.mean(x * x, axis=-1, keepdims=True)
    h = x * lax.rsqrt(ms + EPS) * g_ref[...]
    if has_mod:
        xo_ref[...] = x
        h = h * (1.0 + scale_ref[0]) + shift_ref[0]
    h_ref[...] = h.astype(h_ref.dtype)


def _combine_norm(x, y, dest, probs, gate, g, seq, mod=None, out_dtype=BF16):
    t, d = x.shape
    bt = COMBINE_ROWS
    n_steps = t // bt
    per_b = seq // bt
    row = pl.BlockSpec((bt, d), lambda i, dst: (i, 0))
    vec = pl.BlockSpec((1, 1, d), lambda i, dst: (i // per_b, 0, 0))
    args = [x, y, probs, gate, g.reshape(1, d)]
    specs = [row, pl.BlockSpec(memory_space=pl.ANY), pl.BlockSpec((bt, TOP_K), lambda i, dst: (i, 0)), vec,
             pl.BlockSpec((1, d), lambda i, dst: (0, 0))]
    out_shape, out_specs = [], []
    if mod is not None:
        args += [mod[0], mod[1]]
        specs += [vec, vec]
        out_shape.append(jax.ShapeDtypeStruct((t, d), F32))
        out_specs.append(row)
    out_shape.append(jax.ShapeDtypeStruct((t, d), out_dtype))
    out_specs.append(row)
    grid_spec = pltpu.PrefetchScalarGridSpec(
        num_scalar_prefetch=1,
        grid=(n_steps,),
        in_specs=specs,
        out_specs=out_specs,
        scratch_shapes=[pltpu.VMEM((2, TOP_K, bt, d), F32), pltpu.SemaphoreType.DMA((2,))],
    )
    outs = pl.pallas_call(
        functools.partial(_combine_norm_kernel, has_mod=mod is not None, bt=bt, n_steps=n_steps),
        grid_spec=grid_spec,
        out_shape=out_shape,
        compiler_params=_params("arbitrary"),
        name="moe_combine_norm",
    )(dest.reshape(-1), *args)
    return (outs[0], outs[1]) if mod is not None else (None, outs[0])


def _mm_kernel(*refs, ks, has_res):
    a_refs = refs[: len(ks)]
    rest = refs[len(ks):]
    w_ref = rest[0]
    if has_res:
        x_ref, gate_ref = rest[1], rest[2]
    o_ref, wb_ref = rest[-2], rest[-1]

    wb_ref[...] = w_ref[...].astype(BF16)

    acc = None
    k0 = 0
    for a_ref, k in zip(a_refs, ks):
        part = _dot(a_ref[...], wb_ref[k0:k0 + k, :])
        acc = part if acc is None else acc + part
        k0 += k
    if has_res:
        acc = x_ref[...] + gate_ref[0] * acc
    o_ref[...] = acc.astype(o_ref.dtype)


def _matmul(a_list, w, col0, n, out_dtype, residual=None):
    m = a_list[0].shape[0]
    ks = tuple(a.shape[1] for a in a_list)
    ktot = sum(ks)
    bm = min(MM_BM, m)
    bn = min(MM_BN, n)
    cb0 = col0 // bn
    args = list(a_list) + [w]
    specs = [pl.BlockSpec((bm, k), lambda j, i: (i, 0)) for k in ks]
    specs.append(pl.BlockSpec((ktot, bn), lambda j, i: (0, cb0 + j)))
    if residual is not None:
        x, gate, seq = residual
        bm = min(bm, seq)
        per_b = seq // bm
        specs[: len(ks)] = [pl.BlockSpec((bm, k), lambda j, i: (i, 0)) for k in ks]
        args += [x, gate]
        specs += [pl.BlockSpec((bm, bn), lambda j, i: (i, j)),
                  pl.BlockSpec((1, 1, bn), lambda j, i: (i // per_b, 0, j))]
    return pl.pallas_call(
        functools.partial(_mm_kernel, ks=ks, has_res=residual is not None),
        grid=(n // bn, m // bm),
        in_specs=specs,
        out_specs=pl.BlockSpec((bm, bn), lambda j, i: (i, j)),
        out_shape=jax.ShapeDtypeStruct((m, n), out_dtype),
        scratch_shapes=[pltpu.VMEM((ktot, bn), BF16)],
        compiler_params=_params("arbitrary", "arbitrary"),
        name="proj_matmul",
    )(*args)


def _ffn_kernel(st_ref, se_ref, r0_ref, nr_ref, h_ref, wg_ref, wu_ref, wd_ref, o_ref, wgb, wub, wdb,
                *, sub, nsub, ncol):
    del se_ref
    k = pl.program_id(0)
    f = pl.program_id(1)
    nr = nr_ref[k]
    r0 = pl.multiple_of(r0_ref[k], sub)
    bm, d = o_ref.shape
    nchunk = (nr + sub - 1) // sub
    first = jnp.logical_or(k == 0, st_ref[k] != st_ref[jnp.maximum(k - 1, 0)])

    @pl.when(jnp.logical_and(f == 0, first))
    def _():
        o_ref[...] = jnp.zeros((bm, d), F32)

    for c in range(1, nsub + 1):
        rows = c * sub

        @pl.when(nchunk == c)
        def _():
            h = h_ref[pl.ds(r0, rows), :]
            wgb[...] = wg_ref[0].astype(BF16)
            g = _dot(h, wgb[...])
            wub[...] = wu_ref[0].astype(BF16)
            u = _dot(h, wub[...])
            wdb[...] = wd_ref[0].astype(BF16)
            a = (g * _sigmoid(g) * u).astype(BF16)
            for n in range(ncol):
                cols = slice(n * (d // ncol), (n + 1) * (d // ncol))
                o_ref[pl.ds(r0, rows), cols] += _dot(a, wdb[:, cols])


def _ffn(h, w_gate, w_up, w_down, seg_tile, seg_expert, seg_row0, seg_rows):
    rows, d = h.shape
    _, _, dff = w_gate.shape
    bm, bf, sub = FFN_BM, FFN_BF, FFN_SUB
    nf = dff // bf
    nseg = seg_tile.shape[0]

    def f_eff(k, f, nr):
        return jnp.where(nr[k] > 0, f, nf - 1)

    grid_spec = pltpu.PrefetchScalarGridSpec(
        num_scalar_prefetch=4,
        grid=(nseg, nf),
        in_specs=[
            pl.BlockSpec((bm, d), lambda k, f, st, se, r0, nr: (st[k], 0), pipeline_mode=pl.Buffered(1)),
            pl.BlockSpec((1, d, bf), lambda k, f, st, se, r0, nr: (se[k], 0, f_eff(k, f, nr))),
            pl.BlockSpec((1, d, bf), lambda k, f, st, se, r0, nr: (se[k], 0, f_eff(k, f, nr))),
            pl.BlockSpec((1, bf, d), lambda k, f, st, se, r0, nr: (se[k], f_eff(k, f, nr), 0)),
        ],
        out_specs=pl.BlockSpec((bm, d), lambda k, f, st, se, r0, nr: (st[k], 0), pipeline_mode=pl.Buffered(1)),
        scratch_shapes=[
            pltpu.VMEM((d, bf), BF16),
            pltpu.VMEM((d, bf), BF16),
            pltpu.VMEM((bf, d), BF16),
        ],
    )
    return pl.pallas_call(
        functools.partial(_ffn_kernel, sub=sub, nsub=bm // sub, ncol=FFN_NCOL),
        grid_spec=grid_spec,
        out_shape=jax.ShapeDtypeStruct((rows, d), F32),
        compiler_params=_params("arbitrary", "arbitrary"),
        name="swiglu_ffn",
    )(seg_tile, seg_expert, seg_row0, seg_rows, h, w_gate, w_up, w_down)


def _sb_kernel(q_ref, k_ref, v_ref, tri_ref, tri2_ref, o_ref, *, bq, nh, scale):
    i = pl.program_id(2)
    dh = HEAD_DIM
    qs = [q_ref[:, h * dh:(h + 1) * dh] for h in range(nh)]

    def tile(k0, kwid, state, tri, shift):
        if shift is not None:
            row = lax.broadcasted_iota(jnp.int32, (bq, kwid), 0)
            col = lax.broadcasted_iota(jnp.int32, (bq, kwid), 1)
            causal = col < row + shift
        out = []
        for h in range(nh):
            carry, acc = state[2 * h], state[2 * h + 1]
            k = k_ref[pl.ds(k0, kwid), h * dh:(h + 1) * dh]
            v = v_ref[pl.ds(k0, kwid), h * dh:(h + 1) * dh]
            z = _dot_nt(qs[h], k) * scale
            log_beta = jnp.minimum(z, 0.0) - jnp.log(1.0 + jnp.exp(-jnp.abs(z)))
            log_keep = log_beta - z
            if shift is not None:
                log_keep = jnp.where(causal, log_keep, 0.0)
            hi, lo = _split_bf16(log_keep)
            between = _dot(jnp.concatenate([hi, lo], axis=1), tri[...]) + carry
            w = jnp.exp(log_beta + between)
            if shift is not None:
                w = jnp.where(causal, w, 0.0)
            out.append(carry + jnp.sum(log_keep, axis=-1, keepdims=True))
            out.append(acc + _dot(w.astype(BF16), v))
        return tuple(out)

    state0 = tuple(jnp.zeros((bq, 1 if n % 2 == 0 else dh), F32) for n in range(2 * nh))
    state = lax.cond(
        i == 0,
        lambda: tile(0, bq, state0, tri_ref, 0),
        lambda: tile(pl.multiple_of((i - 1) * bq, bq), 2 * bq, state0, tri2_ref, bq))
    n_left = jnp.maximum(i - 1, 0)

    def top_carry(st):
        m = jnp.max(st[0])
        for h in range(1, nh):
            m = jnp.maximum(m, jnp.max(st[2 * h]))
        return m

    def cond(c):
        return jnp.logical_and(c[0] < n_left, c[1] > SB_EXP_ZERO)

    def body(c):
        st = tile(pl.multiple_of((n_left - 1 - c[0]) * bq, bq), bq, c[2], tri_ref, None)
        return c[0] + 1, top_carry(st), st

    _, _, state = lax.while_loop(cond, body, (jnp.int32(0), top_carry(state), state))
    o_ref[...] = jnp.concatenate([state[2 * h + 1] for h in range(nh)], axis=1).astype(o_ref.dtype)


def _sb_attention(qkv, batch, seq, heads):
    t = qkv.shape[0]
    bq = min(SB_BQ, seq)
    nq = seq // bq
    nh = SB_HEADS_PER_STEP
    hw = nh * HEAD_DIM
    ng = heads // nh
    def suffix_op(width):
        rj = lax.broadcasted_iota(jnp.int32, (2 * width, width), 0)
        cs = lax.broadcasted_iota(jnp.int32, (2 * width, width), 1)
        return jnp.where((rj % width) > cs, 1.0, 0.0).astype(BF16)

    return pl.pallas_call(
        functools.partial(_sb_kernel, bq=bq, nh=nh, scale=HEAD_DIM ** -0.5),
        grid=(batch, ng, nq),
        in_specs=[
            pl.BlockSpec((bq, hw), lambda b, h, i: (b * nq + i, h)),
            pl.BlockSpec((seq, hw), lambda b, h, i: (b, ng + h)),
            pl.BlockSpec((seq, hw), lambda b, h, i: (b, 2 * ng + h)),
            pl.BlockSpec((2 * bq, bq), lambda b, h, i: (0, 0)),
            pl.BlockSpec((4 * bq, 2 * bq), lambda b, h, i: (0, 0)),
        ],
        out_specs=pl.BlockSpec((bq, hw), lambda b, h, i: (b * nq + i, h)),
        out_shape=jax.ShapeDtypeStruct((t, heads * HEAD_DIM), BF16),
        compiler_params=_params("arbitrary", "arbitrary", "arbitrary"),
        name="stickbreak_attn",
    )(qkv, qkv, qkv, suffix_op(bq), suffix_op(2 * bq))


def _rglru_kernel(x_ref, g_ref, cw_ref, cb_ref, wa_ref, ba_ref, wx_ref, bx_ref, lam_ref, o_ref, xp_ref,
                  *, seq, cb, tc, kw):
    pad = SUBLANE
    xp_ref[0:pad, :] = jnp.zeros((pad, cb), F32)
    xp_ref[pad:, :] = x_ref[...]
    log_lam = _log_sigmoid(lam_ref[...])
    sub_iota = lax.broadcasted_iota(jnp.int32, (SUBLANE, cb), 0)
    ngroup = cb // LANE

    def chunk(ci, h):
        t0 = pl.multiple_of(ci * tc, tc)
        win = xp_ref[pl.ds(t0, tc + pad), :]
        xc = cb_ref[...] + cw_ref[0:1, :] * win[pad - kw + 1:pad - kw + 1 + tc, :]
        for k in range(1, kw):
            off = pad - kw + 1 + k
            xc = xc + cw_ref[k:k + 1, :] * win[off:off + tc, :]
        xcb = xc.astype(BF16)
        ra = jnp.concatenate(
            [_dot(xcb[:, q * LANE:(q + 1) * LANE], wa_ref[q].astype(BF16)) for q in range(ngroup)], axis=1)
        rx = jnp.concatenate(
            [_dot(xcb[:, q * LANE:(q + 1) * LANE], wx_ref[q].astype(BF16)) for q in range(ngroup)], axis=1)
        r = _sigmoid(ra + ba_ref[...])
        gi = _sigmoid(rx + bx_ref[...])
        log_a = RG_C * r * log_lam
        a = jnp.exp(log_a)
        u = jnp.sqrt(-jnp.tanh(log_a) * (a * a + 1.0)) * (gi * xc)
        gate = _gelu_tanh(g_ref[pl.ds(t0, tc), :])
        outs = []
        for gidx in range(tc // SUBLANE):
            av = a[gidx * SUBLANE:(gidx + 1) * SUBLANE, :]
            bv = u[gidx * SUBLANE:(gidx + 1) * SUBLANE, :]
            for sh in (1, 2, 4):
                a_s = pltpu.roll(av, sh, axis=0)
                b_s = pltpu.roll(bv, sh, axis=0)
                m = sub_iota >= sh
                bv = jnp.where(m, av * b_s + bv, bv)
                av = jnp.where(m, av * a_s, av)
            hv = av * h + bv
            outs.append(hv)
            h = jnp.broadcast_to(hv[SUBLANE - 1:SUBLANE, :], (SUBLANE, cb))
        hs = jnp.concatenate(outs, axis=0)
        o_ref[pl.ds(t0, tc), :] = (hs * gate).astype(o_ref.dtype)
        return h

    lax.fori_loop(0, seq // tc, chunk, jnp.zeros((SUBLANE, cb), F32))


def _block_diag_pairs(w):
    nblk, c, _ = w.shape
    w2 = w.reshape(nblk // 2, 2, c, c)
    z = jnp.zeros((nblk // 2, c, c), w.dtype)
    top = jnp.concatenate([w2[:, 0], z], axis=2)
    bot = jnp.concatenate([z, w2[:, 1]], axis=2)
    return jnp.concatenate([top, bot], axis=1)


def _rglru(rg, batch, seq, conv_w, conv_b, wa, ba, wx, bx, lam):
    t, c2 = rg.shape
    c = c2 // 2
    cb = min(RG_CB, c)
    tc = min(RG_TC, seq)
    ncb = c // cb
    kw = conv_w.shape[0]
    gpb = cb // LANE
    vec = pl.BlockSpec((1, cb), lambda b, j: (0, j))
    return pl.pallas_call(
        functools.partial(_rglru_kernel, seq=seq, cb=cb, tc=tc, kw=kw),
        grid=(batch, ncb),
        in_specs=[
            pl.BlockSpec((seq, cb), lambda b, j: (b, j)),
            pl.BlockSpec((seq, cb), lambda b, j: (b, ncb + j)),
            pl.BlockSpec((kw, cb), lambda b, j: (0, j)),
            vec,
            pl.BlockSpec((gpb, LANE, LANE), lambda b, j: (j, 0, 0)),
            vec,
            pl.BlockSpec((gpb, LANE, LANE), lambda b, j: (j, 0, 0)),
            vec,
            vec,
        ],
        out_specs=pl.BlockSpec((seq, cb), lambda b, j: (b, j)),
        out_shape=jax.ShapeDtypeStruct((t, c), BF16),
        scratch_shapes=[pltpu.VMEM((seq + SUBLANE, cb), F32)],
        compiler_params=_params("arbitrary", "arbitrary"),
        name="rglru",
    )(rg, rg, conv_w, conv_b.reshape(1, c), _block_diag_pairs(wa), ba.reshape(1, c),
      _block_diag_pairs(wx), bx.reshape(1, c), lam.reshape(1, c))


def _compress_kernel(x_ref, pos_ref, w1_ref, w2_ref, o_ref, *, ngrp):
    st = CMP_STRIDE
    half = st * HEAD_DIM
    xs = [x_ref[pl.ds(r, ngrp, stride=st), :] for r in range(st)]
    pos = pos_ref[0]
    x0 = jnp.concatenate([xs[r] + pos[r:r + 1, :] for r in range(st)], axis=1).astype(BF16)
    x1 = jnp.concatenate([xs[r] + pos[st + r:st + r + 1, :] for r in range(st)], axis=1).astype(BF16)
    p0 = _dot(x0, w1_ref[0, 0:half, :].astype(BF16))
    p1 = _dot(x1, w1_ref[0, half:2 * half, :].astype(BF16))
    pre = p0 + pltpu.roll(p1, ngrp - 1, axis=0)
    out = _dot(_gelu_tanh(pre).astype(BF16), w2_ref[0].astype(BF16))
    rown = lax.broadcasted_iota(jnp.int32, out.shape, 0)
    o_ref[...] = jnp.where(rown < ngrp - 1, out, 0.0).astype(o_ref.dtype)


def _compress(kcvc, batch, seq, pos, w1, w2):
    assert CMP_BLOCK == 2 * CMP_STRIDE
    g2 = kcvc.shape[1] // HEAD_DIM
    per = g2 // 2
    ngrp = seq // CMP_STRIDE
    return pl.pallas_call(
        functools.partial(_compress_kernel, ngrp=ngrp),
        grid=(batch, g2),
        in_specs=[
            pl.BlockSpec((seq, HEAD_DIM), lambda b, j: (b, j)),
            pl.BlockSpec((1, CMP_BLOCK, HEAD_DIM), lambda b, j: (j // per, 0, 0)),
            pl.BlockSpec((1, CMP_BLOCK * HEAD_DIM, HEAD_DIM), lambda b, j: (j // per, 0, 0)),
            pl.BlockSpec((1, HEAD_DIM, HEAD_DIM), lambda b, j: (j // per, 0, 0)),
        ],
        out_specs=pl.BlockSpec((ngrp, HEAD_DIM), lambda b, j: (b * g2 + j, 0)),
        out_shape=jax.ShapeDtypeStruct((batch * g2 * ngrp, HEAD_DIM), BF16),
        compiler_params=_params("arbitrary", "arbitrary"),
        name="nsa_compress",
    )(kcvc, pos, w1, w2)


def _softmax_parts(s2, bias):
    sb = s2 + bias
    m = jnp.max(sb, axis=-1, keepdims=True)
    e = jnp.exp2(sb - m)
    return e, jnp.sum(e, axis=-1, keepdims=True)


def _nsa_kernel(q_ref, kc_ref, vc_ref, ks_ref, vs_ref, kw_ref, vw_ref, g_ref, ov_ref, ex_ref, o_ref,
                acc_ref, inv_ref, *, seq, hg, n_slc, scale):
    bq, bk, dh = NSA_BQ, NSA_BK, HEAD_DIM
    i = pl.program_id(2)
    q0 = i * bq
    qb = q_ref[...]
    qs = jnp.concatenate([qb[:, h * dh:(h + 1) * dh] for h in range(hg)], axis=0)
    qpos1 = q0 + lax.broadcasted_iota(jnp.int32, (bq, 1), 0)
    qpos = jnp.concatenate([qpos1] * hg, axis=0)

    scale2 = scale * LOG2E
    tile_heads = lambda a: jnp.concatenate([a] * hg, axis=0)

    ncmp = kc_ref.shape[0]
    s_c = _dot_nt(qs, kc_ref[...]) * scale2
    n_idx = lax.broadcasted_iota(jnp.int32, (1, ncmp), 1)
    bias_c = jnp.where(n_idx * CMP_STRIDE + (CMP_BLOCK - 1) <= qpos1, 0.0, NEG_BIG)
    e_c, l_c = _softmax_parts(s_c, tile_heads(bias_c))
    p_c = e_c * jnp.where(qpos >= CMP_BLOCK - 1, 1.0 / l_c, 0.0)
    o_c = _dot(p_c.astype(BF16), vc_ref[...])

    p_sum = p_c[0:bq, :]
    for h in range(1, hg):
        p_sum = p_sum + p_c[h * bq:(h + 1) * bq, :]
    overlap_t = ov_ref[...]
    p_hi, p_lo = _split_bf16(p_sum)
    imp_t = (_dot_nt(overlap_t, p_hi) + _dot_nt(overlap_t, p_lo))[0:n_slc, :]

    blk = lax.broadcasted_iota(jnp.int32, (n_slc, bq), 0)
    qp_l = q0 + lax.broadcasted_iota(jnp.int32, (n_slc, bq), 1)
    cur = qp_l // SLC_BLOCK
    forced = (blk == 0) | (blk == cur) | (blk == cur - 1)
    valid = blk * SLC_BLOCK <= qp_l
    rank = jnp.where(valid, imp_t + FORCE_BONUS * jnp.where(forced, 1.0, 0.0), -jnp.inf)
    ahead = jnp.zeros((n_slc, bq), F32)
    for jp in range(n_slc):
        other = rank[jp:jp + 1, :]
        beats = (other > rank) | ((other == rank) & (blk > jp))
        ahead = ahead + jnp.where(beats, 1.0, 0.0)
    k_sel = min(N_SEL, n_slc)
    sel_t = jnp.where(valid & (ahead < k_sel), 1.0, 0.0)
    sel_t = jnp.concatenate([sel_t, jnp.zeros((LANE - n_slc, bq), F32)], axis=0)
    sel = sel_t.T.astype(BF16)

    n_need = (q0 + bq + bk - 1) // bk
    for v in range(1, seq // bk + 1):
        nk = v * bk

        @pl.when(n_need == v)
        def _():
            s = _dot_nt(qs, ks_ref[0:nk, :]) * scale2
            picked = _dot(sel, ex_ref[:, 0:nk])
            kpos = lax.broadcasted_iota(jnp.int32, (1, nk), 1)
            bias = jnp.where((picked > 0.5) & (kpos <= qpos1), 0.0, NEG_BIG)
            e_s, l_s = _softmax_parts(s, tile_heads(bias))
            acc_ref[...] = _dot(e_s.astype(BF16), vs_ref[0:nk, :])
            inv_ref[...] = 1.0 / l_s

    acc_s = acc_ref[...]
    inv_s = inv_ref[...]

    span = min(WINDOW + bq, seq)
    w0 = pl.multiple_of(jnp.maximum(jnp.minimum(q0 - WINDOW, seq - span), 0), bq)
    kwt = kw_ref[pl.ds(w0, span), :]
    vwt = vw_ref[pl.ds(w0, span), :]
    s_w = _dot_nt(qs, kwt) * scale2
    kpos_w = w0 + lax.broadcasted_iota(jnp.int32, (1, span), 1)
    bias_w = jnp.where((kpos_w <= qpos1) & (kpos_w > qpos1 - WINDOW), 0.0, NEG_BIG)
    e_w, l_w = _softmax_parts(s_w, tile_heads(bias_w))
    acc_w = _dot(e_w.astype(BF16), vwt)
    inv_w = 1.0 / l_w

    gates = _sigmoid(g_ref[...])
    outs = []
    for h in range(hg):
        r = slice(h * bq, (h + 1) * bq)
        c = h * N_BRANCH
        outs.append(gates[:, c:c + 1] * o_c[r, :] + (gates[:, c + 1:c + 2] * inv_s[r, :]) * acc_s[r, :]
                    + (gates[:, c + 2:c + 3] * inv_w[r, :]) * acc_w[r, :])
    o_ref[...] = jnp.concatenate(outs, axis=1).astype(o_ref.dtype)


def _nsa_attention(q, kv4, cmp, gate_logits, batch, seq):
    t, width = q.shape
    g = NSA_KV_HEADS
    hg = width // (g * HEAD_DIM)
    bq = NSA_BQ
    nb = seq // bq
    ncmp = seq // CMP_STRIDE
    n_slc = seq // SLC_BLOCK
    dh = HEAD_DIM
    full = lambda off: pl.BlockSpec((seq, dh), lambda b, gi, i: (b, off * g + gi))
    jj = lax.broadcasted_iota(jnp.int32, (LANE, ncmp), 0)
    cstart = lax.broadcasted_iota(jnp.int32, (LANE, ncmp), 1) * CMP_STRIDE
    overlap_t = jnp.where((cstart < (jj + 1) * SLC_BLOCK) & (cstart + CMP_BLOCK > jj * SLC_BLOCK)
                          & (jj < n_slc), 1.0, 0.0).astype(BF16)
    expand = jnp.where(lax.broadcasted_iota(jnp.int32, (LANE, seq), 1) // SLC_BLOCK
                       == lax.broadcasted_iota(jnp.int32, (LANE, seq), 0), 1.0, 0.0).astype(BF16)
    const = lambda shape: pl.BlockSpec(shape, lambda b, gi, i: (0, 0))
    return pl.pallas_call(
        functools.partial(_nsa_kernel, seq=seq, hg=hg, n_slc=n_slc, scale=dh ** -0.5),
        grid=(batch, g, nb),
        in_specs=[
            pl.BlockSpec((bq, hg * dh), lambda b, gi, i: (b * nb + i, gi)),
            pl.BlockSpec((ncmp, dh), lambda b, gi, i: (b * 2 * g + gi, 0)),
            pl.BlockSpec((ncmp, dh), lambda b, gi, i: (b * 2 * g + g + gi, 0)),
            full(0), full(1), full(2), full(3),
            pl.BlockSpec((bq, LANE), lambda b, gi, i: (b * nb + i, gi)),
            const((LANE, ncmp)),
            const((LANE, seq)),
        ],
        out_specs=pl.BlockSpec((bq, hg * dh), lambda b, gi, i: (b * nb + i, gi)),
        out_shape=jax.ShapeDtypeStruct((t, width), BF16),
        scratch_shapes=[pltpu.VMEM((hg * bq, dh), F32), pltpu.VMEM((hg * bq, 1), F32)],
        compiler_params=_params("arbitrary", "arbitrary", "arbitrary"),
        name="nsa_attn",
    )(q, cmp, cmp, kv4, kv4, kv4, kv4, gate_logits, overlap_t, expand)


def _gather_kernel(lo_ref, hi_ref, src_ref, h_ref, o_ref, *, chunk):
    i = pl.program_id(0)
    rows = o_ref.shape[0]
    src = src_ref[...]
    lane = lax.broadcasted_iota(jnp.int32, (rows, chunk), 1)
    o_ref[...] = jnp.zeros(o_ref.shape, o_ref.dtype)

    def body(c, carry):
        c0 = pl.multiple_of(c * chunk, chunk)
        onehot = jnp.where(src - c0 == lane, 1.0, 0.0).astype(BF16)
        o_ref[...] += _dot(onehot, h_ref[pl.ds(c0, chunk), :]).astype(o_ref.dtype)
        return carry

    lax.fori_loop(lo_ref[i], hi_ref[i], body, 0)


def _gather_rows(h, src, n_rows):
    t, d = h.shape
    bt, chunk = GATHER_ROWS, GATHER_CHUNK
    nt = n_rows // bt
    src2 = src.reshape(nt, bt)
    live = src2 >= 0
    lo = jnp.min(jnp.where(live, src2, t), axis=1) // chunk
    hi = jnp.where(jnp.any(live, axis=1), jnp.max(src2, axis=1) // chunk + 1, lo)
    grid_spec = pltpu.PrefetchScalarGridSpec(
        num_scalar_prefetch=2,
        grid=(nt,),
        in_specs=[
            pl.BlockSpec((bt, 1), lambda i, lo, hi: (i, 0)),
            pl.BlockSpec((t, d), lambda i, lo, hi: (0, 0), pipeline_mode=pl.Buffered(1)),
        ],
        out_specs=pl.BlockSpec((bt, d), lambda i, lo, hi: (i, 0)),
    )
    return pl.pallas_call(
        functools.partial(_gather_kernel, chunk=chunk),
        grid_spec=grid_spec,
        out_shape=jax.ShapeDtypeStruct((n_rows, d), h.dtype),
        compiler_params=_params("arbitrary"),
        name="moe_dispatch",
    )(jnp.minimum(lo, hi).astype(jnp.int32), hi.astype(jnp.int32), src.reshape(n_rows, 1), h)


def _router_kernel(l_ref, o_ref, *, n_exp):
    x = l_ref[...]
    lane = lax.broadcasted_iota(jnp.int32, x.shape, 1)
    xm = jnp.where(lane < n_exp, x, -jnp.inf)
    v0 = jnp.max(xm, axis=-1, keepdims=True)
    i0 = jnp.min(jnp.where(xm == v0, lane, LANE), axis=-1, keepdims=True)
    xm = jnp.where(lane == i0, -jnp.inf, xm)
    v1 = jnp.max(xm, axis=-1, keepdims=True)
    i1 = jnp.min(jnp.where(xm == v1, lane, LANE), axis=-1, keepdims=True)
    e = jnp.exp(v1 - v0)
    p0 = 1.0 / (1.0 + e)
    out = jnp.where(lane == 0, p0, jnp.where(lane == 1, e * p0, jnp.where(
        lane == 2, i0.astype(F32), jnp.where(lane == 3, i1.astype(F32), 0.0))))
    o_ref[...] = out


def _router_top2(logits, n_exp):
    t = logits.shape[0]
    bt = ROW_TILE
    out = pl.pallas_call(
        functools.partial(_router_kernel, n_exp=n_exp),
        grid=(t // bt,),
        in_specs=[pl.BlockSpec((bt, LANE), lambda i: (i, 0))],
        out_specs=pl.BlockSpec((bt, LANE), lambda i: (i, 0)),
        out_shape=jax.ShapeDtypeStruct((t, LANE), F32),
        compiler_params=_params("arbitrary"),
        name="router_top2",
    )(logits)
    return out[:, 0:TOP_K], out[:, TOP_K:2 * TOP_K].astype(jnp.int32)


def _route(probs, top_i, n_experts):
    t = top_i.shape[0]
    flat_e = top_i.reshape(-1)
    onehot = (flat_e[:, None] == jnp.arange(n_experts)[None, :]).astype(jnp.int32)
    rank = jnp.take_along_axis(jnp.cumsum(onehot, axis=0), flat_e[:, None], axis=1)[:, 0] - 1
    counts = jnp.sum(onehot, axis=0)
    padded = (counts + FFN_SUB - 1) // FFN_SUB * FFN_SUB
    end = jnp.cumsum(padded)
    start = end - padded
    n_rows = -(-(TOP_K * t + n_experts * (FFN_SUB - 1)) // FFN_BM) * FFN_BM
    n_tiles = n_rows // FFN_BM
    dest = (start[flat_e] + rank).astype(jnp.int32)
    src = jnp.full((n_rows,), -1, jnp.int32).at[dest].set(jnp.arange(TOP_K * t, dtype=jnp.int32) // TOP_K)
    cuts = jnp.sort(jnp.concatenate([start, jnp.arange(n_tiles) * FFN_BM]))
    nxt = jnp.concatenate([cuts[1:], jnp.full((1,), n_rows, cuts.dtype)])
    seg_rows = jnp.maximum(jnp.minimum(nxt, end[-1]) - cuts, 0)
    seg_tile = jnp.minimum(cuts // FFN_BM, n_tiles - 1)
    seg_expert = jnp.minimum(jnp.searchsorted(end, cuts, side="right"), n_experts - 1)
    i32 = lambda a: a.astype(jnp.int32)
    return probs, dest.reshape(t, TOP_K), src, i32(seg_tile), i32(seg_expert), i32(cuts % FFN_BM), i32(seg_rows)


def kernel(x, c, ada_mix_w, ada_mix_b, norm_mix_g, ada_ffn_w, ada_ffn_b, norm_ffn_g, even_in_w, rg_conv_w, rg_conv_b, rg_wa, rg_ba, rg_wx, rg_bx, rg_lambda, even_out_w, dense_w_gate, dense_w_up, dense_w_down, nsa_in_w, cmp_pos_k, cmp_pos_v, cmp_k_w1, cmp_k_w2, cmp_v_w1, cmp_v_w2, nsa_out_w, router_w, moe_w_gate, moe_w_up, moe_w_down, final_norm_g):
    batch, seq, d = x.shape
    t = batch * seq
    depth = ada_mix_w.shape[0]
    xf = x.reshape(t, d)

    c_pad = jnp.pad(c, ((0, (-batch) % SUBLANE), (0, 0)))
    m_mix = _adaln(c_pad, ada_mix_w, ada_mix_b)
    m_ffn = _adaln(c_pad, ada_ffn_w, ada_ffn_b)

    def mods(m, layer):
        v = m[layer, :batch].reshape(batch, 1, 3, d)
        return v[:, :, 0], v[:, :, 1], v[:, :, 2]

    def norm_step(xf, pending, g, mod, out_dtype=BF16):
        if pending is not None and pending[0] == "experts":
            _, y, dest, probs, gate = pending
            return _combine_norm(xf, y, dest, probs, gate, g, seq, mod=mod, out_dtype=out_dtype)
        res = None if pending is None else pending[1:]
        return _normmod(xf, g, seq, res=res, mod=mod, out_dtype=out_dtype)

    pending = None
    for layer in range(depth):
        j = layer // 2
        shift, scale, gate = mods(m_mix, layer)
        xf, h = norm_step(xf, pending, norm_mix_g[layer], (shift, scale))
        if layer % 2 == 0:
            w_in = even_in_w[j]
            sbw = (w_in.shape[1] - 2 * rg_conv_w.shape[2]) // 3
            heads = sbw // HEAD_DIM
            qkv = _matmul([h], w_in, 0, 3 * sbw, BF16)
            rg = _matmul([h], w_in, 3 * sbw, w_in.shape[1] - 3 * sbw, F32)
            o_a = _sb_attention(qkv, batch, seq, heads)
            o_b = _rglru(rg, batch, seq, rg_conv_w[j], rg_conv_b[j], rg_wa[j], rg_ba[j], rg_wx[j],
                         rg_bx[j], rg_lambda[j])
            xf = _matmul([o_a, o_b], even_out_w[j], 0, d, F32, residual=(xf, gate, seq))
        else:
            w_in = nsa_in_w[j]
            g = NSA_KV_HEADS
            kvw = g * HEAD_DIM
            nsa_w = nsa_out_w.shape[1]
            hg = nsa_w // kvw
            q = _matmul([h], w_in, 0, nsa_w, BF16)
            kcvc = _matmul([h], w_in, nsa_w, 2 * kvw, F32)
            kv4 = _matmul([h], w_in, nsa_w + 2 * kvw, 4 * kvw, BF16)
            wg = w_in[:, nsa_w + 6 * kvw:].reshape(d, g, hg * N_BRANCH)
            wg = jnp.pad(wg, ((0, 0), (0, 0), (0, LANE - hg * N_BRANCH))).reshape(d, g * LANE)
            gl = _matmul([h], wg, 0, g * LANE, F32)
            cmp = _compress(kcvc, batch, seq, jnp.stack([cmp_pos_k[j], cmp_pos_v[j]]),
                            jnp.stack([cmp_k_w1[j], cmp_v_w1[j]]), jnp.stack([cmp_k_w2[j], cmp_v_w2[j]]))
            o = _nsa_attention(q, kv4, cmp, gl, batch, seq)
            xf = _matmul([o], nsa_out_w[j], 0, d, F32, residual=(xf, gate, seq))
        pending = None

        shift, scale, gate = mods(m_ffn, layer)
        xf, h = norm_step(xf, pending, norm_ffn_g[layer], (shift, scale))
        if layer % 2 == 0:
            nt = t // FFN_BM
            ffn = _ffn(h, dense_w_gate[j:j + 1], dense_w_up[j:j + 1], dense_w_down[j:j + 1],
                       jnp.arange(nt, dtype=jnp.int32), jnp.zeros((nt,), jnp.int32),
                       jnp.zeros((nt,), jnp.int32), jnp.full((nt,), FFN_BM, jnp.int32))
            pending = ("dense", ffn, gate)
        else:
            n_exp = router_w.shape[2]
            rw = jnp.pad(router_w[j], ((0, 0), (0, LANE - n_exp)))
            probs, top_i = _router_top2(_matmul([h], rw, 0, LANE, F32), n_exp)
            probs, dest, src, seg_tile, seg_expert, seg_row0, seg_rows = _route(probs, top_i, n_exp)
            h_sorted = _gather_rows(h, src, src.shape[0])
            y = _ffn(h_sorted, moe_w_gate[j], moe_w_up[j], moe_w_down[j], seg_tile, seg_expert, seg_row0,
                     seg_rows)
            pending = ("experts", y, dest, probs, gate)

    _, out = norm_step(xf, pending, final_norm_g, None, F32)
    return out.reshape(batch, seq, d)
```

```python
import functools

import jax
import jax.numpy as jnp
from jax import lax
from jax.experimental import pallas as pl
from jax.experimental.pallas import tpu as pltpu

F32 = jnp.float32
BF16 = jnp.bfloat16

LANE = 128
SUBLANE = 8
VMEM_LIMIT_BYTES = 56 * 1024 * 1024

HEAD_DIM = 128
EPS = 1e-6
RG_C = 8.0
NSA_KV_HEADS = 4
N_BRANCH = 3
CMP_BLOCK = 32
CMP_STRIDE = 16
SLC_BLOCK = 64
N_SEL = 8
WINDOW = 512
FORCE_BONUS = 1e6
TOP_K = 2
NEG_BIG = -1e30
LOG2E = 1.4426950408889634
SB_EXP_ZERO = -110.0

ROW_TILE = 512
COMBINE_ROWS = 256
MM_BM = 2048
MM_BN = 512
FFN_BM = 2048
FFN_SUB = 256
FFN_NCOL = 4
FFN_BF = 256
GATHER_ROWS = 256
GATHER_CHUNK = 512
SB_BQ = 256
SB_HEADS_PER_STEP = 4
NSA_BQ = 128
NSA_BK = 256
RG_CB = 256
RG_TC = 256


def _params(*sem):
    return pltpu.CompilerParams(dimension_semantics=sem, vmem_limit_bytes=VMEM_LIMIT_BYTES)


def _dot(a, b):
    return jnp.dot(a, b, preferred_element_type=F32)


def _dot_nt(a, b):
    return lax.dot_general(a, b, (((1,), (1,)), ((), ())), preferred_element_type=F32)


def _sigmoid(x):
    return 1.0 / (1.0 + jnp.exp(-x))


def _log_sigmoid(x):
    return jnp.minimum(x, 0.0) - jnp.log1p(jnp.exp(-jnp.abs(x)))


def _gelu_tanh(x):
    return 0.5 * x * (1.0 + jnp.tanh(0.7978845608028654 * (x + 0.044715 * (x * x * x))))


def _split_bf16(x):
    hi = x.astype(BF16)
    lo = (x - hi.astype(F32)).astype(BF16)
    return hi, lo


def _adaln_kernel(c_ref, w_ref, b_ref, o_ref):
    c = c_ref[...]
    s = (c * _sigmoid(c)).astype(BF16)
    o_ref[0] = _dot(s, w_ref[0].astype(BF16)) + b_ref[0]


def _adaln(c_pad, w, b, bn=1024):
    depth, d, n3 = w.shape
    rows = c_pad.shape[0]
    return pl.pallas_call(
        _adaln_kernel,
        grid=(depth, n3 // bn),
        in_specs=[
            pl.BlockSpec((rows, d), lambda l, j: (0, 0)),
            pl.BlockSpec((1, d, bn), lambda l, j: (l, 0, j)),
            pl.BlockSpec((1, 1, bn), lambda l, j: (l, 0, j)),
        ],
        out_specs=pl.BlockSpec((1, rows, bn), lambda l, j: (l, 0, j)),
        out_shape=jax.ShapeDtypeStruct((depth, rows, n3), F32),
        compiler_params=_params("arbitrary", "arbitrary"),
        name="adaln",
    )(c_pad, w, b.reshape(depth, 1, n3))


def _normmod_kernel(*refs, has_res, has_mod):
    it = iter(refs)
    x_ref = next(it)
    if has_res:
        y_ref, gate_ref = next(it), next(it)
    g_ref = next(it)
    if has_mod:
        shift_ref, scale_ref = next(it), next(it)
    if has_res:
        xo_ref = next(it)
    h_ref = next(it)

    x = x_ref[...]
    if has_res:
        x = x + gate_ref[0] * y_ref[...]
        xo_ref[...] = x
    ms = jnp.mean(x * x, axis=-1, keepdims=True)
    h = x * lax.rsqrt(ms + EPS) * g_ref[...]
    if has_mod:
        h = h * (1.0 + scale_ref[0]) + shift_ref[0]
    h_ref[...] = h.astype(h_ref.dtype)


def _normmod(x, g, seq, res=None, mod=None, out_dtype=BF16):
    t, d = x.shape
    bt = ROW_TILE
    per_b = seq // bt
    row = pl.BlockSpec((bt, d), lambda i: (i, 0))
    vec = pl.BlockSpec((1, 1, d), lambda i: (i // per_b, 0, 0))
    args, specs = [x], [row]
    if res is not None:
        args += [res[0], res[1]]
        specs += [row, vec]
    args.append(g.reshape(1, d))
    specs.append(pl.BlockSpec((1, d), lambda i: (0, 0)))
    if mod is not None:
        args += [mod[0], mod[1]]
        specs += [vec, vec]
    out_shape, out_specs = [], []
    if res is not None:
        out_shape.append(jax.ShapeDtypeStruct((t, d), F32))
        out_specs.append(row)
    out_shape.append(jax.ShapeDtypeStruct((t, d), out_dtype))
    out_specs.append(row)
    outs = pl.pallas_call(
        functools.partial(_normmod_kernel, has_res=res is not None, has_mod=mod is not None),
        grid=(t // bt,),
        in_specs=specs,
        out_specs=out_specs,
        out_shape=out_shape,
        compiler_params=_params("arbitrary"),
        name="normmod",
    )(*args)
    return outs if res is not None else (x, outs[0])


def _combine_norm_kernel(dest_ref, *refs, has_mod, bt, n_steps):
    it = iter(refs)
    x_ref, y_hbm, p_ref, gate_ref, g_ref = (next(it) for _ in range(5))
    if has_mod:
        shift_ref, scale_ref, xo_ref = next(it), next(it), next(it)
    h_ref, buf, sem = next(it), next(it), next(it)
    i = pl.program_id(0)

    def row_copy(src_row, slot, k, r):
        return pltpu.make_async_copy(y_hbm.at[pl.ds(src_row, 1), :], buf.at[slot, k, pl.ds(r, 1), :],
                                     sem.at[slot])

    def start_tile(tile, slot):
        base = tile * (bt * TOP_K)

        def body(r, carry):
            for k in range(TOP_K):
                row_copy(dest_ref[base + r * TOP_K + k], slot, k, r).start()
            return carry

        lax.fori_loop(0, bt, body, 0, unroll=8)

    @pl.when(i == 0)
    def _():
        start_tile(0, 0)

    @pl.when(i + 1 < n_steps)
    def _():
        start_tile(i + 1, (i + 1) % 2)

    slot = i % 2

    def wait_row(r, carry):
        for k in range(TOP_K):
            row_copy(0, slot, k, r).wait()
        return carry

    lax.fori_loop(0, bt, wait_row, 0, unroll=8)
    p = p_ref[...]
    y = p[:, 0:1] * buf[slot, 0]
    for k in range(1, TOP_K):
        y = y + p[:, k:k + 1] * buf[slot, k]
    x = x_ref[...] + gate_ref[0] * y
    ms = jnp.mean(x * x, axis=-1, keepdims=True)
    h = x * lax.rsqrt(ms + EPS) * g_ref[...]
    if has_mod:
        xo_ref[...] = x
        h = h * (1.0 + scale_ref[0]) + shift_ref[0]
    h_ref[...] = h.astype(h_ref.dtype)


def _combine_norm(x, y, dest, probs, gate, g, seq, mod=None, out_dtype=BF16):
    t, d = x.shape
    bt = COMBINE_ROWS
    n_steps = t // bt
    per_b = seq // bt
    row = pl.BlockSpec((bt, d), lambda i, dst: (i, 0))
    vec = pl.BlockSpec((1, 1, d), lambda i, dst: (i // per_b, 0, 0))
    args = [x, y, probs, gate, g.reshape(1, d)]
    specs = [row, pl.BlockSpec(memory_space=pl.ANY), pl.BlockSpec((bt, TOP_K), lambda i, dst: (i, 0)), vec,
             pl.BlockSpec((1, d), lambda i, dst: (0, 0))]
    out_shape, out_specs = [], []
    if mod is not None:
        args += [mod[0], mod[1]]
        specs += [vec, vec]
        out_shape.append(jax.ShapeDtypeStruct((t, d), F32))
        out_specs.append(row)
    out_shape.append(jax.ShapeDtypeStruct((t, d), out_dtype))
    out_specs.append(row)
    grid_spec = pltpu.PrefetchScalarGridSpec(
        num_scalar_prefetch=1,
        grid=(n_steps,),
        in_specs=specs,
        out_specs=out_specs,
        scratch_shapes=[pltpu.VMEM((2, TOP_K, bt, d), F32), pltpu.SemaphoreType.DMA((2,))],
    )
    outs = pl.pallas_call(
        functools.partial(_combine_norm_kernel, has_mod=mod is not None, bt=bt, n_steps=n_steps),
        grid_spec=grid_spec,
        out_shape=out_shape,
        compiler_params=_params("arbitrary"),
        name="moe_combine_norm",
    )(dest.reshape(-1), *args)
    return (outs[0], outs[1]) if mod is not None else (None, outs[0])


def _mm_kernel(*refs, ks, has_res):
    a_refs = refs[: len(ks)]
    rest = refs[len(ks):]
    w_ref = rest[0]
    if has_res:
        x_ref, gate_ref = rest[1], rest[2]
    o_ref, wb_ref = rest[-2], rest[-1]

    wb_ref[...] = w_ref[...].astype(BF16)

    acc = None
    k0 = 0
    for a_ref, k in zip(a_refs, ks):
        part = _dot(a_ref[...], wb_ref[k0:k0 + k, :])
        acc = part if acc is None else acc + part
        k0 += k
    if has_res:
        acc = x_ref[...] + gate_ref[0] * acc
    o_ref[...] = acc.astype(o_ref.dtype)


def _matmul(a_list, w, col0, n, out_dtype, residual=None):
    m = a_list[0].shape[0]
    ks = tuple(a.shape[1] for a in a_list)
    ktot = sum(ks)
    bm = min(MM_BM, m)
    bn = min(MM_BN, n)
    cb0 = col0 // bn
    args = list(a_list) + [w]
    specs = [pl.BlockSpec((bm, k), lambda j, i: (i, 0)) for k in ks]
    specs.append(pl.BlockSpec((ktot, bn), lambda j, i: (0, cb0 + j)))
    if residual is not None:
        x, gate, seq = residual
        bm = min(bm, seq)
        per_b = seq // bm
        specs[: len(ks)] = [pl.BlockSpec((bm, k), lambda j, i: (i, 0)) for k in ks]
        args += [x, gate]
        specs += [pl.BlockSpec((bm, bn), lambda j, i: (i, j)),
                  pl.BlockSpec((1, 1, bn), lambda j, i: (i // per_b, 0, j))]
    return pl.pallas_call(
        functools.partial(_mm_kernel, ks=ks, has_res=residual is not None),
        grid=(n // bn, m // bm),
        in_specs=specs,
        out_specs=pl.BlockSpec((bm, bn), lambda j, i: (i, j)),
        out_shape=jax.ShapeDtypeStruct((m, n), out_dtype),
        scratch_shapes=[pltpu.VMEM((ktot, bn), BF16)],
        compiler_params=_params("arbitrary", "arbitrary"),
        name="proj_matmul",
    )(*args)


def _ffn_kernel(st_ref, se_ref, r0_ref, nr_ref, h_ref, wg_ref, wu_ref, wd_ref, o_ref, wgb, wub, wdb,
                *, sub, nsub, ncol):
    del se_ref
    k = pl.program_id(0)
    f = pl.program_id(1)
    nr = nr_ref[k]
    r0 = pl.multiple_of(r0_ref[k], sub)
    bm, d = o_ref.shape
    nchunk = (nr + sub - 1) // sub
    first = jnp.logical_or(k == 0, st_ref[k] != st_ref[jnp.maximum(k - 1, 0)])

    @pl.when(jnp.logical_and(f == 0, first))
    def _():
        o_ref[...] = jnp.zeros((bm, d), F32)

    for c in range(1, nsub + 1):
        rows = c * sub

        @pl.when(nchunk == c)
        def _():
            h = h_ref[pl.ds(r0, rows), :]
            wgb[...] = wg_ref[0].astype(BF16)
            g = _dot(h, wgb[...])
            wub[...] = wu_ref[0].astype(BF16)
            u = _dot(h, wub[...])
            wdb[...] = wd_ref[0].astype(BF16)
            a = (g * _sigmoid(g) * u).astype(BF16)
            for n in range(ncol):
                cols = slice(n * (d // ncol), (n + 1) * (d // ncol))
                o_ref[pl.ds(r0, rows), cols] += _dot(a, wdb[:, cols])


def _ffn(h, w_gate, w_up, w_down, seg_tile, seg_expert, seg_row0, seg_rows):
    rows, d = h.shape
    _, _, dff = w_gate.shape
    bm, bf, sub = FFN_BM, FFN_BF, FFN_SUB
    nf = dff // bf
    nseg = seg_tile.shape[0]

    def f_eff(k, f, nr):
        return jnp.where(nr[k] > 0, f, nf - 1)

    grid_spec = pltpu.PrefetchScalarGridSpec(
        num_scalar_prefetch=4,
        grid=(nseg, nf),
        in_specs=[
            pl.BlockSpec((bm, d), lambda k, f, st, se, r0, nr: (st[k], 0), pipeline_mode=pl.Buffered(1)),
            pl.BlockSpec((1, d, bf), lambda k, f, st, se, r0, nr: (se[k], 0, f_eff(k, f, nr))),
            pl.BlockSpec((1, d, bf), lambda k, f, st, se, r0, nr: (se[k], 0, f_eff(k, f, nr))),
            pl.BlockSpec((1, bf, d), lambda k, f, st, se, r0, nr: (se[k], f_eff(k, f, nr), 0)),
        ],
        out_specs=pl.BlockSpec((bm, d), lambda k, f, st, se, r0, nr: (st[k], 0), pipeline_mode=pl.Buffered(1)),
        scratch_shapes=[
            pltpu.VMEM((d, bf), BF16),
            pltpu.VMEM((d, bf), BF16),
            pltpu.VMEM((bf, d), BF16),
        ],
    )
    return pl.pallas_call(
        functools.partial(_ffn_kernel, sub=sub, nsub=bm // sub, ncol=FFN_NCOL),
        grid_spec=grid_spec,
        out_shape=jax.ShapeDtypeStruct((rows, d), F32),
        compiler_params=_params("arbitrary", "arbitrary"),
        name="swiglu_ffn",
    )(seg_tile, seg_expert, seg_row0, seg_rows, h, w_gate, w_up, w_down)


def _sb_kernel(q_ref, k_ref, v_ref, tri_ref, tri2_ref, o_ref, *, bq, nh, scale):
    i = pl.program_id(2)
    dh = HEAD_DIM
    qs = [q_ref[:, h * dh:(h + 1) * dh] for h in range(nh)]

    def tile(k0, kwid, state, tri, shift):
        carry, acc = state
        ks = [k_ref[pl.ds(k0, kwid), h * dh:(h + 1) * dh] for h in range(nh)]
        vs = [v_ref[pl.ds(k0, kwid), h * dh:(h + 1) * dh] for h in range(nh)]
        z = jnp.concatenate([_dot_nt(qs[h], ks[h]) for h in range(nh)], axis=0) * scale
        log_beta = jnp.minimum(z, 0.0) - jnp.log(1.0 + jnp.exp(-jnp.abs(z)))
        log_keep = log_beta - z
        if shift is not None:
            row = lax.broadcasted_iota(jnp.int32, (bq, kwid), 0)
            col = lax.broadcasted_iota(jnp.int32, (bq, kwid), 1)
            causal = jnp.concatenate([jnp.where(col < row + shift, 1.0, 0.0)] * nh, axis=0) > 0.5
            log_keep = jnp.where(causal, log_keep, 0.0)
        hi, lo = _split_bf16(log_keep)
        between = _dot(jnp.concatenate([hi, lo], axis=1), tri[...]) + carry
        w = jnp.exp(log_beta + between)
        if shift is not None:
            w = jnp.where(causal, w, 0.0)
        wb = w.astype(BF16)
        pv = jnp.concatenate([_dot(wb[h * bq:(h + 1) * bq, :], vs[h]) for h in range(nh)], axis=0)
        return carry + jnp.sum(log_keep, axis=-1, keepdims=True), acc + pv

    state0 = (jnp.zeros((nh * bq, 1), F32), jnp.zeros((nh * bq, dh), F32))
    state = lax.cond(
        i == 0,
        lambda: tile(0, bq, state0, tri_ref, 0),
        lambda: tile(pl.multiple_of((i - 1) * bq, bq), 2 * bq, state0, tri2_ref, bq))
    n_left = jnp.maximum(i - 1, 0)

    def cond(c):
        return jnp.logical_and(c[0] < n_left, c[1] > SB_EXP_ZERO)

    def body(c):
        st = tile(pl.multiple_of((n_left - 1 - c[0]) * bq, bq), bq, c[2], tri_ref, None)
        return c[0] + 1, jnp.max(st[0]), st

    _, _, state = lax.while_loop(cond, body, (jnp.int32(0), jnp.max(state[0]), state))
    o_ref[...] = jnp.concatenate([state[1][h * bq:(h + 1) * bq, :] for h in range(nh)],
                                 axis=1).astype(o_ref.dtype)


def _sb_attention(qkv, batch, seq, heads):
    t = qkv.shape[0]
    bq = min(SB_BQ, seq)
    nq = seq // bq
    nh = SB_HEADS_PER_STEP
    hw = nh * HEAD_DIM
    ng = heads // nh
    def suffix_op(width):
        rj = lax.broadcasted_iota(jnp.int32, (2 * width, width), 0)
        cs = lax.broadcasted_iota(jnp.int32, (2 * width, width), 1)
        return jnp.where((rj % width) > cs, 1.0, 0.0).astype(BF16)

    return pl.pallas_call(
        functools.partial(_sb_kernel, bq=bq, nh=nh, scale=HEAD_DIM ** -0.5),
        grid=(batch, ng, nq),
        in_specs=[
            pl.BlockSpec((bq, hw), lambda b, h, i: (b * nq + i, h)),
            pl.BlockSpec((seq, hw), lambda b, h, i: (b, ng + h)),
            pl.BlockSpec((seq, hw), lambda b, h, i: (b, 2 * ng + h)),
            pl.BlockSpec((2 * bq, bq), lambda b, h, i: (0, 0)),
            pl.BlockSpec((4 * bq, 2 * bq), lambda b, h, i: (0, 0)),
        ],
        out_specs=pl.BlockSpec((bq, hw), lambda b, h, i: (b * nq + i, h)),
        out_shape=jax.ShapeDtypeStruct((t, heads * HEAD_DIM), BF16),
        compiler_params=_params("arbitrary", "arbitrary", "arbitrary"),
        name="stickbreak_attn",
    )(qkv, qkv, qkv, suffix_op(bq), suffix_op(2 * bq))


def _rglru_kernel(x_ref, g_ref, cw_ref, cb_ref, wa_ref, ba_ref, wx_ref, bx_ref, lam_ref, o_ref, xp_ref,
                  *, seq, cb, tc, kw):
    pad = SUBLANE
    xp_ref[0:pad, :] = jnp.zeros((pad, cb), F32)
    xp_ref[pad:, :] = x_ref[...]
    log_lam = _log_sigmoid(lam_ref[...])
    sub_iota = lax.broadcasted_iota(jnp.int32, (SUBLANE, cb), 0)
    ngroup = cb // LANE

    def chunk(ci, h):
        t0 = pl.multiple_of(ci * tc, tc)
        win = xp_ref[pl.ds(t0, tc + pad), :]
        xc = cb_ref[...] + cw_ref[0:1, :] * win[pad - kw + 1:pad - kw + 1 + tc, :]
        for k in range(1, kw):
            off = pad - kw + 1 + k
            xc = xc + cw_ref[k:k + 1, :] * win[off:off + tc, :]
        xcb = xc.astype(BF16)
        ra = jnp.concatenate(
            [_dot(xcb[:, q * LANE:(q + 1) * LANE], wa_ref[q].astype(BF16)) for q in range(ngroup)], axis=1)
        rx = jnp.concatenate(
            [_dot(xcb[:, q * LANE:(q + 1) * LANE], wx_ref[q].astype(BF16)) for q in range(ngroup)], axis=1)
        r = _sigmoid(ra + ba_ref[...])
        gi = _sigmoid(rx + bx_ref[...])
        log_a = RG_C * r * log_lam
        a = jnp.exp(log_a)
        u = jnp.sqrt(-jnp.tanh(log_a) * (a * a + 1.0)) * (gi * xc)
        gate = _gelu_tanh(g_ref[pl.ds(t0, tc), :])
        outs = []
        for gidx in range(tc // SUBLANE):
            av = a[gidx * SUBLANE:(gidx + 1) * SUBLANE, :]
            bv = u[gidx * SUBLANE:(gidx + 1) * SUBLANE, :]
            for sh in (1, 2, 4):
                a_s = pltpu.roll(av, sh, axis=0)
                b_s = pltpu.roll(bv, sh, axis=0)
                m = sub_iota >= sh
                bv = jnp.where(m, av * b_s + bv, bv)
                av = jnp.where(m, av * a_s, av)
            hv = av * h + bv
            outs.append(hv)
            h = jnp.broadcast_to(hv[SUBLANE - 1:SUBLANE, :], (SUBLANE, cb))
        hs = jnp.concatenate(outs, axis=0)
        o_ref[pl.ds(t0, tc), :] = (hs * gate).astype(o_ref.dtype)
        return h

    lax.fori_loop(0, seq // tc, chunk, jnp.zeros((SUBLANE, cb), F32))


def _block_diag_pairs(w):
    nblk, c, _ = w.shape
    w2 = w.reshape(nblk // 2, 2, c, c)
    z = jnp.zeros((nblk // 2, c, c), w.dtype)
    top = jnp.concatenate([w2[:, 0], z], axis=2)
    bot = jnp.concatenate([z, w2[:, 1]], axis=2)
    return jnp.concatenate([top, bot], axis=1)


def _rglru(rg, batch, seq, conv_w, conv_b, wa, ba, wx, bx, lam):
    t, c2 = rg.shape
    c = c2 // 2
    cb = min(RG_CB, c)
    tc = min(RG_TC, seq)
    ncb = c // cb
    kw = conv_w.shape[0]
    gpb = cb // LANE
    vec = pl.BlockSpec((1, cb), lambda b, j: (0, j))
    return pl.pallas_call(
        functools.partial(_rglru_kernel, seq=seq, cb=cb, tc=tc, kw=kw),
        grid=(batch, ncb),
        in_specs=[
            pl.BlockSpec((seq, cb), lambda b, j: (b, j)),
            pl.BlockSpec((seq, cb), lambda b, j: (b, ncb + j)),
            pl.BlockSpec((kw, cb), lambda b, j: (0, j)),
            vec,
            pl.BlockSpec((gpb, LANE, LANE), lambda b, j: (j, 0, 0)),
            vec,
            pl.BlockSpec((gpb, LANE, LANE), lambda b, j: (j, 0, 0)),
            vec,
            vec,
        ],
        out_specs=pl.BlockSpec((seq, cb), lambda b, j: (b, j)),
        out_shape=jax.ShapeDtypeStruct((t, c), BF16),
        scratch_shapes=[pltpu.VMEM((seq + SUBLANE, cb), F32)],
        compiler_params=_params("arbitrary", "arbitrary"),
        name="rglru",
    )(rg, rg, conv_w, conv_b.reshape(1, c), _block_diag_pairs(wa), ba.reshape(1, c),
      _block_diag_pairs(wx), bx.reshape(1, c), lam.reshape(1, c))


def _compress_kernel(x_ref, pos_ref, w1_ref, w2_ref, o_ref, *, ngrp):
    st = CMP_STRIDE
    half = st * HEAD_DIM
    xs = [x_ref[pl.ds(r, ngrp, stride=st), :] for r in range(st)]
    pos = pos_ref[0]
    x0 = jnp.concatenate([xs[r] + pos[r:r + 1, :] for r in range(st)], axis=1).astype(BF16)
    x1 = jnp.concatenate([xs[r] + pos[st + r:st + r + 1, :] for r in range(st)], axis=1).astype(BF16)
    p0 = _dot(x0, w1_ref[0, 0:half, :].astype(BF16))
    p1 = _dot(x1, w1_ref[0, half:2 * half, :].astype(BF16))
    pre = p0 + pltpu.roll(p1, ngrp - 1, axis=0)
    out = _dot(_gelu_tanh(pre).astype(BF16), w2_ref[0].astype(BF16))
    rown = lax.broadcasted_iota(jnp.int32, out.shape, 0)
    o_ref[...] = jnp.where(rown < ngrp - 1, out, 0.0).astype(o_ref.dtype)


def _compress(kcvc, batch, seq, pos, w1, w2):
    assert CMP_BLOCK == 2 * CMP_STRIDE
    g2 = kcvc.shape[1] // HEAD_DIM
    per = g2 // 2
    ngrp = seq // CMP_STRIDE
    return pl.pallas_call(
        functools.partial(_compress_kernel, ngrp=ngrp),
        grid=(batch, g2),
        in_specs=[
            pl.BlockSpec((seq, HEAD_DIM), lambda b, j: (b, j)),
            pl.BlockSpec((1, CMP_BLOCK, HEAD_DIM), lambda b, j: (j // per, 0, 0)),
            pl.BlockSpec((1, CMP_BLOCK * HEAD_DIM, HEAD_DIM), lambda b, j: (j // per, 0, 0)),
            pl.BlockSpec((1, HEAD_DIM, HEAD_DIM), lambda b, j: (j // per, 0, 0)),
        ],
        out_specs=pl.BlockSpec((ngrp, HEAD_DIM), lambda b, j: (b * g2 + j, 0)),
        out_shape=jax.ShapeDtypeStruct((batch * g2 * ngrp, HEAD_DIM), BF16),
        compiler_params=_params("arbitrary", "arbitrary"),
        name="nsa_compress",
    )(kcvc, pos, w1, w2)


def _softmax_parts(s2, bias):
    sb = s2 + bias
    m = jnp.max(sb, axis=-1, keepdims=True)
    e = jnp.exp2(sb - m)
    return e, jnp.sum(e, axis=-1, keepdims=True)


def _nsa_kernel(q_ref, kc_ref, vc_ref, ks_ref, vs_ref, kw_ref, vw_ref, g_ref, ov_ref, ex_ref, o_ref,
                acc_ref, inv_ref, *, seq, hg, n_slc, scale):
    bq, bk, dh = NSA_BQ, NSA_BK, HEAD_DIM
    i = pl.program_id(2)
    q0 = i * bq
    qb = q_ref[...]
    qs = jnp.concatenate([qb[:, h * dh:(h + 1) * dh] for h in range(hg)], axis=0)
    qpos1 = q0 + lax.broadcasted_iota(jnp.int32, (bq, 1), 0)
    qpos = jnp.concatenate([qpos1] * hg, axis=0)

    scale2 = scale * LOG2E
    tile_heads = lambda a: jnp.concatenate([a] * hg, axis=0)

    ncmp = kc_ref.shape[0]
    s_c = _dot_nt(qs, kc_ref[...]) * scale2
    n_idx = lax.broadcasted_iota(jnp.int32, (1, ncmp), 1)
    bias_c = jnp.where(n_idx * CMP_STRIDE + (CMP_BLOCK - 1) <= qpos1, 0.0, NEG_BIG)
    e_c, l_c = _softmax_parts(s_c, tile_heads(bias_c))
    p_c = e_c * jnp.where(qpos >= CMP_BLOCK - 1, 1.0 / l_c, 0.0)
    o_c = _dot(p_c.astype(BF16), vc_ref[...])

    p_sum = p_c[0:bq, :]
    for h in range(1, hg):
        p_sum = p_sum + p_c[h * bq:(h + 1) * bq, :]
    overlap_t = ov_ref[...]
    p_hi, p_lo = _split_bf16(p_sum)
    imp_t = (_dot_nt(overlap_t, p_hi) + _dot_nt(overlap_t, p_lo))[0:n_slc, :]

    blk = lax.broadcasted_iota(jnp.int32, (n_slc, bq), 0)
    qp_l = q0 + lax.broadcasted_iota(jnp.int32, (n_slc, bq), 1)
    cur = qp_l // SLC_BLOCK
    forced = (blk == 0) | (blk == cur) | (blk == cur - 1)
    valid = blk * SLC_BLOCK <= qp_l
    rank = jnp.where(valid, imp_t + FORCE_BONUS * jnp.where(forced, 1.0, 0.0), -jnp.inf)
    ahead = jnp.zeros((n_slc, bq), F32)
    for jp in range(n_slc):
        other = rank[jp:jp + 1, :]
        beats = (other > rank) | ((other == rank) & (blk > jp))
        ahead = ahead + jnp.where(beats, 1.0, 0.0)
    k_sel = min(N_SEL, n_slc)
    sel_t = jnp.where(valid & (ahead < k_sel), 1.0, 0.0)
    sel_t = jnp.concatenate([sel_t, jnp.zeros((LANE - n_slc, bq), F32)], axis=0)
    sel = sel_t.T.astype(BF16)

    n_need = (q0 + bq + bk - 1) // bk
    for v in range(1, seq // bk + 1):
        nk = v * bk

        @pl.when(n_need == v)
        def _():
            s = _dot_nt(qs, ks_ref[0:nk, :]) * scale2
            picked = _dot(sel, ex_ref[:, 0:nk])
            kpos = lax.broadcasted_iota(jnp.int32, (1, nk), 1)
            bias = jnp.where((picked > 0.5) & (kpos <= qpos1), 0.0, NEG_BIG)
            e_s, l_s = _softmax_parts(s, tile_heads(bias))
            acc_ref[...] = _dot(e_s.astype(BF16), vs_ref[0:nk, :])
            inv_ref[...] = 1.0 / l_s

    acc_s = acc_ref[...]
    inv_s = inv_ref[...]

    span = min(WINDOW + bq, seq)
    w0 = pl.multiple_of(jnp.maximum(jnp.minimum(q0 - WINDOW, seq - span), 0), bq)
    kwt = kw_ref[pl.ds(w0, span), :]
    vwt = vw_ref[pl.ds(w0, span), :]
    s_w = _dot_nt(qs, kwt) * scale2
    kpos_w = w0 + lax.broadcasted_iota(jnp.int32, (1, span), 1)
    bias_w = jnp.where((kpos_w <= qpos1) & (kpos_w > qpos1 - WINDOW), 0.0, NEG_BIG)
    e_w, l_w = _softmax_parts(s_w, tile_heads(bias_w))
    acc_w = _dot(e_w.astype(BF16), vwt)
    inv_w = 1.0 / l_w

    gates = _sigmoid(g_ref[...])
    outs = []
    for h in range(hg):
        r = slice(h * bq, (h + 1) * bq)
        c = h * N_BRANCH
        outs.append(gates[:, c:c + 1] * o_c[r, :] + (gates[:, c + 1:c + 2] * inv_s[r, :]) * acc_s[r, :]
                    + (gates[:, c + 2:c + 3] * inv_w[r, :]) * acc_w[r, :])
    o_ref[...] = jnp.concatenate(outs, axis=1).astype(o_ref.dtype)


def _nsa_attention(q, kv4, cmp, gate_logits, batch, seq):
    t, width = q.shape
    g = NSA_KV_HEADS
    hg = width // (g * HEAD_DIM)
    bq = NSA_BQ
    nb = seq // bq
    ncmp = seq // CMP_STRIDE
    n_slc = seq // SLC_BLOCK
    dh = HEAD_DIM
    full = lambda off: pl.BlockSpec((seq, dh), lambda b, gi, i: (b, off * g + gi))
    jj = lax.broadcasted_iota(jnp.int32, (LANE, ncmp), 0)
    cstart = lax.broadcasted_iota(jnp.int32, (LANE, ncmp), 1) * CMP_STRIDE
    overlap_t = jnp.where((cstart < (jj + 1) * SLC_BLOCK) & (cstart + CMP_BLOCK > jj * SLC_BLOCK)
                          & (jj < n_slc), 1.0, 0.0).astype(BF16)
    expand = jnp.where(lax.broadcasted_iota(jnp.int32, (LANE, seq), 1) // SLC_BLOCK
                       == lax.broadcasted_iota(jnp.int32, (LANE, seq), 0), 1.0, 0.0).astype(BF16)
    const = lambda shape: pl.BlockSpec(shape, lambda b, gi, i: (0, 0))
    return pl.pallas_call(
        functools.partial(_nsa_kernel, seq=seq, hg=hg, n_slc=n_slc, scale=dh ** -0.5),
        grid=(batch, g, nb),
        in_specs=[
            pl.BlockSpec((bq, hg * dh), lambda b, gi, i: (b * nb + i, gi)),
            pl.BlockSpec((ncmp, dh), lambda b, gi, i: (b * 2 * g + gi, 0)),
            pl.BlockSpec((ncmp, dh), lambda b, gi, i: (b * 2 * g + g + gi, 0)),
            full(0), full(1), full(2), full(3),
            pl.BlockSpec((bq, LANE), lambda b, gi, i: (b * nb + i, gi)),
            const((LANE, ncmp)),
            const((LANE, seq)),
        ],
        out_specs=pl.BlockSpec((bq, hg * dh), lambda b, gi, i: (b * nb + i, gi)),
        out_shape=jax.ShapeDtypeStruct((t, width), BF16),
        scratch_shapes=[pltpu.VMEM((hg * bq, dh), F32), pltpu.VMEM((hg * bq, 1), F32)],
        compiler_params=_params("arbitrary", "arbitrary", "arbitrary"),
        name="nsa_attn",
    )(q, cmp, cmp, kv4, kv4, kv4, kv4, gate_logits, overlap_t, expand)


def _gather_kernel(lo_ref, hi_ref, src_ref, h_ref, o_ref, *, chunk):
    i = pl.program_id(0)
    rows = o_ref.shape[0]
    src = src_ref[...]
    lane = lax.broadcasted_iota(jnp.int32, (rows, chunk), 1)
    o_ref[...] = jnp.zeros(o_ref.shape, o_ref.dtype)

    def body(c, carry):
        c0 = pl.multiple_of(c * chunk, chunk)
        onehot = jnp.where(src - c0 == lane, 1.0, 0.0).astype(BF16)
        o_ref[...] += _dot(onehot, h_ref[pl.ds(c0, chunk), :]).astype(o_ref.dtype)
        return carry

    lax.fori_loop(lo_ref[i], hi_ref[i], body, 0)


def _gather_rows(h, src, n_rows):
    t, d = h.shape
    bt, chunk = GATHER_ROWS, GATHER_CHUNK
    nt = n_rows // bt
    src2 = src.reshape(nt, bt)
    live = src2 >= 0
    lo = jnp.min(jnp.where(live, src2, t), axis=1) // chunk
    hi = jnp.where(jnp.any(live, axis=1), jnp.max(src2, axis=1) // chunk + 1, lo)
    grid_spec = pltpu.PrefetchScalarGridSpec(
        num_scalar_prefetch=2,
        grid=(nt,),
        in_specs=[
            pl.BlockSpec((bt, 1), lambda i, lo, hi: (i, 0)),
            pl.BlockSpec((t, d), lambda i, lo, hi: (0, 0), pipeline_mode=pl.Buffered(1)),
        ],
        out_specs=pl.BlockSpec((bt, d), lambda i, lo, hi: (i, 0)),
    )
    return pl.pallas_call(
        functools.partial(_gather_kernel, chunk=chunk),
        grid_spec=grid_spec,
        out_shape=jax.ShapeDtypeStruct((n_rows, d), h.dtype),
        compiler_params=_params("arbitrary"),
        name="moe_dispatch",
    )(jnp.minimum(lo, hi).astype(jnp.int32), hi.astype(jnp.int32), src.reshape(n_rows, 1), h)


def _router_kernel(l_ref, o_ref, *, n_exp):
    x = l_ref[...]
    lane = lax.broadcasted_iota(jnp.int32, x.shape, 1)
    xm = jnp.where(lane < n_exp, x, -jnp.inf)
    v0 = jnp.max(xm, axis=-1, keepdims=True)
    i0 = jnp.min(jnp.where(xm == v0, lane, LANE), axis=-1, keepdims=True)
    xm = jnp.where(lane == i0, -jnp.inf, xm)
    v1 = jnp.max(xm, axis=-1, keepdims=True)
    i1 = jnp.min(jnp.where(xm == v1, lane, LANE), axis=-1, keepdims=True)
    e = jnp.exp(v1 - v0)
    p0 = 1.0 / (1.0 + e)
    out = jnp.where(lane == 0, p0, jnp.where(lane == 1, e * p0, jnp.where(
        lane == 2, i0.astype(F32), jnp.where(lane == 3, i1.astype(F32), 0.0))))
    o_ref[...] = out


def _router_top2(logits, n_exp):
    t = logits.shape[0]
    bt = ROW_TILE
    out = pl.pallas_call(
        functools.partial(_router_kernel, n_exp=n_exp),
        grid=(t // bt,),
        in_specs=[pl.BlockSpec((bt, LANE), lambda i: (i, 0))],
        out_specs=pl.BlockSpec((bt, LANE), lambda i: (i, 0)),
        out_shape=jax.ShapeDtypeStruct((t, LANE), F32),
        compiler_params=_params("arbitrary"),
        name="router_top2",
    )(logits)
    return out[:, 0:TOP_K], out[:, TOP_K:2 * TOP_K].astype(jnp.int32)


def _route(probs, top_i, n_experts):
    t = top_i.shape[0]
    flat_e = top_i.reshape(-1)
    onehot = (flat_e[:, None] == jnp.arange(n_experts)[None, :]).astype(jnp.int32)
    rank = jnp.take_along_axis(jnp.cumsum(onehot, axis=0), flat_e[:, None], axis=1)[:, 0] - 1
    counts = jnp.sum(onehot, axis=0)
    padded = (counts + FFN_SUB - 1) // FFN_SUB * FFN_SUB
    end = jnp.cumsum(padded)
    start = end - padded
    n_rows = -(-(TOP_K * t + n_experts * (FFN_SUB - 1)) // FFN_BM) * FFN_BM
    n_tiles = n_rows // FFN_BM
    dest = (start[flat_e] + rank).astype(jnp.int32)
    src = jnp.full((n_rows,), -1, jnp.int32).at[dest].set(jnp.arange(TOP_K * t, dtype=jnp.int32) // TOP_K)
    cuts = jnp.sort(jnp.concatenate([start, jnp.arange(n_tiles) * FFN_BM]))
    nxt = jnp.concatenate([cuts[1:], jnp.full((1,), n_rows, cuts.dtype)])
    seg_rows = jnp.maximum(jnp.minimum(nxt, end[-1]) - cuts, 0)
    seg_tile = jnp.minimum(cuts // FFN_BM, n_tiles - 1)
    seg_expert = jnp.minimum(jnp.searchsorted(end, cuts, side="right"), n_experts - 1)
    i32 = lambda a: a.astype(jnp.int32)
    return probs, dest.reshape(t, TOP_K), src, i32(seg_tile), i32(seg_expert), i32(cuts % FFN_BM), i32(seg_rows)


def kernel(x, c, ada_mix_w, ada_mix_b, norm_mix_g, ada_ffn_w, ada_ffn_b, norm_ffn_g, even_in_w, rg_conv_w, rg_conv_b, rg_wa, rg_ba, rg_wx, rg_bx, rg_lambda, even_out_w, dense_w_gate, dense_w_up, dense_w_down, nsa_in_w, cmp_pos_k, cmp_pos_v, cmp_k_w1, cmp_k_w2, cmp_v_w1, cmp_v_w2, nsa_out_w, router_w, moe_w_gate, moe_w_up, moe_w_down, final_norm_g):
    batch, seq, d = x.shape
    t = batch * seq
    depth = ada_mix_w.shape[0]
    xf = x.reshape(t, d)

    c_pad = jnp.pad(c, ((0, (-batch) % SUBLANE), (0, 0)))
    m_mix = _adaln(c_pad, ada_mix_w, ada_mix_b)
    m_ffn = _adaln(c_pad, ada_ffn_w, ada_ffn_b)

    def mods(m, layer):
        v = m[layer, :batch].reshape(batch, 1, 3, d)
        return v[:, :, 0], v[:, :, 1], v[:, :, 2]

    def norm_step(xf, pending, g, mod, out_dtype=BF16):
        if pending is not None and pending[0] == "experts":
            _, y, dest, probs, gate = pending
            return _combine_norm(xf, y, dest, probs, gate, g, seq, mod=mod, out_dtype=out_dtype)
        res = None if pending is None else pending[1:]
        return _normmod(xf, g, seq, res=res, mod=mod, out_dtype=out_dtype)

    pending = None
    for layer in range(depth):
        j = layer // 2
        shift, scale, gate = mods(m_mix, layer)
        xf, h = norm_step(xf, pending, norm_mix_g[layer], (shift, scale))
        if layer % 2 == 0:
            w_in = even_in_w[j]
            sbw = (w_in.shape[1] - 2 * rg_conv_w.shape[2]) // 3
            heads = sbw // HEAD_DIM
            qkv = _matmul([h], w_in, 0, 3 * sbw, BF16)
            rg = _matmul([h], w_in, 3 * sbw, w_in.shape[1] - 3 * sbw, F32)
            o_a = _sb_attention(qkv, batch, seq, heads)
            o_b = _rglru(rg, batch, seq, rg_conv_w[j], rg_conv_b[j], rg_wa[j], rg_ba[j], rg_wx[j],
                         rg_bx[j], rg_lambda[j])
            xf = _matmul([o_a, o_b], even_out_w[j], 0, d, F32, residual=(xf, gate, seq))
        else:
            w_in = nsa_in_w[j]
            g = NSA_KV_HEADS
            kvw = g * HEAD_DIM
            nsa_w = nsa_out_w.shape[1]
            hg = nsa_w // kvw
            q = _matmul([h], w_in, 0, nsa_w, BF16)
            kcvc = _matmul([h], w_in, nsa_w, 2 * kvw, F32)
            kv4 = _matmul([h], w_in, nsa_w + 2 * kvw, 4 * kvw, BF16)
            wg = w_in[:, nsa_w + 6 * kvw:].reshape(d, g, hg * N_BRANCH)
            wg = jnp.pad(wg, ((0, 0), (0, 0), (0, LANE - hg * N_BRANCH))).reshape(d, g * LANE)
            gl = _matmul([h], wg, 0, g * LANE, F32)
            cmp = _compress(kcvc, batch, seq, jnp.stack([cmp_pos_k[j], cmp_pos_v[j]]),
                            jnp.stack([cmp_k_w1[j], cmp_v_w1[j]]), jnp.stack([cmp_k_w2[j], cmp_v_w2[j]]))
            o = _nsa_attention(q, kv4, cmp, gl, batch, seq)
            xf = _matmul([o], nsa_out_w[j], 0, d, F32, residual=(xf, gate, seq))
        pending = None

        shift, scale, gate = mods(m_ffn, layer)
        xf, h = norm_step(xf, pending, norm_ffn_g[layer], (shift, scale))
        if layer % 2 == 0:
            nt = t // FFN_BM
            ffn = _ffn(h, dense_w_gate[j:j + 1], dense_w_up[j:j + 1], dense_w_down[j:j + 1],
                       jnp.arange(nt, dtype=jnp.int32), jnp.zeros((nt,), jnp.int32),
                       jnp.zeros((nt,), jnp.int32), jnp.full((nt,), FFN_BM, jnp.int32))
            pending = ("dense", ffn, gate)
        else:
            n_exp = router_w.shape[2]
            rw = jnp.pad(router_w[j], ((0, 0), (0, LANE - n_exp)))
            probs, top_i = _router_top2(_matmul([h], rw, 0, LANE, F32), n_exp)
            probs, dest, src, seg_tile, seg_expert, seg_row0, seg_rows = _route(probs, top_i, n_exp)
            h_sorted = _gather_rows(h, src, src.shape[0])
            y = _ffn(h_sorted, moe_w_gate[j], moe_w_up[j], moe_w_down[j], seg_tile, seg_expert, seg_row0,
                     seg_rows)
            pending = ("experts", y, dest, probs, gate)

    _, out = norm_step(xf, pending, final_norm_g, None, F32)
    return out.reshape(batch, seq, d)
```

```python
import functools

import jax
import jax.numpy as jnp
from jax import lax
from jax.experimental import pallas as pl
from jax.experimental.pallas import tpu as pltpu

F32 = jnp.float32
BF16 = jnp.bfloat16

LANE = 128
SUBLANE = 8
VMEM_LIMIT_BYTES = 58 * 1024 * 1024

HEAD_DIM = 128
EPS = 1e-6
RG_C = 8.0
NSA_KV_HEADS = 4
N_BRANCH = 3
CMP_BLOCK = 32
CMP_STRIDE = 16
SLC_BLOCK = 64
N_SEL = 8
WINDOW = 512
FORCE_BONUS = 1e6
TOP_K = 2
NEG_BIG = -1e30
LOG2E = 1.4426950408889634
SB_EXP_ZERO = -110.0

ROW_TILE = 512
COMBINE_ROWS = 256
MM_BM = 2048
MM_BN = 512
FFN_BM = 2048
FFN_SUB = 256
FFN_NCOL = 4
FFN_BF = 256
GATHER_ROWS = 256
GATHER_CHUNK = 512
SB_BQ = 256
SB_HEADS_PER_STEP = 4
NSA_BQ = 128
NSA_BK = 256
NSA_GROUPS_PER_STEP = 2
RG_CB = 256
RG_TC = 256


def _params(*sem):
    return pltpu.CompilerParams(dimension_semantics=sem, vmem_limit_bytes=VMEM_LIMIT_BYTES)


def _dot(a, b):
    return jnp.dot(a, b, preferred_element_type=F32)


def _dot_nt(a, b):
    return lax.dot_general(a, b, (((1,), (1,)), ((), ())), preferred_element_type=F32)


def _sigmoid(x):
    return 1.0 / (1.0 + jnp.exp(-x))


def _log_sigmoid(x):
    return jnp.minimum(x, 0.0) - jnp.log1p(jnp.exp(-jnp.abs(x)))


def _gelu_tanh(x):
    return 0.5 * x * (1.0 + jnp.tanh(0.7978845608028654 * (x + 0.044715 * (x * x * x))))


def _split_bf16(x):
    hi = x.astype(BF16)
    lo = (x - hi.astype(F32)).astype(BF16)
    return hi, lo


def _adaln_kernel(c_ref, w_ref, b_ref, o_ref):
    c = c_ref[...]
    s = (c * _sigmoid(c)).astype(BF16)
    o_ref[0] = _dot(s, w_ref[0].astype(BF16)) + b_ref[0]


def _adaln(c_pad, w, b, bn=1024):
    depth, d, n3 = w.shape
    rows = c_pad.shape[0]
    return pl.pallas_call(
        _adaln_kernel,
        grid=(depth, n3 // bn),
        in_specs=[
            pl.BlockSpec((rows, d), lambda l, j: (0, 0)),
            pl.BlockSpec((1, d, bn), lambda l, j: (l, 0, j)),
            pl.BlockSpec((1, 1, bn), lambda l, j: (l, 0, j)),
        ],
        out_specs=pl.BlockSpec((1, rows, bn), lambda l, j: (l, 0, j)),
        out_shape=jax.ShapeDtypeStruct((depth, rows, n3), F32),
        compiler_params=_params("arbitrary", "arbitrary"),
        name="adaln",
    )(c_pad, w, b.reshape(depth, 1, n3))


def _normmod_kernel(*refs, has_res, has_mod):
    it = iter(refs)
    x_ref = next(it)
    if has_res:
        y_ref, gate_ref = next(it), next(it)
    g_ref = next(it)
    if has_mod:
        shift_ref, scale_ref = next(it), next(it)
    if has_res:
        xo_ref = next(it)
    h_ref = next(it)

    x = x_ref[...]
    if has_res:
        x = x + gate_ref[0] * y_ref[...]
        xo_ref[...] = x
    ms = jnp.mean(x * x, axis=-1, keepdims=True)
    h = x * lax.rsqrt(ms + EPS) * g_ref[...]
    if has_mod:
        h = h * (1.0 + scale_ref[0]) + shift_ref[0]
    h_ref[...] = h.astype(h_ref.dtype)


def _normmod(x, g, seq, res=None, mod=None, out_dtype=BF16):
    t, d = x.shape
    bt = ROW_TILE
    per_b = seq // bt
    row = pl.BlockSpec((bt, d), lambda i: (i, 0))
    vec = pl.BlockSpec((1, 1, d), lambda i: (i // per_b, 0, 0))
    args, specs = [x], [row]
    if res is not None:
        args += [res[0], res[1]]
        specs += [row, vec]
    args.append(g.reshape(1, d))
    specs.append(pl.BlockSpec((1, d), lambda i: (0, 0)))
    if mod is not None:
        args += [mod[0], mod[1]]
        specs += [vec, vec]
    out_shape, out_specs = [], []
    if res is not None:
        out_shape.append(jax.ShapeDtypeStruct((t, d), F32))
        out_specs.append(row)
    out_shape.append(jax.ShapeDtypeStruct((t, d), out_dtype))
    out_specs.append(row)
    outs = pl.pallas_call(
        functools.partial(_normmod_kernel, has_res=res is not None, has_mod=mod is not None),
        grid=(t // bt,),
        in_specs=specs,
        out_specs=out_specs,
        out_shape=out_shape,
        compiler_params=_params("arbitrary"),
        name="normmod",
    )(*args)
    return outs if res is not None else (x, outs[0])


def _combine_norm_kernel(dest_ref, *refs, has_mod, bt, n_steps):
    it = iter(refs)
    x_ref, y_hbm, p_ref, gate_ref, g_ref = (next(it) for _ in range(5))
    if has_mod:
        shift_ref, scale_ref, xo_ref = next(it), next(it), next(it)
    h_ref, buf, sem = next(it), next(it), next(it)
    i = pl.program_id(0)

    def row_copy(src_row, slot, k, r):
        return pltpu.make_async_copy(y_hbm.at[pl.ds(src_row, 1), :], buf.at[slot, k, pl.ds(r, 1), :],
                                     sem.at[slot])

    def start_tile(tile, slot):
        base = tile * (bt * TOP_K)

        def body(r, carry):
            for k in range(TOP_K):
                row_copy(dest_ref[base + r * TOP_K + k], slot, k, r).start()
            return carry

        lax.fori_loop(0, bt, body, 0, unroll=8)

    @pl.when(i == 0)
    def _():
        start_tile(0, 0)

    @pl.when(i + 1 < n_steps)
    def _():
        start_tile(i + 1, (i + 1) % 2)

    slot = i % 2

    def wait_row(r, carry):
        for k in range(TOP_K):
            row_copy(0, slot, k, r).wait()
        return carry

    lax.fori_loop(0, bt, wait_row, 0, unroll=8)
    p = p_ref[...]
    y = p[:, 0:1] * buf[slot, 0]
    for k in range(1, TOP_K):
        y = y + p[:, k:k + 1] * buf[slot, k]
    x = x_ref[...] + gate_ref[0] * y
    ms = jnp.mean(x * x, axis=-1, keepdims=True)
    h = x * lax.rsqrt(ms + EPS) * g_ref[...]
    if has_mod:
        xo_ref[...] = x
        h = h * (1.0 + scale_ref[0]) + shift_ref[0]
    h_ref[...] = h.astype(h_ref.dtype)


def _combine_norm(x, y, dest, probs, gate, g, seq, mod=None, out_dtype=BF16):
    t, d = x.shape
    bt = COMBINE_ROWS
    n_steps = t // bt
    per_b = seq // bt
    row = pl.BlockSpec((bt, d), lambda i, dst: (i, 0))
    vec = pl.BlockSpec((1, 1, d), lambda i, dst: (i // per_b, 0, 0))
    args = [x, y, probs, gate, g.reshape(1, d)]
    specs = [row, pl.BlockSpec(memory_space=pl.ANY), pl.BlockSpec((bt, TOP_K), lambda i, dst: (i, 0)), vec,
             pl.BlockSpec((1, d), lambda i, dst: (0, 0))]
    out_shape, out_specs = [], []
    if mod is not None:
        args += [mod[0], mod[1]]
        specs += [vec, vec]
        out_shape.append(jax.ShapeDtypeStruct((t, d), F32))
        out_specs.append(row)
    out_shape.append(jax.ShapeDtypeStruct((t, d), out_dtype))
    out_specs.append(row)
    grid_spec = pltpu.PrefetchScalarGridSpec(
        num_scalar_prefetch=1,
        grid=(n_steps,),
        in_specs=specs,
        out_specs=out_specs,
        scratch_shapes=[pltpu.VMEM((2, TOP_K, bt, d), F32), pltpu.SemaphoreType.DMA((2,))],
    )
    outs = pl.pallas_call(
        functools.partial(_combine_norm_kernel, has_mod=mod is not None, bt=bt, n_steps=n_steps),
        grid_spec=grid_spec,
        out_shape=out_shape,
        compiler_params=_params("arbitrary"),
        name="moe_combine_norm",
    )(dest.reshape(-1), *args)
    return (outs[0], outs[1]) if mod is not None else (None, outs[0])


def _mm_kernel(*refs, ks, has_res):
    a_refs = refs[: len(ks)]
    rest = refs[len(ks):]
    w_ref = rest[0]
    if has_res:
        x_ref, gate_ref = rest[1], rest[2]
    o_ref, wb_ref = rest[-2], rest[-1]

    wb_ref[...] = w_ref[...].astype(BF16)

    acc = None
    k0 = 0
    for a_ref, k in zip(a_refs, ks):
        part = _dot(a_ref[...], wb_ref[k0:k0 + k, :])
        acc = part if acc is None else acc + part
        k0 += k
    if has_res:
        acc = x_ref[...] + gate_ref[0] * acc
    o_ref[...] = acc.astype(o_ref.dtype)


def _matmul(a_list, w, col0, n, out_dtype, residual=None):
    m = a_list[0].shape[0]
    ks = tuple(a.shape[1] for a in a_list)
    ktot = sum(ks)
    bm = min(MM_BM, m)
    bn = min(MM_BN, n)
    cb0 = col0 // bn
    args = list(a_list) + [w]
    specs = [pl.BlockSpec((bm, k), lambda j, i: (i, 0)) for k in ks]
    specs.append(pl.BlockSpec((ktot, bn), lambda j, i: (0, cb0 + j)))
    if residual is not None:
        x, gate, seq = residual
        bm = min(bm, seq)
        per_b = seq // bm
        specs[: len(ks)] = [pl.BlockSpec((bm, k), lambda j, i: (i, 0)) for k in ks]
        args += [x, gate]
        specs += [pl.BlockSpec((bm, bn), lambda j, i: (i, j)),
                  pl.BlockSpec((1, 1, bn), lambda j, i: (i // per_b, 0, j))]
    return pl.pallas_call(
        functools.partial(_mm_kernel, ks=ks, has_res=residual is not None),
        grid=(n // bn, m // bm),
        in_specs=specs,
        out_specs=pl.BlockSpec((bm, bn), lambda j, i: (i, j)),
        out_shape=jax.ShapeDtypeStruct((m, n), out_dtype),
        scratch_shapes=[pltpu.VMEM((ktot, bn), BF16)],
        compiler_params=_params("arbitrary", "arbitrary"),
        name="proj_matmul",
    )(*args)


def _ffn_kernel(st_ref, se_ref, r0_ref, nr_ref, h_ref, wg_ref, wu_ref, wd_ref, o_ref, wgb, wub, wdb,
                *, sub, nsub, ncol):
    del se_ref
    k = pl.program_id(0)
    f = pl.program_id(1)
    nr = nr_ref[k]
    r0 = pl.multiple_of(r0_ref[k], sub)
    bm, d = o_ref.shape
    nchunk = (nr + sub - 1) // sub
    first = jnp.logical_or(k == 0, st_ref[k] != st_ref[jnp.maximum(k - 1, 0)])

    @pl.when(jnp.logical_and(f == 0, first))
    def _():
        o_ref[...] = jnp.zeros((bm, d), F32)

    for c in range(1, nsub + 1):
        rows = c * sub

        @pl.when(nchunk == c)
        def _():
            h = h_ref[pl.ds(r0, rows), :]
            wgb[...] = wg_ref[0].astype(BF16)
            g = _dot(h, wgb[...])
            wub[...] = wu_ref[0].astype(BF16)
            u = _dot(h, wub[...])
            wdb[...] = wd_ref[0].astype(BF16)
            a = (g * _sigmoid(g) * u).astype(BF16)
            for n in range(ncol):
                cols = slice(n * (d // ncol), (n + 1) * (d // ncol))
                o_ref[pl.ds(r0, rows), cols] += _dot(a, wdb[:, cols])


def _ffn(h, w_gate, w_up, w_down, seg_tile, seg_expert, seg_row0, seg_rows):
    rows, d = h.shape
    _, _, dff = w_gate.shape
    bm, bf, sub = FFN_BM, FFN_BF, FFN_SUB
    nf = dff // bf
    nseg = seg_tile.shape[0]

    def f_eff(k, f, nr):
        return jnp.where(nr[k] > 0, f, nf - 1)

    grid_spec = pltpu.PrefetchScalarGridSpec(
        num_scalar_prefetch=4,
        grid=(nseg, nf),
        in_specs=[
            pl.BlockSpec((bm, d), lambda k, f, st, se, r0, nr: (st[k], 0)),
            pl.BlockSpec((1, d, bf), lambda k, f, st, se, r0, nr: (se[k], 0, f_eff(k, f, nr))),
            pl.BlockSpec((1, d, bf), lambda k, f, st, se, r0, nr: (se[k], 0, f_eff(k, f, nr))),
            pl.BlockSpec((1, bf, d), lambda k, f, st, se, r0, nr: (se[k], f_eff(k, f, nr), 0)),
        ],
        out_specs=pl.BlockSpec((bm, d), lambda k, f, st, se, r0, nr: (st[k], 0), pipeline_mode=pl.Buffered(1)),
        scratch_shapes=[
            pltpu.VMEM((d, bf), BF16),
            pltpu.VMEM((d, bf), BF16),
            pltpu.VMEM((bf, d), BF16),
        ],
    )
    return pl.pallas_call(
        functools.partial(_ffn_kernel, sub=sub, nsub=bm // sub, ncol=FFN_NCOL),
        grid_spec=grid_spec,
        out_shape=jax.ShapeDtypeStruct((rows, d), F32),
        compiler_params=_params("arbitrary", "arbitrary"),
        name="swiglu_ffn",
    )(seg_tile, seg_expert, seg_row0, seg_rows, h, w_gate, w_up, w_down)


def _sb_kernel(q_ref, k_ref, v_ref, tri_ref, tri2_ref, o_ref, *, bq, nh, scale):
    i = pl.program_id(2)
    dh = HEAD_DIM
    qs = [q_ref[:, h * dh:(h + 1) * dh] for h in range(nh)]

    def tile(k0, kwid, state, tri, shift):
        carry, acc = state
        ks = [k_ref[pl.ds(k0, kwid), h * dh:(h + 1) * dh] for h in range(nh)]
        vs = [v_ref[pl.ds(k0, kwid), h * dh:(h + 1) * dh] for h in range(nh)]
        z = jnp.concatenate([_dot_nt(qs[h], ks[h]) for h in range(nh)], axis=0) * scale
        log_beta = jnp.minimum(z, 0.0) - jnp.log(1.0 + jnp.exp(-jnp.abs(z)))
        log_keep = log_beta - z
        if shift is not None:
            row = lax.broadcasted_iota(jnp.int32, (bq, kwid), 0)
            col = lax.broadcasted_iota(jnp.int32, (bq, kwid), 1)
            causal = jnp.concatenate([jnp.where(col < row + shift, 1.0, 0.0)] * nh, axis=0) > 0.5
            log_keep = jnp.where(causal, log_keep, 0.0)
        hi, lo = _split_bf16(log_keep)
        between = _dot(jnp.concatenate([hi, lo], axis=1), tri[...]) + carry
        w = jnp.exp(log_beta + between)
        if shift is not None:
            w = jnp.where(causal, w, 0.0)
        wb = w.astype(BF16)
        pv = jnp.concatenate([_dot(wb[h * bq:(h + 1) * bq, :], vs[h]) for h in range(nh)], axis=0)
        return carry + jnp.sum(log_keep, axis=-1, keepdims=True), acc + pv

    state0 = (jnp.zeros((nh * bq, 1), F32), jnp.zeros((nh * bq, dh), F32))
    state = lax.cond(
        i == 0,
        lambda: tile(0, bq, state0, tri_ref, 0),
        lambda: tile(pl.multiple_of((i - 1) * bq, bq), 2 * bq, state0, tri2_ref, bq))
    n_left = jnp.maximum(i - 1, 0)

    def cond(c):
        return jnp.logical_and(c[0] < n_left, c[1] > SB_EXP_ZERO)

    def body(c):
        st = tile(pl.multiple_of((n_left - 1 - c[0]) * bq, bq), bq, c[2], tri_ref, None)
        return c[0] + 1, jnp.max(st[0]), st

    _, _, state = lax.while_loop(cond, body, (jnp.int32(0), jnp.max(state[0]), state))
    o_ref[...] = jnp.concatenate([state[1][h * bq:(h + 1) * bq, :] for h in range(nh)],
                                 axis=1).astype(o_ref.dtype)


def _sb_attention(qkv, batch, seq, heads):
    t = qkv.shape[0]
    bq = min(SB_BQ, seq)
    nq = seq // bq
    nh = SB_HEADS_PER_STEP
    hw = nh * HEAD_DIM
    ng = heads // nh
    def suffix_op(width):
        rj = lax.broadcasted_iota(jnp.int32, (2 * width, width), 0)
        cs = lax.broadcasted_iota(jnp.int32, (2 * width, width), 1)
        return jnp.where((rj % width) > cs, 1.0, 0.0).astype(BF16)

    return pl.pallas_call(
        functools.partial(_sb_kernel, bq=bq, nh=nh, scale=HEAD_DIM ** -0.5),
        grid=(batch, ng, nq),
        in_specs=[
            pl.BlockSpec((bq, hw), lambda b, h, i: (b * nq + i, h)),
            pl.BlockSpec((seq, hw), lambda b, h, i: (b, ng + h)),
            pl.BlockSpec((seq, hw), lambda b, h, i: (b, 2 * ng + h)),
            pl.BlockSpec((2 * bq, bq), lambda b, h, i: (0, 0)),
            pl.BlockSpec((4 * bq, 2 * bq), lambda b, h, i: (0, 0)),
        ],
        out_specs=pl.BlockSpec((bq, hw), lambda b, h, i: (b * nq + i, h)),
        out_shape=jax.ShapeDtypeStruct((t, heads * HEAD_DIM), BF16),
        compiler_params=_params("arbitrary", "arbitrary", "arbitrary"),
        name="stickbreak_attn",
    )(qkv, qkv, qkv, suffix_op(bq), suffix_op(2 * bq))


def _rglru_kernel(x_ref, g_ref, cw_ref, cb_ref, wa_ref, ba_ref, wx_ref, bx_ref, lam_ref, o_ref, xp_ref,
                  *, seq, cb, tc, kw):
    pad = SUBLANE
    xp_ref[0:pad, :] = jnp.zeros((pad, cb), F32)
    xp_ref[pad:, :] = x_ref[...]
    log_lam = _log_sigmoid(lam_ref[...])
    sub_iota = lax.broadcasted_iota(jnp.int32, (SUBLANE, cb), 0)
    ngroup = cb // LANE

    def chunk(ci, h):
        t0 = pl.multiple_of(ci * tc, tc)
        win = xp_ref[pl.ds(t0, tc + pad), :]
        xc = cb_ref[...] + cw_ref[0:1, :] * win[pad - kw + 1:pad - kw + 1 + tc, :]
        for k in range(1, kw):
            off = pad - kw + 1 + k
            xc = xc + cw_ref[k:k + 1, :] * win[off:off + tc, :]
        xcb = xc.astype(BF16)
        ra = jnp.concatenate(
            [_dot(xcb[:, q * LANE:(q + 1) * LANE], wa_ref[q].astype(BF16)) for q in range(ngroup)], axis=1)
        rx = jnp.concatenate(
            [_dot(xcb[:, q * LANE:(q + 1) * LANE], wx_ref[q].astype(BF16)) for q in range(ngroup)], axis=1)
        r = _sigmoid(ra + ba_ref[...])
        gi = _sigmoid(rx + bx_ref[...])
        log_a = RG_C * r * log_lam
        a = jnp.exp(log_a)
        u = jnp.sqrt(-jnp.tanh(log_a) * (a * a + 1.0)) * (gi * xc)
        gate = _gelu_tanh(g_ref[pl.ds(t0, tc), :])
        outs = []
        for gidx in range(tc // SUBLANE):
            av = a[gidx * SUBLANE:(gidx + 1) * SUBLANE, :]
            bv = u[gidx * SUBLANE:(gidx + 1) * SUBLANE, :]
            for sh in (1, 2, 4):
                a_s = pltpu.roll(av, sh, axis=0)
                b_s = pltpu.roll(bv, sh, axis=0)
                m = sub_iota >= sh
                bv = jnp.where(m, av * b_s + bv, bv)
                av = jnp.where(m, av * a_s, av)
            hv = av * h + bv
            outs.append(hv)
            h = jnp.broadcast_to(hv[SUBLANE - 1:SUBLANE, :], (SUBLANE, cb))
        hs = jnp.concatenate(outs, axis=0)
        o_ref[pl.ds(t0, tc), :] = (hs * gate).astype(o_ref.dtype)
        return h

    lax.fori_loop(0, seq // tc, chunk, jnp.zeros((SUBLANE, cb), F32))


def _block_diag_pairs(w):
    nblk, c, _ = w.shape
    w2 = w.reshape(nblk // 2, 2, c, c)
    z = jnp.zeros((nblk // 2, c, c), w.dtype)
    top = jnp.concatenate([w2[:, 0], z], axis=2)
    bot = jnp.concatenate([z, w2[:, 1]], axis=2)
    return jnp.concatenate([top, bot], axis=1)


def _rglru(rg, batch, seq, conv_w, conv_b, wa, ba, wx, bx, lam):
    t, c2 = rg.shape
    c = c2 // 2
    cb = min(RG_CB, c)
    tc = min(RG_TC, seq)
    ncb = c // cb
    kw = conv_w.shape[0]
    gpb = cb // LANE
    vec = pl.BlockSpec((1, cb), lambda b, j: (0, j))
    return pl.pallas_call(
        functools.partial(_rglru_kernel, seq=seq, cb=cb, tc=tc, kw=kw),
        grid=(batch, ncb),
        in_specs=[
            pl.BlockSpec((seq, cb), lambda b, j: (b, j)),
            pl.BlockSpec((seq, cb), lambda b, j: (b, ncb + j)),
            pl.BlockSpec((kw, cb), lambda b, j: (0, j)),
            vec,
            pl.BlockSpec((gpb, LANE, LANE), lambda b, j: (j, 0, 0)),
            vec,
            pl.BlockSpec((gpb, LANE, LANE), lambda b, j: (j, 0, 0)),
            vec,
            vec,
        ],
        out_specs=pl.BlockSpec((seq, cb), lambda b, j: (b, j)),
        out_shape=jax.ShapeDtypeStruct((t, c), BF16),
        scratch_shapes=[pltpu.VMEM((seq + SUBLANE, cb), F32)],
        compiler_params=_params("arbitrary", "arbitrary"),
        name="rglru",
    )(rg, rg, conv_w, conv_b.reshape(1, c), _block_diag_pairs(wa), ba.reshape(1, c),
      _block_diag_pairs(wx), bx.reshape(1, c), lam.reshape(1, c))


def _compress_kernel(x_ref, pos_ref, w1_ref, w2_ref, o_ref, *, ngrp):
    st = CMP_STRIDE
    half = st * HEAD_DIM
    xs = [x_ref[pl.ds(r, ngrp, stride=st), :] for r in range(st)]
    pos = pos_ref[0]
    x0 = jnp.concatenate([xs[r] + pos[r:r + 1, :] for r in range(st)], axis=1).astype(BF16)
    x1 = jnp.concatenate([xs[r] + pos[st + r:st + r + 1, :] for r in range(st)], axis=1).astype(BF16)
    p0 = _dot(x0, w1_ref[0, 0:half, :].astype(BF16))
    p1 = _dot(x1, w1_ref[0, half:2 * half, :].astype(BF16))
    pre = p0 + pltpu.roll(p1, ngrp - 1, axis=0)
    out = _dot(_gelu_tanh(pre).astype(BF16), w2_ref[0].astype(BF16))
    rown = lax.broadcasted_iota(jnp.int32, out.shape, 0)
    o_ref[...] = jnp.where(rown < ngrp - 1, out, 0.0).astype(o_ref.dtype)


def _compress(kcvc, batch, seq, pos, w1, w2):
    assert CMP_BLOCK == 2 * CMP_STRIDE
    g2 = kcvc.shape[1] // HEAD_DIM
    per = g2 // 2
    ngrp = seq // CMP_STRIDE
    return pl.pallas_call(
        functools.partial(_compress_kernel, ngrp=ngrp),
        grid=(batch, g2),
        in_specs=[
            pl.BlockSpec((seq, HEAD_DIM), lambda b, j: (b, j)),
            pl.BlockSpec((1, CMP_BLOCK, HEAD_DIM), lambda b, j: (j // per, 0, 0)),
            pl.BlockSpec((1, CMP_BLOCK * HEAD_DIM, HEAD_DIM), lambda b, j: (j // per, 0, 0)),
            pl.BlockSpec((1, HEAD_DIM, HEAD_DIM), lambda b, j: (j // per, 0, 0)),
        ],
        out_specs=pl.BlockSpec((ngrp, HEAD_DIM), lambda b, j: (b * g2 + j, 0)),
        out_shape=jax.ShapeDtypeStruct((batch * g2 * ngrp, HEAD_DIM), BF16),
        compiler_params=_params("arbitrary", "arbitrary"),
        name="nsa_compress",
    )(kcvc, pos, w1, w2)


def _softmax_parts(s2, bias):
    sb = s2 + bias
    m = jnp.max(sb, axis=-1, keepdims=True)
    e = jnp.exp2(sb - m)
    return e, jnp.sum(e, axis=-1, keepdims=True)


def _nsa_kernel(q_ref, kc_ref, vc_ref, ks_ref, vs_ref, kw_ref, vw_ref, g_ref, ov_ref, ex_ref, o_ref,
                acc_ref, inv_ref, *, seq, hg, ng, n_slc, scale):
    bq, bk, dh = NSA_BQ, NSA_BK, HEAD_DIM
    i = pl.program_id(2)
    q0 = i * bq
    rg = hg * bq
    qb = q_ref[...]
    qs = [jnp.concatenate([qb[:, (g * hg + h) * dh:(g * hg + h + 1) * dh] for h in range(hg)], axis=0)
          for g in range(ng)]
    kv = lambda ref, g: ref[:, g * dh:(g + 1) * dh]
    grp = lambda a, g: a[g * rg:(g + 1) * rg, :]
    stack = lambda parts: jnp.concatenate(parts, axis=0)
    tile_heads = lambda a: jnp.concatenate([a] * hg, axis=0)
    tile_rows = lambda a: jnp.concatenate([a] * (ng * hg), axis=0)
    qpos1 = q0 + lax.broadcasted_iota(jnp.int32, (bq, 1), 0)
    qpos = tile_rows(qpos1)

    scale2 = scale * LOG2E

    ncmp = kc_ref.shape[0] // ng
    s_c = stack([_dot_nt(qs[g], kc_ref[g * ncmp:(g + 1) * ncmp, :]) for g in range(ng)]) * scale2
    n_idx = lax.broadcasted_iota(jnp.int32, (1, ncmp), 1)
    bias_c = jnp.where(n_idx * CMP_STRIDE + (CMP_BLOCK - 1) <= qpos1, 0.0, NEG_BIG)
    e_c, l_c = _softmax_parts(s_c, tile_rows(bias_c))
    p_c = e_c * jnp.where(qpos >= CMP_BLOCK - 1, 1.0 / l_c, 0.0)
    p_cb = p_c.astype(BF16)
    o_c = stack([_dot(grp(p_cb, g), vc_ref[g * ncmp:(g + 1) * ncmp, :]) for g in range(ng)])

    overlap_t = ov_ref[...]
    blk = lax.broadcasted_iota(jnp.int32, (n_slc, bq), 0)
    qp_l = q0 + lax.broadcasted_iota(jnp.int32, (n_slc, bq), 1)
    cur = qp_l // SLC_BLOCK
    forced = (blk == 0) | (blk == cur) | (blk == cur - 1)
    valid = blk * SLC_BLOCK <= qp_l
    k_sel = min(N_SEL, n_slc)
    sels = []
    for g in range(ng):
        p_g = grp(p_c, g)
        p_sum = p_g[0:bq, :]
        for h in range(1, hg):
            p_sum = p_sum + p_g[h * bq:(h + 1) * bq, :]
        p_hi, p_lo = _split_bf16(p_sum)
        imp_t = (_dot_nt(overlap_t, p_hi) + _dot_nt(overlap_t, p_lo))[0:n_slc, :]
        rank = jnp.where(valid, imp_t + FORCE_BONUS * jnp.where(forced, 1.0, 0.0), -jnp.inf)
        ahead = jnp.zeros((n_slc, bq), F32)
        for jp in range(n_slc):
            other = rank[jp:jp + 1, :]
            beats = (other > rank) | ((other == rank) & (blk > jp))
            ahead = ahead + jnp.where(beats, 1.0, 0.0)
        sel_t = jnp.where(valid & (ahead < k_sel), 1.0, 0.0)
        sel_t = jnp.concatenate([sel_t, jnp.zeros((LANE - n_slc, bq), F32)], axis=0)
        sels.append(sel_t.T.astype(BF16))

    n_need = (q0 + bq + bk - 1) // bk
    for v in range(1, seq // bk + 1):
        nk = v * bk

        @pl.when(n_need == v)
        def _():
            s = stack([_dot_nt(qs[g], kv(ks_ref, g)[0:nk, :]) for g in range(ng)]) * scale2
            kpos = lax.broadcasted_iota(jnp.int32, (1, nk), 1)
            bias = stack([tile_heads(jnp.where((_dot(sels[g], ex_ref[:, 0:nk]) > 0.5) & (kpos <= qpos1),
                                               0.0, NEG_BIG)) for g in range(ng)])
            e_s, l_s = _softmax_parts(s, bias)
            e_sb = e_s.astype(BF16)
            acc_ref[...] = stack([_dot(grp(e_sb, g), kv(vs_ref, g)[0:nk, :]) for g in range(ng)])
            inv_ref[...] = 1.0 / l_s

    acc_s = acc_ref[...]
    inv_s = inv_ref[...]

    span = min(WINDOW + bq, seq)
    w0 = pl.multiple_of(jnp.maximum(jnp.minimum(q0 - WINDOW, seq - span), 0), bq)
    kwt = kw_ref[pl.ds(w0, span), :]
    vwt = vw_ref[pl.ds(w0, span), :]
    s_w = stack([_dot_nt(qs[g], kwt[:, g * dh:(g + 1) * dh]) for g in range(ng)]) * scale2
    kpos_w = w0 + lax.broadcasted_iota(jnp.int32, (1, span), 1)
    bias_w = jnp.where((kpos_w <= qpos1) & (kpos_w > qpos1 - WINDOW), 0.0, NEG_BIG)
    e_w, l_w = _softmax_parts(s_w, tile_rows(bias_w))
    e_wb = e_w.astype(BF16)
    acc_w = stack([_dot(grp(e_wb, g), vwt[:, g * dh:(g + 1) * dh]) for g in range(ng)])
    inv_w = 1.0 / l_w

    gates = _sigmoid(g_ref[...])
    outs = []
    for g in range(ng):
        for h in range(hg):
            r = slice(g * rg + h * bq, g * rg + (h + 1) * bq)
            c = g * LANE + h * N_BRANCH
            outs.append(gates[:, c:c + 1] * o_c[r, :] + (gates[:, c + 1:c + 2] * inv_s[r, :]) * acc_s[r, :]
                        + (gates[:, c + 2:c + 3] * inv_w[r, :]) * acc_w[r, :])
    o_ref[...] = jnp.concatenate(outs, axis=1).astype(o_ref.dtype)


def _nsa_attention(q, kv4, cmp, gate_logits, batch, seq):
    t, width = q.shape
    g = NSA_KV_HEADS
    hg = width // (g * HEAD_DIM)
    bq = NSA_BQ
    nb = seq // bq
    ncmp = seq // CMP_STRIDE
    n_slc = seq // SLC_BLOCK
    dh = HEAD_DIM
    ng = NSA_GROUPS_PER_STEP
    gs = g // ng
    full = lambda off: pl.BlockSpec((seq, ng * dh), lambda b, gi, i: (b, off * gs + gi))
    jj = lax.broadcasted_iota(jnp.int32, (LANE, ncmp), 0)
    cstart = lax.broadcasted_iota(jnp.int32, (LANE, ncmp), 1) * CMP_STRIDE
    overlap_t = jnp.where((cstart < (jj + 1) * SLC_BLOCK) & (cstart + CMP_BLOCK > jj * SLC_BLOCK)
                          & (jj < n_slc), 1.0, 0.0).astype(BF16)
    expand = jnp.where(lax.broadcasted_iota(jnp.int32, (LANE, seq), 1) // SLC_BLOCK
                       == lax.broadcasted_iota(jnp.int32, (LANE, seq), 0), 1.0, 0.0).astype(BF16)
    const = lambda shape: pl.BlockSpec(shape, lambda b, gi, i: (0, 0))
    return pl.pallas_call(
        functools.partial(_nsa_kernel, seq=seq, hg=hg, ng=ng, n_slc=n_slc, scale=dh ** -0.5),
        grid=(batch, gs, nb),
        in_specs=[
            pl.BlockSpec((bq, ng * hg * dh), lambda b, gi, i: (b * nb + i, gi)),
            pl.BlockSpec((ng * ncmp, dh), lambda b, gi, i: (b * 2 * gs + gi, 0)),
            pl.BlockSpec((ng * ncmp, dh), lambda b, gi, i: (b * 2 * gs + gs + gi, 0)),
            full(0), full(1), full(2), full(3),
            pl.BlockSpec((bq, ng * LANE), lambda b, gi, i: (b * nb + i, gi)),
            const((LANE, ncmp)),
            const((LANE, seq)),
        ],
        out_specs=pl.BlockSpec((bq, ng * hg * dh), lambda b, gi, i: (b * nb + i, gi)),
        out_shape=jax.ShapeDtypeStruct((t, width), BF16),
        scratch_shapes=[pltpu.VMEM((ng * hg * bq, dh), F32), pltpu.VMEM((ng * hg * bq, 1), F32)],
        compiler_params=_params("arbitrary", "arbitrary", "arbitrary"),
        name="nsa_attn",
    )(q, cmp, cmp, kv4, kv4, kv4, kv4, gate_logits, overlap_t, expand)


def _gather_kernel(lo_ref, hi_ref, src_ref, h_ref, o_ref, *, chunk):
    i = pl.program_id(0)
    rows = o_ref.shape[0]
    src = src_ref[...]
    lane = lax.broadcasted_iota(jnp.int32, (rows, chunk), 1)
    o_ref[...] = jnp.zeros(o_ref.shape, o_ref.dtype)

    def body(c, carry):
        c0 = pl.multiple_of(c * chunk, chunk)
        onehot = jnp.where(src - c0 == lane, 1.0, 0.0).astype(BF16)
        o_ref[...] += _dot(onehot, h_ref[pl.ds(c0, chunk), :]).astype(o_ref.dtype)
        return carry

    lax.fori_loop(lo_ref[i], hi_ref[i], body, 0)


def _gather_rows(h, src, n_rows):
    t, d = h.shape
    bt, chunk = GATHER_ROWS, GATHER_CHUNK
    nt = n_rows // bt
    src2 = src.reshape(nt, bt)
    live = src2 >= 0
    lo = jnp.min(jnp.where(live, src2, t), axis=1) // chunk
    hi = jnp.where(jnp.any(live, axis=1), jnp.max(src2, axis=1) // chunk + 1, lo)
    grid_spec = pltpu.PrefetchScalarGridSpec(
        num_scalar_prefetch=2,
        grid=(nt,),
        in_specs=[
            pl.BlockSpec((bt, 1), lambda i, lo, hi: (i, 0)),
            pl.BlockSpec((t, d), lambda i, lo, hi: (0, 0), pipeline_mode=pl.Buffered(1)),
        ],
        out_specs=pl.BlockSpec((bt, d), lambda i, lo, hi: (i, 0)),
    )
    return pl.pallas_call(
        functools.partial(_gather_kernel, chunk=chunk),
        grid_spec=grid_spec,
        out_shape=jax.ShapeDtypeStruct((n_rows, d), h.dtype),
        compiler_params=_params("arbitrary"),
        name="moe_dispatch",
    )(jnp.minimum(lo, hi).astype(jnp.int32), hi.astype(jnp.int32), src.reshape(n_rows, 1), h)


def _router_kernel(l_ref, o_ref, *, n_exp):
    x = l_ref[...]
    lane = lax.broadcasted_iota(jnp.int32, x.shape, 1)
    xm = jnp.where(lane < n_exp, x, -jnp.inf)
    v0 = jnp.max(xm, axis=-1, keepdims=True)
    i0 = jnp.min(jnp.where(xm == v0, lane, LANE), axis=-1, keepdims=True)
    xm = jnp.where(lane == i0, -jnp.inf, xm)
    v1 = jnp.max(xm, axis=-1, keepdims=True)
    i1 = jnp.min(jnp.where(xm == v1, lane, LANE), axis=-1, keepdims=True)
    e = jnp.exp(v1 - v0)
    p0 = 1.0 / (1.0 + e)
    out = jnp.where(lane == 0, p0, jnp.where(lane == 1, e * p0, jnp.where(
        lane == 2, i0.astype(F32), jnp.where(lane == 3, i1.astype(F32), 0.0))))
    o_ref[...] = out


def _router_top2(logits, n_exp):
    t = logits.shape[0]
    bt = ROW_TILE
    out = pl.pallas_call(
        functools.partial(_router_kernel, n_exp=n_exp),
        grid=(t // bt,),
        in_specs=[pl.BlockSpec((bt, LANE), lambda i: (i, 0))],
        out_specs=pl.BlockSpec((bt, LANE), lambda i: (i, 0)),
        out_shape=jax.ShapeDtypeStruct((t, LANE), F32),
        compiler_params=_params("arbitrary"),
        name="router_top2",
    )(logits)
    return out[:, 0:TOP_K], out[:, TOP_K:2 * TOP_K].astype(jnp.int32)


def _route(probs, top_i, n_experts):
    t = top_i.shape[0]
    flat_e = top_i.reshape(-1)
    onehot = (flat_e[:, None] == jnp.arange(n_experts)[None, :]).astype(jnp.int32)
    rank = jnp.take_along_axis(jnp.cumsum(onehot, axis=0), flat_e[:, None], axis=1)[:, 0] - 1
    counts = jnp.sum(onehot, axis=0)
    padded = (counts + FFN_SUB - 1) // FFN_SUB * FFN_SUB
    end = jnp.cumsum(padded)
    start = end - padded
    n_rows = -(-(TOP_K * t + n_experts * (FFN_SUB - 1)) // FFN_BM) * FFN_BM
    n_tiles = n_rows // FFN_BM
    dest = (start[flat_e] + rank).astype(jnp.int32)
    src = jnp.full((n_rows,), -1, jnp.int32).at[dest].set(jnp.arange(TOP_K * t, dtype=jnp.int32) // TOP_K)
    cuts = jnp.sort(jnp.concatenate([start, jnp.arange(n_tiles) * FFN_BM]))
    nxt = jnp.concatenate([cuts[1:], jnp.full((1,), n_rows, cuts.dtype)])
    seg_rows = jnp.maximum(jnp.minimum(nxt, end[-1]) - cuts, 0)
    seg_tile = jnp.minimum(cuts // FFN_BM, n_tiles - 1)
    seg_expert = jnp.minimum(jnp.searchsorted(end, cuts, side="right"), n_experts - 1)
    i32 = lambda a: a.astype(jnp.int32)
    return probs, dest.reshape(t, TOP_K), src, i32(seg_tile), i32(seg_expert), i32(cuts % FFN_BM), i32(seg_rows)


def kernel(x, c, ada_mix_w, ada_mix_b, norm_mix_g, ada_ffn_w, ada_ffn_b, norm_ffn_g, even_in_w, rg_conv_w, rg_conv_b, rg_wa, rg_ba, rg_wx, rg_bx, rg_lambda, even_out_w, dense_w_gate, dense_w_up, dense_w_down, nsa_in_w, cmp_pos_k, cmp_pos_v, cmp_k_w1, cmp_k_w2, cmp_v_w1, cmp_v_w2, nsa_out_w, router_w, moe_w_gate, moe_w_up, moe_w_down, final_norm_g):
    batch, seq, d = x.shape
    t = batch * seq
    depth = ada_mix_w.shape[0]
    xf = x.reshape(t, d)

    c_pad = jnp.pad(c, ((0, (-batch) % SUBLANE), (0, 0)))
    m_mix = _adaln(c_pad, ada_mix_w, ada_mix_b)
    m_ffn = _adaln(c_pad, ada_ffn_w, ada_ffn_b)

    def mods(m, layer):
        v = m[layer, :batch].reshape(batch, 1, 3, d)
        return v[:, :, 0], v[:, :, 1], v[:, :, 2]

    def norm_step(xf, pending, g, mod, out_dtype=BF16):
        if pending is not None and pending[0] == "experts":
            _, y, dest, probs, gate = pending
            return _combine_norm(xf, y, dest, probs, gate, g, seq, mod=mod, out_dtype=out_dtype)
        res = None if pending is None else pending[1:]
        return _normmod(xf, g, seq, res=res, mod=mod, out_dtype=out_dtype)

    pending = None
    for layer in range(depth):
        j = layer // 2
        shift, scale, gate = mods(m_mix, layer)
        xf, h = norm_step(xf, pending, norm_mix_g[layer], (shift, scale))
        if layer % 2 == 0:
            w_in = even_in_w[j]
            sbw = (w_in.shape[1] - 2 * rg_conv_w.shape[2]) // 3
            heads = sbw // HEAD_DIM
            qkv = _matmul([h], w_in, 0, 3 * sbw, BF16)
            rg = _matmul([h], w_in, 3 * sbw, w_in.shape[1] - 3 * sbw, F32)
            o_a = _sb_attention(qkv, batch, seq, heads)
            o_b = _rglru(rg, batch, seq, rg_conv_w[j], rg_conv_b[j], rg_wa[j], rg_ba[j], rg_wx[j],
                         rg_bx[j], rg_lambda[j])
            xf = _matmul([o_a, o_b], even_out_w[j], 0, d, F32, residual=(xf, gate, seq))
        else:
            w_in = nsa_in_w[j]
            g = NSA_KV_HEADS
            kvw = g * HEAD_DIM
            nsa_w = nsa_out_w.shape[1]
            hg = nsa_w // kvw
            q = _matmul([h], w_in, 0, nsa_w, BF16)
            kcvc = _matmul([h], w_in, nsa_w, 2 * kvw, F32)
            kv4 = _matmul([h], w_in, nsa_w + 2 * kvw, 4 * kvw, BF16)
            wg = w_in[:, nsa_w + 6 * kvw:].reshape(d, g, hg * N_BRANCH)
            wg = jnp.pad(wg, ((0, 0), (0, 0), (0, LANE - hg * N_BRANCH))).reshape(d, g * LANE)
            gl = _matmul([h], wg, 0, g * LANE, F32)
            cmp = _compress(kcvc, batch, seq, jnp.stack([cmp_pos_k[j], cmp_pos_v[j]]),
                            jnp.stack([cmp_k_w1[j], cmp_v_w1[j]]), jnp.stack([cmp_k_w2[j], cmp_v_w2[j]]))
            o = _nsa_attention(q, kv4, cmp, gl, batch, seq)
            xf = _matmul([o], nsa_out_w[j], 0, d, F32, residual=(xf, gate, seq))
        pending = None

        shift, scale, gate = mods(m_ffn, layer)
        xf, h = norm_step(xf, pending, norm_ffn_g[layer], (shift, scale))
        if layer % 2 == 0:
            nt = t // FFN_BM
            ffn = _ffn(h, dense_w_gate[j:j + 1], dense_w_up[j:j + 1], dense_w_down[j:j + 1],
                       jnp.arange(nt, dtype=jnp.int32), jnp.zeros((nt,), jnp.int32),
                       jnp.zeros((nt,), jnp.int32), jnp.full((nt,), FFN_BM, jnp.int32))
            pending = ("dense", ffn, gate)
        else:
            n_exp = router_w.shape[2]
            rw = jnp.pad(router_w[j], ((0, 0), (0, LANE - n_exp)))
            probs, top_i = _router_top2(_matmul([h], rw, 0, LANE, F32), n_exp)
            probs, dest, src, seg_tile, seg_expert, seg_row0, seg_rows = _route(probs, top_i, n_exp)
            h_sorted = _gather_rows(h, src, src.shape[0])
            y = _ffn(h_sorted, moe_w_gate[j], moe_w_up[j], moe_w_down[j], seg_tile, seg_expert, seg_row0,
                     seg_rows)
            pending = ("experts", y, dest, probs, gate)

    _, out = norm_step(xf, pending, final_norm_g, None, F32)
    return out.reshape(batch, seq, d)
```

```python
import functools

import jax
import jax.numpy as jnp
from jax import lax
from jax.experimental import pallas as pl
from jax.experimental.pallas import tpu as pltpu

F32 = jnp.float32
BF16 = jnp.bfloat16

LANE = 128
SUBLANE = 8
VMEM_LIMIT_BYTES = 58 * 1024 * 1024

HEAD_DIM = 128
EPS = 1e-6
RG_C = 8.0
NSA_KV_HEADS = 4
N_BRANCH = 3
CMP_BLOCK = 32
CMP_STRIDE = 16
SLC_BLOCK = 64
N_SEL = 8
WINDOW = 512
FORCE_BONUS = 1e6
TOP_K = 2
NEG_BIG = -1e30
LOG2E = 1.4426950408889634
SB_EXP_ZERO = -110.0

ROW_TILE = 512
COMBINE_ROWS = 256
MM_BM = 2048
MM_BN = 512
FFN_BM = 2048
FFN_SUB = 128
FFN_NCOL = 4
FFN_BF = 256
GATHER_ROWS = 256
GATHER_CHUNK = 512
SB_BQ = 256
SB_HEADS_PER_STEP = 4
NSA_BQ = 128
NSA_BK = 256
NSA_GROUPS_PER_STEP = 2
RG_CB = 256
RG_TC = 256


def _params(*sem):
    return pltpu.CompilerParams(dimension_semantics=sem, vmem_limit_bytes=VMEM_LIMIT_BYTES)


def _dot(a, b):
    return jnp.dot(a, b, preferred_element_type=F32)


def _dot_nt(a, b):
    return lax.dot_general(a, b, (((1,), (1,)), ((), ())), preferred_element_type=F32)


def _sigmoid(x):
    return 1.0 / (1.0 + jnp.exp(-x))


def _log_sigmoid(x):
    return jnp.minimum(x, 0.0) - jnp.log1p(jnp.exp(-jnp.abs(x)))


def _gelu_tanh(x):
    return 0.5 * x * (1.0 + jnp.tanh(0.7978845608028654 * (x + 0.044715 * (x * x * x))))


def _split_bf16(x):
    hi = x.astype(BF16)
    lo = (x - hi.astype(F32)).astype(BF16)
    return hi, lo


def _adaln_kernel(c_ref, w_ref, b_ref, o_ref):
    c = c_ref[...]
    s = (c * _sigmoid(c)).astype(BF16)
    o_ref[0] = _dot(s, w_ref[0].astype(BF16)) + b_ref[0]


def _adaln(c_pad, w, b, bn=1024):
    depth, d, n3 = w.shape
    rows = c_pad.shape[0]
    return pl.pallas_call(
        _adaln_kernel,
        grid=(depth, n3 // bn),
        in_specs=[
            pl.BlockSpec((rows, d), lambda l, j: (0, 0)),
            pl.BlockSpec((1, d, bn), lambda l, j: (l, 0, j)),
            pl.BlockSpec((1, 1, bn), lambda l, j: (l, 0, j)),
        ],
        out_specs=pl.BlockSpec((1, rows, bn), lambda l, j: (l, 0, j)),
        out_shape=jax.ShapeDtypeStruct((depth, rows, n3), F32),
        compiler_params=_params("arbitrary", "arbitrary"),
        name="adaln",
    )(c_pad, w, b.reshape(depth, 1, n3))


def _normmod_kernel(*refs, has_res, has_mod):
    it = iter(refs)
    x_ref = next(it)
    if has_res:
        y_ref, gate_ref = next(it), next(it)
    g_ref = next(it)
    if has_mod:
        shift_ref, scale_ref = next(it), next(it)
    if has_res:
        xo_ref = next(it)
    h_ref = next(it)

    x = x_ref[...]
    if has_res:
        x = x + gate_ref[0] * y_ref[...]
        xo_ref[...] = x
    ms = jnp.mean(x * x, axis=-1, keepdims=True)
    h = x * lax.rsqrt(ms + EPS) * g_ref[...]
    if has_mod:
        h = h * (1.0 + scale_ref[0]) + shift_ref[0]
    h_ref[...] = h.astype(h_ref.dtype)


def _normmod(x, g, seq, res=None, mod=None, out_dtype=BF16):
    t, d = x.shape
    bt = ROW_TILE
    per_b = seq // bt
    row = pl.BlockSpec((bt, d), lambda i: (i, 0))
    vec = pl.BlockSpec((1, 1, d), lambda i: (i // per_b, 0, 0))
    args, specs = [x], [row]
    if res is not None:
        args += [res[0], res[1]]
        specs += [row, vec]
    args.append(g.reshape(1, d))
    specs.append(pl.BlockSpec((1, d), lambda i: (0, 0)))
    if mod is not None:
        args += [mod[0], mod[1]]
        specs += [vec, vec]
    out_shape, out_specs = [], []
    if res is not None:
        out_shape.append(jax.ShapeDtypeStruct((t, d), F32))
        out_specs.append(row)
    out_shape.append(jax.ShapeDtypeStruct((t, d), out_dtype))
    out_specs.append(row)
    outs = pl.pallas_call(
        functools.partial(_normmod_kernel, has_res=res is not None, has_mod=mod is not None),
        grid=(t // bt,),
        in_specs=specs,
        out_specs=out_specs,
        out_shape=out_shape,
        compiler_params=_params("arbitrary"),
        name="normmod",
    )(*args)
    return outs if res is not None else (x, outs[0])


def _combine_norm_kernel(dest_ref, *refs, has_mod, bt, n_steps):
    it = iter(refs)
    x_ref, y_hbm, p_ref, gate_ref, g_ref = (next(it) for _ in range(5))
    if has_mod:
        shift_ref, scale_ref, xo_ref = next(it), next(it), next(it)
    h_ref, buf, sem = next(it), next(it), next(it)
    i = pl.program_id(0)

    def row_copy(src_row, slot, k, r):
        return pltpu.make_async_copy(y_hbm.at[pl.ds(src_row, 1), :], buf.at[slot, k, pl.ds(r, 1), :],
                                     sem.at[slot])

    def start_tile(tile, slot):
        base = tile * (bt * TOP_K)

        def body(r, carry):
            for k in range(TOP_K):
                row_copy(dest_ref[base + r * TOP_K + k], slot, k, r).start()
            return carry

        lax.fori_loop(0, bt, body, 0, unroll=8)

    @pl.when(i == 0)
    def _():
        start_tile(0, 0)

    @pl.when(i + 1 < n_steps)
    def _():
        start_tile(i + 1, (i + 1) % 2)

    slot = i % 2

    def wait_row(r, carry):
        for k in range(TOP_K):
            row_copy(0, slot, k, r).wait()
        return carry

    lax.fori_loop(0, bt, wait_row, 0, unroll=8)
    p = p_ref[...]
    y = p[:, 0:1] * buf[slot, 0]
    for k in range(1, TOP_K):
        y = y + p[:, k:k + 1] * buf[slot, k]
    x = x_ref[...] + gate_ref[0] * y
    ms = jnp.mean(x * x, axis=-1, keepdims=True)
    h = x * lax.rsqrt(ms + EPS) * g_ref[...]
    if has_mod:
        xo_ref[...] = x
        h = h * (1.0 + scale_ref[0]) + shift_ref[0]
    h_ref[...] = h.astype(h_ref.dtype)


def _combine_norm(x, y, dest, probs, gate, g, seq, mod=None, out_dtype=BF16):
    t, d = x.shape
    bt = COMBINE_ROWS
    n_steps = t // bt
    per_b = seq // bt
    row = pl.BlockSpec((bt, d), lambda i, dst: (i, 0))
    vec = pl.BlockSpec((1, 1, d), lambda i, dst: (i // per_b, 0, 0))
    args = [x, y, probs, gate, g.reshape(1, d)]
    specs = [row, pl.BlockSpec(memory_space=pl.ANY), pl.BlockSpec((bt, TOP_K), lambda i, dst: (i, 0)), vec,
             pl.BlockSpec((1, d), lambda i, dst: (0, 0))]
    out_shape, out_specs = [], []
    if mod is not None:
        args += [mod[0], mod[1]]
        specs += [vec, vec]
        out_shape.append(jax.ShapeDtypeStruct((t, d), F32))
        out_specs.append(row)
    out_shape.append(jax.ShapeDtypeStruct((t, d), out_dtype))
    out_specs.append(row)
    grid_spec = pltpu.PrefetchScalarGridSpec(
        num_scalar_prefetch=1,
        grid=(n_steps,),
        in_specs=specs,
        out_specs=out_specs,
        scratch_shapes=[pltpu.VMEM((2, TOP_K, bt, d), F32), pltpu.SemaphoreType.DMA((2,))],
    )
    outs = pl.pallas_call(
        functools.partial(_combine_norm_kernel, has_mod=mod is not None, bt=bt, n_steps=n_steps),
        grid_spec=grid_spec,
        out_shape=out_shape,
        compiler_params=_params("arbitrary"),
        name="moe_combine_norm",
    )(dest.reshape(-1), *args)
    return (outs[0], outs[1]) if mod is not None else (None, outs[0])


def _mm_kernel(*refs, ks, has_res):
    a_refs = refs[: len(ks)]
    rest = refs[len(ks):]
    w_ref = rest[0]
    if has_res:
        x_ref, gate_ref = rest[1], rest[2]
    o_ref, wb_ref = rest[-2], rest[-1]

    wb_ref[...] = w_ref[...].astype(BF16)

    acc = None
    k0 = 0
    for a_ref, k in zip(a_refs, ks):
        part = _dot(a_ref[...], wb_ref[k0:k0 + k, :])
        acc = part if acc is None else acc + part
        k0 += k
    if has_res:
        acc = x_ref[...] + gate_ref[0] * acc
    o_ref[...] = acc.astype(o_ref.dtype)


def _matmul(a_list, w, col0, n, out_dtype, residual=None):
    m = a_list[0].shape[0]
    ks = tuple(a.shape[1] for a in a_list)
    ktot = sum(ks)
    bm = min(MM_BM, m)
    bn = min(MM_BN, n)
    cb0 = col0 // bn
    args = list(a_list) + [w]
    specs = [pl.BlockSpec((bm, k), lambda j, i: (i, 0)) for k in ks]
    specs.append(pl.BlockSpec((ktot, bn), lambda j, i: (0, cb0 + j)))
    if residual is not None:
        x, gate, seq = residual
        bm = min(bm, seq)
        per_b = seq // bm
        specs[: len(ks)] = [pl.BlockSpec((bm, k), lambda j, i: (i, 0)) for k in ks]
        args += [x, gate]
        specs += [pl.BlockSpec((bm, bn), lambda j, i: (i, j)),
                  pl.BlockSpec((1, 1, bn), lambda j, i: (i // per_b, 0, j))]
    return pl.pallas_call(
        functools.partial(_mm_kernel, ks=ks, has_res=residual is not None),
        grid=(n // bn, m // bm),
        in_specs=specs,
        out_specs=pl.BlockSpec((bm, bn), lambda j, i: (i, j)),
        out_shape=jax.ShapeDtypeStruct((m, n), out_dtype),
        scratch_shapes=[pltpu.VMEM((ktot, bn), BF16)],
        compiler_params=_params("arbitrary", "arbitrary"),
        name="proj_matmul",
    )(*args)


def _ffn_kernel(st_ref, se_ref, r0_ref, nr_ref, h_ref, wg_ref, wu_ref, wd_ref, o_ref, wgb, wub, wdb,
                *, sub, nsub, ncol):
    del se_ref
    k = pl.program_id(0)
    f = pl.program_id(1)
    nr = nr_ref[k]
    r0 = pl.multiple_of(r0_ref[k], sub)
    bm, d = o_ref.shape
    nchunk = (nr + sub - 1) // sub
    first = jnp.logical_or(k == 0, st_ref[k] != st_ref[jnp.maximum(k - 1, 0)])

    @pl.when(jnp.logical_and(f == 0, first))
    def _():
        o_ref[...] = jnp.zeros((bm, d), F32)

    for c in range(1, nsub + 1):
        rows = c * sub

        @pl.when(nchunk == c)
        def _():
            h = h_ref[pl.ds(r0, rows), :]
            wgb[...] = wg_ref[0].astype(BF16)
            g = _dot(h, wgb[...])
            wub[...] = wu_ref[0].astype(BF16)
            u = _dot(h, wub[...])
            wdb[...] = wd_ref[0].astype(BF16)
            a = (g * _sigmoid(g) * u).astype(BF16)
            for n in range(ncol):
                cols = slice(n * (d // ncol), (n + 1) * (d // ncol))
                o_ref[pl.ds(r0, rows), cols] += _dot(a, wdb[:, cols])


def _ffn(h, w_gate, w_up, w_down, seg_tile, seg_expert, seg_row0, seg_rows):
    rows, d = h.shape
    _, _, dff = w_gate.shape
    bm, bf, sub = FFN_BM, FFN_BF, FFN_SUB
    nf = dff // bf
    nseg = seg_tile.shape[0]

    def f_eff(k, f, nr):
        return jnp.where(nr[k] > 0, f, nf - 1)

    grid_spec = pltpu.PrefetchScalarGridSpec(
        num_scalar_prefetch=4,
        grid=(nseg, nf),
        in_specs=[
            pl.BlockSpec((bm, d), lambda k, f, st, se, r0, nr: (st[k], 0)),
            pl.BlockSpec((1, d, bf), lambda k, f, st, se, r0, nr: (se[k], 0, f_eff(k, f, nr))),
            pl.BlockSpec((1, d, bf), lambda k, f, st, se, r0, nr: (se[k], 0, f_eff(k, f, nr))),
            pl.BlockSpec((1, bf, d), lambda k, f, st, se, r0, nr: (se[k], f_eff(k, f, nr), 0)),
        ],
        out_specs=pl.BlockSpec((bm, d), lambda k, f, st, se, r0, nr: (st[k], 0), pipeline_mode=pl.Buffered(1)),
        scratch_shapes=[
            pltpu.VMEM((d, bf), BF16),
            pltpu.VMEM((d, bf), BF16),
            pltpu.VMEM((bf, d), BF16),
        ],
    )
    return pl.pallas_call(
        functools.partial(_ffn_kernel, sub=sub, nsub=bm // sub, ncol=FFN_NCOL),
        grid_spec=grid_spec,
        out_shape=jax.ShapeDtypeStruct((rows, d), F32),
        compiler_params=_params("arbitrary", "arbitrary"),
        name="swiglu_ffn",
    )(seg_tile, seg_expert, seg_row0, seg_rows, h, w_gate, w_up, w_down)


def _sb_kernel(q_ref, k_ref, v_ref, tri_ref, o_ref, *, bq, nh, scale):
    i = pl.program_id(2)
    dh = HEAD_DIM
    qs = [q_ref[:, h * dh:(h + 1) * dh] for h in range(nh)]

    def tile(j, state, diag):
        carry, acc = state
        k0 = pl.multiple_of(j * bq, bq)
        ks = [k_ref[pl.ds(k0, bq), h * dh:(h + 1) * dh] for h in range(nh)]
        vs = [v_ref[pl.ds(k0, bq), h * dh:(h + 1) * dh] for h in range(nh)]
        z = jnp.concatenate([_dot_nt(qs[h], ks[h]) for h in range(nh)], axis=0) * scale
        log_beta = jnp.minimum(z, 0.0) - jnp.log(1.0 + jnp.exp(-jnp.abs(z)))
        log_keep = log_beta - z
        if diag:
            row = lax.broadcasted_iota(jnp.int32, (bq, bq), 0)
            col = lax.broadcasted_iota(jnp.int32, (bq, bq), 1)
            causal = jnp.concatenate([jnp.where(col < row, 1.0, 0.0)] * nh, axis=0) > 0.5
            log_keep = jnp.where(causal, log_keep, 0.0)
        hi, lo = _split_bf16(log_keep)
        between = _dot(jnp.concatenate([hi, lo], axis=1), tri_ref[...]) + carry
        w = jnp.exp(log_beta + between)
        if diag:
            w = jnp.where(causal, w, 0.0)
        wb = w.astype(BF16)
        pv = jnp.concatenate([_dot(wb[h * bq:(h + 1) * bq, :], vs[h]) for h in range(nh)], axis=0)
        return carry + jnp.sum(log_keep, axis=-1, keepdims=True), acc + pv

    state = tile(i, (jnp.zeros((nh * bq, 1), F32), jnp.zeros((nh * bq, dh), F32)), True)

    def cond(c):
        return jnp.logical_and(c[0] < i, c[1] > SB_EXP_ZERO)

    def body(c):
        st = tile(i - 1 - c[0], c[2], False)
        return c[0] + 1, jnp.max(st[0]), st

    _, _, state = lax.while_loop(cond, body, (jnp.int32(0), jnp.max(state[0]), state))
    o_ref[...] = jnp.concatenate([state[1][h * bq:(h + 1) * bq, :] for h in range(nh)],
                                 axis=1).astype(o_ref.dtype)


def _sb_attention(qkv, batch, seq, heads):
    t = qkv.shape[0]
    bq = min(SB_BQ, seq)
    nq = seq // bq
    nh = SB_HEADS_PER_STEP
    hw = nh * HEAD_DIM
    ng = heads // nh
    rj = lax.broadcasted_iota(jnp.int32, (2 * bq, bq), 0)
    cs = lax.broadcasted_iota(jnp.int32, (2 * bq, bq), 1)
    tri = jnp.where((rj % bq) > cs, 1.0, 0.0).astype(BF16)
    return pl.pallas_call(
        functools.partial(_sb_kernel, bq=bq, nh=nh, scale=HEAD_DIM ** -0.5),
        grid=(batch, ng, nq),
        in_specs=[
            pl.BlockSpec((bq, hw), lambda b, h, i: (b * nq + i, h)),
            pl.BlockSpec((seq, hw), lambda b, h, i: (b, ng + h)),
            pl.BlockSpec((seq, hw), lambda b, h, i: (b, 2 * ng + h)),
            pl.BlockSpec((2 * bq, bq), lambda b, h, i: (0, 0)),
        ],
        out_specs=pl.BlockSpec((bq, hw), lambda b, h, i: (b * nq + i, h)),
        out_shape=jax.ShapeDtypeStruct((t, heads * HEAD_DIM), BF16),
        compiler_params=_params("arbitrary", "arbitrary", "arbitrary"),
        name="stickbreak_attn",
    )(qkv, qkv, qkv, tri)


def _rglru_kernel(x_ref, g_ref, cw_ref, cb_ref, wa_ref, ba_ref, wx_ref, bx_ref, lam_ref, o_ref, xp_ref,
                  *, seq, cb, tc, kw):
    pad = SUBLANE
    xp_ref[0:pad, :] = jnp.zeros((pad, cb), F32)
    xp_ref[pad:, :] = x_ref[...]
    log_lam = _log_sigmoid(lam_ref[...])
    sub_iota = lax.broadcasted_iota(jnp.int32, (SUBLANE, cb), 0)
    ngroup = cb // LANE

    def chunk(ci, h):
        t0 = pl.multiple_of(ci * tc, tc)
        win = xp_ref[pl.ds(t0, tc + pad), :]
        xc = cb_ref[...] + cw_ref[0:1, :] * win[pad - kw + 1:pad - kw + 1 + tc, :]
        for k in range(1, kw):
            off = pad - kw + 1 + k
            xc = xc + cw_ref[k:k + 1, :] * win[off:off + tc, :]
        xcb = xc.astype(BF16)
        ra = jnp.concatenate(
            [_dot(xcb[:, q * LANE:(q + 1) * LANE], wa_ref[q].astype(BF16)) for q in range(ngroup)], axis=1)
        rx = jnp.concatenate(
            [_dot(xcb[:, q * LANE:(q + 1) * LANE], wx_ref[q].astype(BF16)) for q in range(ngroup)], axis=1)
        r = _sigmoid(ra + ba_ref[...])
        gi = _sigmoid(rx + bx_ref[...])
        log_a = RG_C * r * log_lam
        a = jnp.exp(log_a)
        u = jnp.sqrt(-jnp.tanh(log_a) * (a * a + 1.0)) * (gi * xc)
        gate = _gelu_tanh(g_ref[pl.ds(t0, tc), :])
        outs = []
        for gidx in range(tc // SUBLANE):
            av = a[gidx * SUBLANE:(gidx + 1) * SUBLANE, :]
            bv = u[gidx * SUBLANE:(gidx + 1) * SUBLANE, :]
            for sh in (1, 2, 4):
                a_s = pltpu.roll(av, sh, axis=0)
                b_s = pltpu.roll(bv, sh, axis=0)
                m = sub_iota >= sh
                bv = jnp.where(m, av * b_s + bv, bv)
                av = jnp.where(m, av * a_s, av)
            hv = av * h + bv
            outs.append(hv)
            h = jnp.broadcast_to(hv[SUBLANE - 1:SUBLANE, :], (SUBLANE, cb))
        hs = jnp.concatenate(outs, axis=0)
        o_ref[pl.ds(t0, tc), :] = (hs * gate).astype(o_ref.dtype)
        return h

    lax.fori_loop(0, seq // tc, chunk, jnp.zeros((SUBLANE, cb), F32))


def _block_diag_pairs(w):
    nblk, c, _ = w.shape
    w2 = w.reshape(nblk // 2, 2, c, c)
    z = jnp.zeros((nblk // 2, c, c), w.dtype)
    top = jnp.concatenate([w2[:, 0], z], axis=2)
    bot = jnp.concatenate([z, w2[:, 1]], axis=2)
    return jnp.concatenate([top, bot], axis=1)


def _rglru(rg, batch, seq, conv_w, conv_b, wa, ba, wx, bx, lam):
    t, c2 = rg.shape
    c = c2 // 2
    cb = min(RG_CB, c)
    tc = min(RG_TC, seq)
    ncb = c // cb
    kw = conv_w.shape[0]
    gpb = cb // LANE
    vec = pl.BlockSpec((1, cb), lambda b, j: (0, j))
    return pl.pallas_call(
        functools.partial(_rglru_kernel, seq=seq, cb=cb, tc=tc, kw=kw),
        grid=(batch, ncb),
        in_specs=[
            pl.BlockSpec((seq, cb), lambda b, j: (b, j)),
            pl.BlockSpec((seq, cb), lambda b, j: (b, ncb + j)),
            pl.BlockSpec((kw, cb), lambda b, j: (0, j)),
            vec,
            pl.BlockSpec((gpb, LANE, LANE), lambda b, j: (j, 0, 0)),
            vec,
            pl.BlockSpec((gpb, LANE, LANE), lambda b, j: (j, 0, 0)),
            vec,
            vec,
        ],
        out_specs=pl.BlockSpec((seq, cb), lambda b, j: (b, j)),
        out_shape=jax.ShapeDtypeStruct((t, c), BF16),
        scratch_shapes=[pltpu.VMEM((seq + SUBLANE, cb), F32)],
        compiler_params=_params("arbitrary", "arbitrary"),
        name="rglru",
    )(rg, rg, conv_w, conv_b.reshape(1, c), _block_diag_pairs(wa), ba.reshape(1, c),
      _block_diag_pairs(wx), bx.reshape(1, c), lam.reshape(1, c))


def _compress_kernel(x_ref, pos_ref, w1_ref, w2_ref, o_ref, *, ngrp):
    st = CMP_STRIDE
    half = st * HEAD_DIM
    xs = [x_ref[pl.ds(r, ngrp, stride=st), :] for r in range(st)]
    pos = pos_ref[0]
    x0 = jnp.concatenate([xs[r] + pos[r:r + 1, :] for r in range(st)], axis=1).astype(BF16)
    x1 = jnp.concatenate([xs[r] + pos[st + r:st + r + 1, :] for r in range(st)], axis=1).astype(BF16)
    p0 = _dot(x0, w1_ref[0, 0:half, :].astype(BF16))
    p1 = _dot(x1, w1_ref[0, half:2 * half, :].astype(BF16))
    pre = p0 + pltpu.roll(p1, ngrp - 1, axis=0)
    out = _dot(_gelu_tanh(pre).astype(BF16), w2_ref[0].astype(BF16))
    rown = lax.broadcasted_iota(jnp.int32, out.shape, 0)
    o_ref[...] = jnp.where(rown < ngrp - 1, out, 0.0).astype(o_ref.dtype)


def _compress(kcvc, batch, seq, pos, w1, w2):
    assert CMP_BLOCK == 2 * CMP_STRIDE
    g2 = kcvc.shape[1] // HEAD_DIM
    per = g2 // 2
    ngrp = seq // CMP_STRIDE
    return pl.pallas_call(
        functools.partial(_compress_kernel, ngrp=ngrp),
        grid=(batch, g2),
        in_specs=[
            pl.BlockSpec((seq, HEAD_DIM), lambda b, j: (b, j)),
            pl.BlockSpec((1, CMP_BLOCK, HEAD_DIM), lambda b, j: (j // per, 0, 0)),
            pl.BlockSpec((1, CMP_BLOCK * HEAD_DIM, HEAD_DIM), lambda b, j: (j // per, 0, 0)),
            pl.BlockSpec((1, HEAD_DIM, HEAD_DIM), lambda b, j: (j // per, 0, 0)),
        ],
        out_specs=pl.BlockSpec((ngrp, HEAD_DIM), lambda b, j: (b * g2 + j, 0)),
        out_shape=jax.ShapeDtypeStruct((batch * g2 * ngrp, HEAD_DIM), BF16),
        compiler_params=_params("arbitrary", "arbitrary"),
        name="nsa_compress",
    )(kcvc, pos, w1, w2)


def _softmax_parts(s2, bias):
    sb = s2 + bias
    m = jnp.max(sb, axis=-1, keepdims=True)
    e = jnp.exp2(sb - m)
    return e, jnp.sum(e, axis=-1, keepdims=True)


def _nsa_kernel(q_ref, kc_ref, vc_ref, ks_ref, vs_ref, kw_ref, vw_ref, g_ref, ov_ref, ex_ref, o_ref,
                acc_ref, inv_ref, *, seq, hg, ng, n_slc, scale):
    bq, bk, dh = NSA_BQ, NSA_BK, HEAD_DIM
    i = pl.program_id(2)
    q0 = i * bq
    rg = hg * bq
    qb = q_ref[...]
    qs = [jnp.concatenate([qb[:, (g * hg + h) * dh:(g * hg + h + 1) * dh] for h in range(hg)], axis=0)
          for g in range(ng)]
    kv = lambda ref, g: ref[:, g * dh:(g + 1) * dh]
    grp = lambda a, g: a[g * rg:(g + 1) * rg, :]
    stack = lambda parts: jnp.concatenate(parts, axis=0)
    tile_heads = lambda a: jnp.concatenate([a] * hg, axis=0)
    tile_rows = lambda a: jnp.concatenate([a] * (ng * hg), axis=0)
    qpos1 = q0 + lax.broadcasted_iota(jnp.int32, (bq, 1), 0)
    qpos = tile_rows(qpos1)

    scale2 = scale * LOG2E

    ncmp = kc_ref.shape[0] // ng
    s_c = stack([_dot_nt(qs[g], kc_ref[g * ncmp:(g + 1) * ncmp, :]) for g in range(ng)]) * scale2
    n_idx = lax.broadcasted_iota(jnp.int32, (1, ncmp), 1)
    bias_c = jnp.where(n_idx * CMP_STRIDE + (CMP_BLOCK - 1) <= qpos1, 0.0, NEG_BIG)
    e_c, l_c = _softmax_parts(s_c, tile_rows(bias_c))
    p_c = e_c * jnp.where(qpos >= CMP_BLOCK - 1, 1.0 / l_c, 0.0)
    p_cb = p_c.astype(BF16)
    o_c = stack([_dot(grp(p_cb, g), vc_ref[g * ncmp:(g + 1) * ncmp, :]) for g in range(ng)])

    overlap_t = ov_ref[...]
    blk = lax.broadcasted_iota(jnp.int32, (n_slc, bq), 0)
    qp_l = q0 + lax.broadcasted_iota(jnp.int32, (n_slc, bq), 1)
    cur = qp_l // SLC_BLOCK
    forced = (blk == 0) | (blk == cur) | (blk == cur - 1)
    valid = blk * SLC_BLOCK <= qp_l
    k_sel = min(N_SEL, n_slc)
    sels = []
    for g in range(ng):
        p_g = grp(p_c, g)
        p_sum = p_g[0:bq, :]
        for h in range(1, hg):
            p_sum = p_sum + p_g[h * bq:(h + 1) * bq, :]
        p_hi, p_lo = _split_bf16(p_sum)
        imp_t = (_dot_nt(overlap_t, p_hi) + _dot_nt(overlap_t, p_lo))[0:n_slc, :]
        rank = jnp.where(valid, imp_t + FORCE_BONUS * jnp.where(forced, 1.0, 0.0), -jnp.inf)
        ahead = jnp.zeros((n_slc, bq), F32)
        for jp in range(n_slc):
            other = rank[jp:jp + 1, :]
            beats = (other > rank) | ((other == rank) & (blk > jp))
            ahead = ahead + jnp.where(beats, 1.0, 0.0)
        sel_t = jnp.where(valid & (ahead < k_sel), 1.0, 0.0)
        sel_t = jnp.concatenate([sel_t, jnp.zeros((LANE - n_slc, bq), F32)], axis=0)
        sels.append(sel_t.T.astype(BF16))

    n_need = (q0 + bq + bk - 1) // bk
    for v in range(1, seq // bk + 1):
        nk = v * bk

        @pl.when(n_need == v)
        def _():
            s = stack([_dot_nt(qs[g], kv(ks_ref, g)[0:nk, :]) for g in range(ng)]) * scale2
            kpos = lax.broadcasted_iota(jnp.int32, (1, nk), 1)
            bias = stack([tile_heads(jnp.where((_dot(sels[g], ex_ref[:, 0:nk]) > 0.5) & (kpos <= qpos1),
                                               0.0, NEG_BIG)) for g in range(ng)])
            e_s, l_s = _softmax_parts(s, bias)
            e_sb = e_s.astype(BF16)
            acc_ref[...] = stack([_dot(grp(e_sb, g), kv(vs_ref, g)[0:nk, :]) for g in range(ng)])
            inv_ref[...] = 1.0 / l_s

    acc_s = acc_ref[...]
    inv_s = inv_ref[...]

    span = min(WINDOW + bq, seq)
    w0 = pl.multiple_of(jnp.maximum(jnp.minimum(q0 - WINDOW, seq - span), 0), bq)
    kwt = kw_ref[pl.ds(w0, span), :]
    vwt = vw_ref[pl.ds(w0, span), :]
    s_w = stack([_dot_nt(qs[g], kwt[:, g * dh:(g + 1) * dh]) for g in range(ng)]) * scale2
    kpos_w = w0 + lax.broadcasted_iota(jnp.int32, (1, span), 1)
    bias_w = jnp.where((kpos_w <= qpos1) & (kpos_w > qpos1 - WINDOW), 0.0, NEG_BIG)
    e_w, l_w = _softmax_parts(s_w, tile_rows(bias_w))
    e_wb = e_w.astype(BF16)
    acc_w = stack([_dot(grp(e_wb, g), vwt[:, g * dh:(g + 1) * dh]) for g in range(ng)])
    inv_w = 1.0 / l_w

    gates = _sigmoid(g_ref[...])
    outs = []
    for g in range(ng):
        for h in range(hg):
            r = slice(g * rg + h * bq, g * rg + (h + 1) * bq)
            c = g * LANE + h * N_BRANCH
            outs.append(gates[:, c:c + 1] * o_c[r, :] + (gates[:, c + 1:c + 2] * inv_s[r, :]) * acc_s[r, :]
                        + (gates[:, c + 2:c + 3] * inv_w[r, :]) * acc_w[r, :])
    o_ref[...] = jnp.concatenate(outs, axis=1).astype(o_ref.dtype)


def _nsa_attention(q, kv4, cmp, gate_logits, batch, seq):
    t, width = q.shape
    g = NSA_KV_HEADS
    hg = width // (g * HEAD_DIM)
    bq = NSA_BQ
    nb = seq // bq
    ncmp = seq // CMP_STRIDE
    n_slc = seq // SLC_BLOCK
    dh = HEAD_DIM
    ng = NSA_GROUPS_PER_STEP
    gs = g // ng
    full = lambda off: pl.BlockSpec((seq, ng * dh), lambda b, gi, i: (b, off * gs + gi))
    jj = lax.broadcasted_iota(jnp.int32, (LANE, ncmp), 0)
    cstart = lax.broadcasted_iota(jnp.int32, (LANE, ncmp), 1) * CMP_STRIDE
    overlap_t = jnp.where((cstart < (jj + 1) * SLC_BLOCK) & (cstart + CMP_BLOCK > jj * SLC_BLOCK)
                          & (jj < n_slc), 1.0, 0.0).astype(BF16)
    expand = jnp.where(lax.broadcasted_iota(jnp.int32, (LANE, seq), 1) // SLC_BLOCK
                       == lax.broadcasted_iota(jnp.int32, (LANE, seq), 0), 1.0, 0.0).astype(BF16)
    const = lambda shape: pl.BlockSpec(shape, lambda b, gi, i: (0, 0))
    return pl.pallas_call(
        functools.partial(_nsa_kernel, seq=seq, hg=hg, ng=ng, n_slc=n_slc, scale=dh ** -0.5),
        grid=(batch, gs, nb),
        in_specs=[
            pl.BlockSpec((bq, ng * hg * dh), lambda b, gi, i: (b * nb + i, gi)),
            pl.BlockSpec((ng * ncmp, dh), lambda b, gi, i: (b * 2 * gs + gi, 0)),
            pl.BlockSpec((ng * ncmp, dh), lambda b, gi, i: (b * 2 * gs + gs + gi, 0)),
            full(0), full(1), full(2), full(3),
            pl.BlockSpec((bq, ng * LANE), lambda b, gi, i: (b * nb + i, gi)),
            const((LANE, ncmp)),
            const((LANE, seq)),
        ],
        out_specs=pl.BlockSpec((bq, ng * hg * dh), lambda b, gi, i: (b * nb + i, gi)),
        out_shape=jax.ShapeDtypeStruct((t, width), BF16),
        scratch_shapes=[pltpu.VMEM((ng * hg * bq, dh), F32), pltpu.VMEM((ng * hg * bq, 1), F32)],
        compiler_params=_params("arbitrary", "arbitrary", "arbitrary"),
        name="nsa_attn",
    )(q, cmp, cmp, kv4, kv4, kv4, kv4, gate_logits, overlap_t, expand)


def _gather_kernel(lo_ref, hi_ref, src_ref, h_ref, o_ref, *, chunk):
    i = pl.program_id(0)
    rows = o_ref.shape[0]
    src = src_ref[...]
    lane = lax.broadcasted_iota(jnp.int32, (rows, chunk), 1)
    o_ref[...] = jnp.zeros(o_ref.shape, o_ref.dtype)

    def body(c, carry):
        c0 = pl.multiple_of(c * chunk, chunk)
        onehot = jnp.where(src - c0 == lane, 1.0, 0.0).astype(BF16)
        o_ref[...] += _dot(onehot, h_ref[pl.ds(c0, chunk), :]).astype(o_ref.dtype)
        return carry

    lax.fori_loop(lo_ref[i], hi_ref[i], body, 0)


def _gather_rows(h, src, n_rows):
    t, d = h.shape
    bt, chunk = GATHER_ROWS, GATHER_CHUNK
    nt = n_rows // bt
    src2 = src.reshape(nt, bt)
    live = src2 >= 0
    lo = jnp.min(jnp.where(live, src2, t), axis=1) // chunk
    hi = jnp.where(jnp.any(live, axis=1), jnp.max(src2, axis=1) // chunk + 1, lo)
    grid_spec = pltpu.PrefetchScalarGridSpec(
        num_scalar_prefetch=2,
        grid=(nt,),
        in_specs=[
            pl.BlockSpec((bt, 1), lambda i, lo, hi: (i, 0)),
            pl.BlockSpec((t, d), lambda i, lo, hi: (0, 0), pipeline_mode=pl.Buffered(1)),
        ],
        out_specs=pl.BlockSpec((bt, d), lambda i, lo, hi: (i, 0)),
    )
    return pl.pallas_call(
        functools.partial(_gather_kernel, chunk=chunk),
        grid_spec=grid_spec,
        out_shape=jax.ShapeDtypeStruct((n_rows, d), h.dtype),
        compiler_params=_params("arbitrary"),
        name="moe_dispatch",
    )(jnp.minimum(lo, hi).astype(jnp.int32), hi.astype(jnp.int32), src.reshape(n_rows, 1), h)


def _router_kernel(l_ref, o_ref, *, n_exp):
    x = l_ref[...]
    lane = lax.broadcasted_iota(jnp.int32, x.shape, 1)
    xm = jnp.where(lane < n_exp, x, -jnp.inf)
    v0 = jnp.max(xm, axis=-1, keepdims=True)
    i0 = jnp.min(jnp.where(xm == v0, lane, LANE), axis=-1, keepdims=True)
    xm = jnp.where(lane == i0, -jnp.inf, xm)
    v1 = jnp.max(xm, axis=-1, keepdims=True)
    i1 = jnp.min(jnp.where(xm == v1, lane, LANE), axis=-1, keepdims=True)
    e = jnp.exp(v1 - v0)
    p0 = 1.0 / (1.0 + e)
    out = jnp.where(lane == 0, p0, jnp.where(lane == 1, e * p0, jnp.where(
        lane == 2, i0.astype(F32), jnp.where(lane == 3, i1.astype(F32), 0.0))))
    o_ref[...] = out


def _router_top2(logits, n_exp):
    t = logits.shape[0]
    bt = ROW_TILE
    out = pl.pallas_call(
        functools.partial(_router_kernel, n_exp=n_exp),
        grid=(t // bt,),
        in_specs=[pl.BlockSpec((bt, LANE), lambda i: (i, 0))],
        out_specs=pl.BlockSpec((bt, LANE), lambda i: (i, 0)),
        out_shape=jax.ShapeDtypeStruct((t, LANE), F32),
        compiler_params=_params("arbitrary"),
        name="router_top2",
    )(logits)
    return out[:, 0:TOP_K], out[:, TOP_K:2 * TOP_K].astype(jnp.int32)


def _route(probs, top_i, n_experts):
    t = top_i.shape[0]
    flat_e = top_i.reshape(-1)
    onehot = (flat_e[:, None] == jnp.arange(n_experts)[None, :]).astype(jnp.int32)
    rank = jnp.take_along_axis(jnp.cumsum(onehot, axis=0), flat_e[:, None], axis=1)[:, 0] - 1
    counts = jnp.sum(onehot, axis=0)
    padded = (counts + FFN_SUB - 1) // FFN_SUB * FFN_SUB
    end = jnp.cumsum(padded)
    start = end - padded
    n_rows = -(-(TOP_K * t + n_experts * (FFN_SUB - 1)) // FFN_BM) * FFN_BM
    n_tiles = n_rows // FFN_BM
    dest = (start[flat_e] + rank).astype(jnp.int32)
    src = jnp.full((n_rows,), -1, jnp.int32).at[dest].set(jnp.arange(TOP_K * t, dtype=jnp.int32) // TOP_K)
    cuts = jnp.sort(jnp.concatenate([start, jnp.arange(n_tiles) * FFN_BM]))
    nxt = jnp.concatenate([cuts[1:], jnp.full((1,), n_rows, cuts.dtype)])
    seg_rows = jnp.maximum(jnp.minimum(nxt, end[-1]) - cuts, 0)
    seg_tile = jnp.minimum(cuts // FFN_BM, n_tiles - 1)
    seg_expert = jnp.minimum(jnp.searchsorted(end, cuts, side="right"), n_experts - 1)
    i32 = lambda a: a.astype(jnp.int32)
    return probs, dest.reshape(t, TOP_K), src, i32(seg_tile), i32(seg_expert), i32(cuts % FFN_BM), i32(seg_rows)


def kernel(x, c, ada_mix_w, ada_mix_b, norm_mix_g, ada_ffn_w, ada_ffn_b, norm_ffn_g, even_in_w, rg_conv_w, rg_conv_b, rg_wa, rg_ba, rg_wx, rg_bx, rg_lambda, even_out_w, dense_w_gate, dense_w_up, dense_w_down, nsa_in_w, cmp_pos_k, cmp_pos_v, cmp_k_w1, cmp_k_w2, cmp_v_w1, cmp_v_w2, nsa_out_w, router_w, moe_w_gate, moe_w_up, moe_w_down, final_norm_g):
    batch, seq, d = x.shape
    t = batch * seq
    depth = ada_mix_w.shape[0]
    xf = x.reshape(t, d)

    c_pad = jnp.pad(c, ((0, (-batch) % SUBLANE), (0, 0)))
    m_mix = _adaln(c_pad, ada_mix_w, ada_mix_b)
    m_ffn = _adaln(c_pad, ada_ffn_w, ada_ffn_b)

    def mods(m, layer):
        v = m[layer, :batch].reshape(batch, 1, 3, d)
        return v[:, :, 0], v[:, :, 1], v[:, :, 2]

    def norm_step(xf, pending, g, mod, out_dtype=BF16):
        if pending is not None and pending[0] == "experts":
            _, y, dest, probs, gate = pending
            return _combine_norm(xf, y, dest, probs, gate, g, seq, mod=mod, out_dtype=out_dtype)
        res = None if pending is None else pending[1:]
        return _normmod(xf, g, seq, res=res, mod=mod, out_dtype=out_dtype)

    pending = None
    for layer in range(depth):
        j = layer // 2
        shift, scale, gate = mods(m_mix, layer)
        xf, h = norm_step(xf, pending, norm_mix_g[layer], (shift, scale))
        if layer % 2 == 0:
            w_in = even_in_w[j]
            sbw = (w_in.shape[1] - 2 * rg_conv_w.shape[2]) // 3
            heads = sbw // HEAD_DIM
            qkv = _matmul([h], w_in, 0, 3 * sbw, BF16)
            rg = _matmul([h], w_in, 3 * sbw, w_in.shape[1] - 3 * sbw, F32)
            o_a = _sb_attention(qkv, batch, seq, heads)
            o_b = _rglru(rg, batch, seq, rg_conv_w[j], rg_conv_b[j], rg_wa[j], rg_ba[j], rg_wx[j],
                         rg_bx[j], rg_lambda[j])
            xf = _matmul([o_a, o_b], even_out_w[j], 0, d, F32, residual=(xf, gate, seq))
        else:
            w_in = nsa_in_w[j]
            g = NSA_KV_HEADS
            kvw = g * HEAD_DIM
            nsa_w = nsa_out_w.shape[1]
            hg = nsa_w // kvw
            q = _matmul([h], w_in, 0, nsa_w, BF16)
            kcvc = _matmul([h], w_in, nsa_w, 2 * kvw, F32)
            kv4 = _matmul([h], w_in, nsa_w + 2 * kvw, 4 * kvw, BF16)
            wg = w_in[:, nsa_w + 6 * kvw:].reshape(d, g, hg * N_BRANCH)
            wg = jnp.pad(wg, ((0, 0), (0, 0), (0, LANE - hg * N_BRANCH))).reshape(d, g * LANE)
            gl = _matmul([h], wg, 0, g * LANE, F32)
            cmp = _compress(kcvc, batch, seq, jnp.stack([cmp_pos_k[j], cmp_pos_v[j]]),
                            jnp.stack([cmp_k_w1[j], cmp_v_w1[j]]), jnp.stack([cmp_k_w2[j], cmp_v_w2[j]]))
            o = _nsa_attention(q, kv4, cmp, gl, batch, seq)
            xf = _matmul([o], nsa_out_w[j], 0, d, F32, residual=(xf, gate, seq))
        pending = None

        shift, scale, gate = mods(m_ffn, layer)
        xf, h = norm_step(xf, pending, norm_ffn_g[layer], (shift, scale))
        if layer % 2 == 0:
            nt = t // FFN_BM
            ffn = _ffn(h, dense_w_gate[j:j + 1], dense_w_up[j:j + 1], dense_w_down[j:j + 1],
                       jnp.arange(nt, dtype=jnp.int32), jnp.zeros((nt,), jnp.int32),
                       jnp.zeros((nt,), jnp.int32), jnp.full((nt,), FFN_BM, jnp.int32))
            pending = ("dense", ffn, gate)
        else:
            n_exp = router_w.shape[2]
            rw = jnp.pad(router_w[j], ((0, 0), (0, LANE - n_exp)))
            probs, top_i = _router_top2(_matmul([h], rw, 0, LANE, F32), n_exp)
            probs, dest, src, seg_tile, seg_expert, seg_row0, seg_rows = _route(probs, top_i, n_exp)
            h_sorted = _gather_rows(h, src, src.shape[0])
            y = _ffn(h_sorted, moe_w_gate[j], moe_w_up[j], moe_w_down[j], seg_tile, seg_expert, seg_row0,
                     seg_rows)
            pending = ("experts", y, dest, probs, gate)

    _, out = norm_step(xf, pending, final_norm_g, None, F32)
    return out.reshape(batch, seq, d)
```

```python
import functools

import jax
import jax.numpy as jnp
from jax import lax
from jax.experimental import pallas as pl
from jax.experimental.pallas import tpu as pltpu

F32 = jnp.float32
BF16 = jnp.bfloat16

LANE = 128
SUBLANE = 8
VMEM_LIMIT_BYTES = 58 * 1024 * 1024

HEAD_DIM = 128
EPS = 1e-6
RG_C = 8.0
NSA_KV_HEADS = 4
N_BRANCH = 3
CMP_BLOCK = 32
CMP_STRIDE = 16
SLC_BLOCK = 64
N_SEL = 8
WINDOW = 512
FORCE_BONUS = 1e6
TOP_K = 2
NEG_BIG = -1e30
LOG2E = 1.4426950408889634
SB_EXP_ZERO = -110.0

ROW_TILE = 512
COMBINE_ROWS = 256
MM_BM = 2048
MM_BN = 512
FFN_BM = 2048
FFN_SUB = 256
FFN_NCOL = 4
FFN_BF = 256
GATHER_ROWS = 256
GATHER_CHUNK = 512
SB_BQ = 256
SB_HEADS_PER_STEP = 4
NSA_BQ = 128
NSA_BK = 256
NSA_GROUPS_PER_STEP = 2
RG_CB = 256
RG_TC = 256


def _params(*sem):
    return pltpu.CompilerParams(dimension_semantics=sem, vmem_limit_bytes=VMEM_LIMIT_BYTES)


def _dot(a, b):
    return jnp.dot(a, b, preferred_element_type=F32)


def _dot_nt(a, b):
    return lax.dot_general(a, b, (((1,), (1,)), ((), ())), preferred_element_type=F32)


def _sigmoid(x):
    return 1.0 / (1.0 + jnp.exp(-x))


def _log_sigmoid(x):
    return jnp.minimum(x, 0.0) - jnp.log1p(jnp.exp(-jnp.abs(x)))


def _gelu_tanh(x):
    return 0.5 * x * (1.0 + jnp.tanh(0.7978845608028654 * (x + 0.044715 * (x * x * x))))


def _split_bf16(x):
    hi = x.astype(BF16)
    lo = (x - hi.astype(F32)).astype(BF16)
    return hi, lo


def _adaln_kernel(c_ref, w_ref, b_ref, o_ref):
    c = c_ref[...]
    s = (c * _sigmoid(c)).astype(BF16)
    o_ref[0] = _dot(s, w_ref[0].astype(BF16)) + b_ref[0]


def _adaln(c_pad, w, b, bn=1024):
    depth, d, n3 = w.shape
    rows = c_pad.shape[0]
    return pl.pallas_call(
        _adaln_kernel,
        grid=(depth, n3 // bn),
        in_specs=[
            pl.BlockSpec((rows, d), lambda l, j: (0, 0)),
            pl.BlockSpec((1, d, bn), lambda l, j: (l, 0, j)),
            pl.BlockSpec((1, 1, bn), lambda l, j: (l, 0, j)),
        ],
        out_specs=pl.BlockSpec((1, rows, bn), lambda l, j: (l, 0, j)),
        out_shape=jax.ShapeDtypeStruct((depth, rows, n3), F32),
        compiler_params=_params("arbitrary", "arbitrary"),
        name="adaln",
    )(c_pad, w, b.reshape(depth, 1, n3))


def _normmod_kernel(*refs, has_res, has_mod):
    it = iter(refs)
    x_ref = next(it)
    if has_res:
        y_ref, gate_ref = next(it), next(it)
    g_ref = next(it)
    if has_mod:
        shift_ref, scale_ref = next(it), next(it)
    if has_res:
        xo_ref = next(it)
    h_ref = next(it)

    x = x_ref[...]
    if has_res:
        x = x + gate_ref[0] * y_ref[...]
        xo_ref[...] = x
    ms = jnp.mean(x * x, axis=-1, keepdims=True)
    h = x * lax.rsqrt(ms + EPS) * g_ref[...]
    if has_mod:
        h = h * (1.0 + scale_ref[0]) + shift_ref[0]
    h_ref[...] = h.astype(h_ref.dtype)


def _normmod(x, g, seq, res=None, mod=None, out_dtype=BF16):
    t, d = x.shape
    bt = ROW_TILE
    per_b = seq // bt
    row = pl.BlockSpec((bt, d), lambda i: (i, 0))
    vec = pl.BlockSpec((1, 1, d), lambda i: (i // per_b, 0, 0))
    args, specs = [x], [row]
    if res is not None:
        args += [res[0], res[1]]
        specs += [row, vec]
    args.append(g.reshape(1, d))
    specs.append(pl.BlockSpec((1, d), lambda i: (0, 0)))
    if mod is not None:
        args += [mod[0], mod[1]]
        specs += [vec, vec]
    out_shape, out_specs = [], []
    if res is not None:
        out_shape.append(jax.ShapeDtypeStruct((t, d), F32))
        out_specs.append(row)
    out_shape.append(jax.ShapeDtypeStruct((t, d), out_dtype))
    out_specs.append(row)
    outs = pl.pallas_call(
        functools.partial(_normmod_kernel, has_res=res is not None, has_mod=mod is not None),
        grid=(t // bt,),
        in_specs=specs,
        out_specs=out_specs,
        out_shape=out_shape,
        compiler_params=_params("arbitrary"),
        name="normmod",
    )(*args)
    return outs if res is not None else (x, outs[0])


def _combine_norm_kernel(dest_ref, *refs, has_mod, bt, n_steps):
    it = iter(refs)
    x_ref, y_hbm, p_ref, gate_ref, g_ref = (next(it) for _ in range(5))
    if has_mod:
        shift_ref, scale_ref, xo_ref = next(it), next(it), next(it)
    h_ref, buf, sem = next(it), next(it), next(it)
    i = pl.program_id(0)

    def row_copy(src_row, slot, k, r):
        return pltpu.make_async_copy(y_hbm.at[pl.ds(src_row, 1), :], buf.at[slot, k, pl.ds(r, 1), :],
                                     sem.at[slot])

    def start_tile(tile, slot):
        base = tile * (bt * TOP_K)

        def body(r, carry):
            for k in range(TOP_K):
                row_copy(dest_ref[base + r * TOP_K + k], slot, k, r).start()
            return carry

        lax.fori_loop(0, bt, body, 0, unroll=8)

    @pl.when(i == 0)
    def _():
        start_tile(0, 0)

    @pl.when(i + 1 < n_steps)
    def _():
        start_tile(i + 1, (i + 1) % 2)

    slot = i % 2

    def wait_row(r, carry):
        for k in range(TOP_K):
            row_copy(0, slot, k, r).wait()
        return carry

    lax.fori_loop(0, bt, wait_row, 0, unroll=8)
    p = p_ref[...]
    y = p[:, 0:1] * buf[slot, 0]
    for k in range(1, TOP_K):
        y = y + p[:, k:k + 1] * buf[slot, k]
    x = x_ref[...] + gate_ref[0] * y
    ms = jnp.mean(x * x, axis=-1, keepdims=True)
    h = x * lax.rsqrt(ms + EPS) * g_ref[...]
    if has_mod:
        xo_ref[...] = x
        h = h * (1.0 + scale_ref[0]) + shift_ref[0]
    h_ref[...] = h.astype(h_ref.dtype)


def _combine_norm(x, y, dest, probs, gate, g, seq, mod=None, out_dtype=BF16):
    t, d = x.shape
    bt = COMBINE_ROWS
    n_steps = t // bt
    per_b = seq // bt
    row = pl.BlockSpec((bt, d), lambda i, dst: (i, 0))
    vec = pl.BlockSpec((1, 1, d), lambda i, dst: (i // per_b, 0, 0))
    args = [x, y, probs, gate, g.reshape(1, d)]
    specs = [row, pl.BlockSpec(memory_space=pl.ANY), pl.BlockSpec((bt, TOP_K), lambda i, dst: (i, 0)), vec,
             pl.BlockSpec((1, d), lambda i, dst: (0, 0))]
    out_shape, out_specs = [], []
    if mod is not None:
        args += [mod[0], mod[1]]
        specs += [vec, vec]
        out_shape.append(jax.ShapeDtypeStruct((t, d), F32))
        out_specs.append(row)
    out_shape.append(jax.ShapeDtypeStruct((t, d), out_dtype))
    out_specs.append(row)
    grid_spec = pltpu.PrefetchScalarGridSpec(
        num_scalar_prefetch=1,
        grid=(n_steps,),
        in_specs=specs,
        out_specs=out_specs,
        scratch_shapes=[pltpu.VMEM((2, TOP_K, bt, d), F32), pltpu.SemaphoreType.DMA((2,))],
    )
    outs = pl.pallas_call(
        functools.partial(_combine_norm_kernel, has_mod=mod is not None, bt=bt, n_steps=n_steps),
        grid_spec=grid_spec,
        out_shape=out_shape,
        compiler_params=_params("arbitrary"),
        name="moe_combine_norm",
    )(dest.reshape(-1), *args)
    return (outs[0], outs[1]) if mod is not None else (None, outs[0])


def _mm_kernel(*refs, ks, has_res):
    a_refs = refs[: len(ks)]
    rest = refs[len(ks):]
    w_ref = rest[0]
    if has_res:
        x_ref, gate_ref = rest[1], rest[2]
    o_ref, wb_ref = rest[-2], rest[-1]

    wb_ref[...] = w_ref[...].astype(BF16)

    acc = None
    k0 = 0
    for a_ref, k in zip(a_refs, ks):
        part = _dot(a_ref[...], wb_ref[k0:k0 + k, :])
        acc = part if acc is None else acc + part
        k0 += k
    if has_res:
        acc = x_ref[...] + gate_ref[0] * acc
    o_ref[...] = acc.astype(o_ref.dtype)


def _matmul(a_list, w, col0, n, out_dtype, residual=None):
    m = a_list[0].shape[0]
    ks = tuple(a.shape[1] for a in a_list)
    ktot = sum(ks)
    bm = min(MM_BM, m)
    bn = min(MM_BN, n)
    cb0 = col0 // bn
    args = list(a_list) + [w]
    specs = [pl.BlockSpec((bm, k), lambda j, i: (i, 0)) for k in ks]
    specs.append(pl.BlockSpec((ktot, bn), lambda j, i: (0, cb0 + j)))
    if residual is not None:
        x, gate, seq = residual
        bm = min(bm, seq)
        per_b = seq // bm
        specs[: len(ks)] = [pl.BlockSpec((bm, k), lambda j, i: (i, 0)) for k in ks]
        args += [x, gate]
        specs += [pl.BlockSpec((bm, bn), lambda j, i: (i, j)),
                  pl.BlockSpec((1, 1, bn), lambda j, i: (i // per_b, 0, j))]
    return pl.pallas_call(
        functools.partial(_mm_kernel, ks=ks, has_res=residual is not None),
        grid=(n // bn, m // bm),
        in_specs=specs,
        out_specs=pl.BlockSpec((bm, bn), lambda j, i: (i, j)),
        out_shape=jax.ShapeDtypeStruct((m, n), out_dtype),
        scratch_shapes=[pltpu.VMEM((ktot, bn), BF16)],
        compiler_params=_params("arbitrary", "arbitrary"),
        name="proj_matmul",
    )(*args)


def _ffn_kernel(st_ref, se_ref, r0_ref, nr_ref, h_ref, wg_ref, wu_ref, wd_ref, o_ref, wgb, wub, wdb,
                *, sub, nsub, ncol):
    del se_ref
    k = pl.program_id(0)
    f = pl.program_id(1)
    nr = nr_ref[k]
    r0 = pl.multiple_of(r0_ref[k], sub)
    bm, d = o_ref.shape
    nchunk = (nr + sub - 1) // sub
    first = jnp.logical_or(k == 0, st_ref[k] != st_ref[jnp.maximum(k - 1, 0)])

    @pl.when(jnp.logical_and(f == 0, first))
    def _():
        o_ref[...] = jnp.zeros((bm, d), F32)

    for c in range(1, nsub + 1):
        rows = c * sub

        @pl.when(nchunk == c)
        def _():
            h = h_ref[pl.ds(r0, rows), :]
            wgb[...] = wg_ref[0].astype(BF16)
            g = _dot(h, wgb[...])
            wub[...] = wu_ref[0].astype(BF16)
            u = _dot(h, wub[...])
            wdb[...] = wd_ref[0].astype(BF16)
            a = (g * _sigmoid(g) * u).astype(BF16)
            for n in range(ncol):
                cols = slice(n * (d // ncol), (n + 1) * (d // ncol))
                o_ref[pl.ds(r0, rows), cols] += _dot(a, wdb[:, cols])


def _ffn(h, w_gate, w_up, w_down, seg_tile, seg_expert, seg_row0, seg_rows):
    rows, d = h.shape
    _, _, dff = w_gate.shape
    bm, bf, sub = FFN_BM, FFN_BF, FFN_SUB
    nf = dff // bf
    nseg = seg_tile.shape[0]

    def f_eff(k, f, nr):
        return jnp.where(nr[k] > 0, f, nf - 1)

    grid_spec = pltpu.PrefetchScalarGridSpec(
        num_scalar_prefetch=4,
        grid=(nseg, nf),
        in_specs=[
            pl.BlockSpec((bm, d), lambda k, f, st, se, r0, nr: (st[k], 0)),
            pl.BlockSpec((1, d, bf), lambda k, f, st, se, r0, nr: (se[k], 0, f_eff(k, f, nr))),
            pl.BlockSpec((1, d, bf), lambda k, f, st, se, r0, nr: (se[k], 0, f_eff(k, f, nr))),
            pl.BlockSpec((1, bf, d), lambda k, f, st, se, r0, nr: (se[k], f_eff(k, f, nr), 0)),
        ],
        out_specs=pl.BlockSpec((bm, d), lambda k, f, st, se, r0, nr: (st[k], 0), pipeline_mode=pl.Buffered(1)),
        scratch_shapes=[
            pltpu.VMEM((d, bf), BF16),
            pltpu.VMEM((d, bf), BF16),
            pltpu.VMEM((bf, d), BF16),
        ],
    )
    return pl.pallas_call(
        functools.partial(_ffn_kernel, sub=sub, nsub=bm // sub, ncol=FFN_NCOL),
        grid_spec=grid_spec,
        out_shape=jax.ShapeDtypeStruct((rows, d), F32),
        compiler_params=_params("arbitrary", "arbitrary"),
        name="swiglu_ffn",
    )(seg_tile, seg_expert, seg_row0, seg_rows, h, w_gate, w_up, w_down)


def _sb_kernel(q_ref, k_ref, v_ref, tri_ref, o_ref, *, bq, nh, scale):
    i = pl.program_id(2)
    dh = HEAD_DIM
    qs = [q_ref[:, h * dh:(h + 1) * dh] for h in range(nh)]

    def tile(j, state, diag):
        carry, acc = state
        k0 = pl.multiple_of(j * bq, bq)
        ks = [k_ref[pl.ds(k0, bq), h * dh:(h + 1) * dh] for h in range(nh)]
        vs = [v_ref[pl.ds(k0, bq), h * dh:(h + 1) * dh] for h in range(nh)]
        z = jnp.concatenate([_dot_nt(qs[h], ks[h]) for h in range(nh)], axis=0) * scale
        log_beta = jnp.minimum(z, 0.0) - jnp.log(1.0 + jnp.exp(-jnp.abs(z)))
        log_keep = log_beta - z
        if diag:
            row = lax.broadcasted_iota(jnp.int32, (bq, bq), 0)
            col = lax.broadcasted_iota(jnp.int32, (bq, bq), 1)
            causal = jnp.concatenate([jnp.where(col < row, 1.0, 0.0)] * nh, axis=0) > 0.5
            log_keep = jnp.where(causal, log_keep, 0.0)
        hi, lo = _split_bf16(log_keep)
        between = _dot(jnp.concatenate([hi, lo], axis=1), tri_ref[...]) + carry
        w = jnp.exp(log_beta + between)
        if diag:
            w = jnp.where(causal, w, 0.0)
        wb = w.astype(BF16)
        pv = jnp.concatenate([_dot(wb[h * bq:(h + 1) * bq, :], vs[h]) for h in range(nh)], axis=0)
        return carry + jnp.sum(log_keep, axis=-1, keepdims=True), acc + pv

    state = tile(i, (jnp.zeros((nh * bq, 1), F32), jnp.zeros((nh * bq, dh), F32)), True)

    def cond(c):
        return jnp.logical_and(c[0] < i, c[1] > SB_EXP_ZERO)

    def body(c):
        st = tile(i - 1 - c[0], c[2], False)
        return c[0] + 1, jnp.max(st[0]), st

    _, _, state = lax.while_loop(cond, body, (jnp.int32(0), jnp.max(state[0]), state))
    o_ref[...] = jnp.concatenate([state[1][h * bq:(h + 1) * bq, :] for h in range(nh)],
                                 axis=1).astype(o_ref.dtype)


def _sb_attention(qkv, batch, seq, heads):
    t = qkv.shape[0]
    bq = min(SB_BQ, seq)
    nq = seq // bq
    nh = SB_HEADS_PER_STEP
    hw = nh * HEAD_DIM
    ng = heads // nh
    rj = lax.broadcasted_iota(jnp.int32, (2 * bq, bq), 0)
    cs = lax.broadcasted_iota(jnp.int32, (2 * bq, bq), 1)
    tri = jnp.where((rj % bq) > cs, 1.0, 0.0).astype(BF16)
    return pl.pallas_call(
        functools.partial(_sb_kernel, bq=bq, nh=nh, scale=HEAD_DIM ** -0.5),
        grid=(batch, ng, nq),
        in_specs=[
            pl.BlockSpec((bq, hw), lambda b, h, i: (b * nq + i, h)),
            pl.BlockSpec((seq, hw), lambda b, h, i: (b, ng + h)),
            pl.BlockSpec((seq, hw), lambda b, h, i: (b, 2 * ng + h)),
            pl.BlockSpec((2 * bq, bq), lambda b, h, i: (0, 0)),
        ],
        out_specs=pl.BlockSpec((bq, hw), lambda b, h, i: (b * nq + i, h)),
        out_shape=jax.ShapeDtypeStruct((t, heads * HEAD_DIM), BF16),
        compiler_params=_params("arbitrary", "arbitrary", "arbitrary"),
        name="stickbreak_attn",
    )(qkv, qkv, qkv, tri)


def _rglru_kernel(x_ref, g_ref, cw_ref, cb_ref, wa_ref, ba_ref, wx_ref, bx_ref, lam_ref, o_ref, xp_ref,
                  *, seq, cb, tc, kw):
    pad = SUBLANE
    xp_ref[0:pad, :] = jnp.zeros((pad, cb), F32)
    xp_ref[pad:, :] = x_ref[...]
    log_lam = _log_sigmoid(lam_ref[...])
    sub_iota = lax.broadcasted_iota(jnp.int32, (SUBLANE, cb), 0)
    ngroup = cb // LANE

    def chunk(ci, h):
        t0 = pl.multiple_of(ci * tc, tc)
        win = xp_ref[pl.ds(t0, tc + pad), :]
        xc = cb_ref[...] + cw_ref[0:1, :] * win[pad - kw + 1:pad - kw + 1 + tc, :]
        for k in range(1, kw):
            off = pad - kw + 1 + k
            xc = xc + cw_ref[k:k + 1, :] * win[off:off + tc, :]
        xcb = xc.astype(BF16)
        ra = jnp.concatenate(
            [_dot(xcb[:, q * LANE:(q + 1) * LANE], wa_ref[q].astype(BF16)) for q in range(ngroup)], axis=1)
        rx = jnp.concatenate(
            [_dot(xcb[:, q * LANE:(q + 1) * LANE], wx_ref[q].astype(BF16)) for q in range(ngroup)], axis=1)
        r = _sigmoid(ra + ba_ref[...])
        gi = _sigmoid(rx + bx_ref[...])
        log_a = RG_C * r * log_lam
        a = jnp.exp(log_a)
        u = jnp.sqrt(-jnp.tanh(log_a) * (a * a + 1.0)) * (gi * xc)
        gate = _gelu_tanh(g_ref[pl.ds(t0, tc), :])
        outs = []
        for gidx in range(tc // SUBLANE):
            av = a[gidx * SUBLANE:(gidx + 1) * SUBLANE, :]
            bv = u[gidx * SUBLANE:(gidx + 1) * SUBLANE, :]
            for sh in (1, 2, 4):
                a_s = pltpu.roll(av, sh, axis=0)
                b_s = pltpu.roll(bv, sh, axis=0)
                m = sub_iota >= sh
                bv = jnp.where(m, av * b_s + bv, bv)
                av = jnp.where(m, av * a_s, av)
            hv = av * h + bv
            outs.append(hv)
            h = jnp.broadcast_to(hv[SUBLANE - 1:SUBLANE, :], (SUBLANE, cb))
        hs = jnp.concatenate(outs, axis=0)
        o_ref[pl.ds(t0, tc), :] = (hs * gate).astype(o_ref.dtype)
        return h

    lax.fori_loop(0, seq // tc, chunk, jnp.zeros((SUBLANE, cb), F32))


def _block_diag_pairs(w):
    nblk, c, _ = w.shape
    w2 = w.reshape(nblk // 2, 2, c, c)
    z = jnp.zeros((nblk // 2, c, c), w.dtype)
    top = jnp.concatenate([w2[:, 0], z], axis=2)
    bot = jnp.concatenate([z, w2[:, 1]], axis=2)
    return jnp.concatenate([top, bot], axis=1)


def _rglru(rg, batch, seq, conv_w, conv_b, wa, ba, wx, bx, lam):
    t, c2 = rg.shape
    c = c2 // 2
    cb = min(RG_CB, c)
    tc = min(RG_TC, seq)
    ncb = c // cb
    kw = conv_w.shape[0]
    gpb = cb // LANE
    vec = pl.BlockSpec((1, cb), lambda b, j: (0, j))
    return pl.pallas_call(
        functools.partial(_rglru_kernel, seq=seq, cb=cb, tc=tc, kw=kw),
        grid=(batch, ncb),
        in_specs=[
            pl.BlockSpec((seq, cb), lambda b, j: (b, j)),
            pl.BlockSpec((seq, cb), lambda b, j: (b, ncb + j)),
            pl.BlockSpec((kw, cb), lambda b, j: (0, j)),
            vec,
            pl.BlockSpec((gpb, LANE, LANE), lambda b, j: (j, 0, 0)),
            vec,
            pl.BlockSpec((gpb, LANE, LANE), lambda b, j: (j, 0, 0)),
            vec,
            vec,
        ],
        out_specs=pl.BlockSpec((seq, cb), lambda b, j: (b, j)),
        out_shape=jax.ShapeDtypeStruct((t, c), BF16),
        scratch_shapes=[pltpu.VMEM((seq + SUBLANE, cb), F32)],
        compiler_params=_params("arbitrary", "arbitrary"),
        name="rglru",
    )(rg, rg, conv_w, conv_b.reshape(1, c), _block_diag_pairs(wa), ba.reshape(1, c),
      _block_diag_pairs(wx), bx.reshape(1, c), lam.reshape(1, c))


def _compress_kernel(x_ref, pos_ref, w1_ref, w2_ref, o_ref, *, ngrp):
    st = CMP_STRIDE
    half = st * HEAD_DIM
    xs = [x_ref[pl.ds(r, ngrp, stride=st), :] for r in range(st)]
    pos = pos_ref[0]
    x0 = jnp.concatenate([xs[r] + pos[r:r + 1, :] for r in range(st)], axis=1).astype(BF16)
    x1 = jnp.concatenate([xs[r] + pos[st + r:st + r + 1, :] for r in range(st)], axis=1).astype(BF16)
    p0 = _dot(x0, w1_ref[0, 0:half, :].astype(BF16))
    p1 = _dot(x1, w1_ref[0, half:2 * half, :].astype(BF16))
    pre = p0 + pltpu.roll(p1, ngrp - 1, axis=0)
    out = _dot(_gelu_tanh(pre).astype(BF16), w2_ref[0].astype(BF16))
    rown = lax.broadcasted_iota(jnp.int32, out.shape, 0)
    o_ref[...] = jnp.where(rown < ngrp - 1, out, 0.0).astype(o_ref.dtype)


def _compress(kcvc, batch, seq, pos, w1, w2):
    assert CMP_BLOCK == 2 * CMP_STRIDE
    g2 = kcvc.shape[1] // HEAD_DIM
    per = g2 // 2
    ngrp = seq // CMP_STRIDE
    return pl.pallas_call(
        functools.partial(_compress_kernel, ngrp=ngrp),
        grid=(batch, g2),
        in_specs=[
            pl.BlockSpec((seq, HEAD_DIM), lambda b, j: (b, j)),
            pl.BlockSpec((1, CMP_BLOCK, HEAD_DIM), lambda b, j: (j // per, 0, 0)),
            pl.BlockSpec((1, CMP_BLOCK * HEAD_DIM, HEAD_DIM), lambda b, j: (j // per, 0, 0)),
            pl.BlockSpec((1, HEAD_DIM, HEAD_DIM), lambda b, j: (j // per, 0, 0)),
        ],
        out_specs=pl.BlockSpec((ngrp, HEAD_DIM), lambda b, j: (b * g2 + j, 0)),
        out_shape=jax.ShapeDtypeStruct((batch * g2 * ngrp, HEAD_DIM), BF16),
        compiler_params=_params("arbitrary", "arbitrary"),
        name="nsa_compress",
    )(kcvc, pos, w1, w2)


def _softmax_parts(s2, bias):
    sb = s2 + bias
    m = jnp.max(sb, axis=-1, keepdims=True)
    e = jnp.exp2(sb - m)
    return e, jnp.sum(e, axis=-1, keepdims=True)


def _nsa_kernel(q_ref, kc_ref, vc_ref, ks_ref, vs_ref, kw_ref, vw_ref, g_ref, ov_ref, ex_ref, o_ref,
                acc_ref, inv_ref, *, seq, hg, ng, n_slc, scale):
    bq, bk, dh = NSA_BQ, NSA_BK, HEAD_DIM
    i = pl.program_id(2)
    q0 = i * bq
    rg = hg * bq
    qb = q_ref[...]
    qs = [jnp.concatenate([qb[:, (g * hg + h) * dh:(g * hg + h + 1) * dh] for h in range(hg)], axis=0)
          for g in range(ng)]
    kv = lambda ref, g: ref[:, g * dh:(g + 1) * dh]
    grp = lambda a, g: a[g * rg:(g + 1) * rg, :]
    stack = lambda parts: jnp.concatenate(parts, axis=0)
    tile_heads = lambda a: jnp.concatenate([a] * hg, axis=0)
    tile_rows = lambda a: jnp.concatenate([a] * (ng * hg), axis=0)
    qpos1 = q0 + lax.broadcasted_iota(jnp.int32, (bq, 1), 0)
    qpos = tile_rows(qpos1)

    scale2 = scale * LOG2E

    ncmp = kc_ref.shape[0] // ng
    s_c = stack([_dot_nt(qs[g], kc_ref[g * ncmp:(g + 1) * ncmp, :]) for g in range(ng)]) * scale2
    n_idx = lax.broadcasted_iota(jnp.int32, (1, ncmp), 1)
    bias_c = jnp.where(n_idx * CMP_STRIDE + (CMP_BLOCK - 1) <= qpos1, 0.0, NEG_BIG)
    e_c, l_c = _softmax_parts(s_c, tile_rows(bias_c))
    p_c = e_c * jnp.where(qpos >= CMP_BLOCK - 1, 1.0 / l_c, 0.0)
    p_cb = p_c.astype(BF16)
    o_c = stack([_dot(grp(p_cb, g), vc_ref[g * ncmp:(g + 1) * ncmp, :]) for g in range(ng)])

    overlap_t = ov_ref[...]
    blk = lax.broadcasted_iota(jnp.int32, (n_slc, bq), 0)
    qp_l = q0 + lax.broadcasted_iota(jnp.int32, (n_slc, bq), 1)
    cur = qp_l // SLC_BLOCK
    forced = (blk == 0) | (blk == cur) | (blk == cur - 1)
    valid = blk * SLC_BLOCK <= qp_l
    k_sel = min(N_SEL, n_slc)
    sels = []
    for g in range(ng):
        p_g = grp(p_c, g)
        p_sum = p_g[0:bq, :]
        for h in range(1, hg):
            p_sum = p_sum + p_g[h * bq:(h + 1) * bq, :]
        p_hi, p_lo = _split_bf16(p_sum)
        imp_t = (_dot_nt(overlap_t, p_hi) + _dot_nt(overlap_t, p_lo))[0:n_slc, :]
        rank = jnp.where(valid, imp_t + FORCE_BONUS * jnp.where(forced, 1.0, 0.0), -jnp.inf)
        ahead = jnp.zeros((n_slc, bq), F32)
        for jp in range(n_slc):
            other = rank[jp:jp + 1, :]
            beats = (other > rank) | ((other == rank) & (blk > jp))
            ahead = ahead + jnp.where(beats, 1.0, 0.0)
        sel_t = jnp.where(valid & (ahead < k_sel), 1.0, 0.0)
        sel_t = jnp.concatenate([sel_t, jnp.zeros((LANE - n_slc, bq), F32)], axis=0)
        sels.append(sel_t.T.astype(BF16))

    n_need = (q0 + bq + bk - 1) // bk
    for v in range(1, seq // bk + 1):
        nk = v * bk

        @pl.when(n_need == v)
        def _():
            s = stack([_dot_nt(qs[g], kv(ks_ref, g)[0:nk, :]) for g in range(ng)]) * scale2
            kpos = lax.broadcasted_iota(jnp.int32, (1, nk), 1)
            bias = stack([tile_heads(jnp.where((_dot(sels[g], ex_ref[:, 0:nk]) > 0.5) & (kpos <= qpos1),
                                               0.0, NEG_BIG)) for g in range(ng)])
            e_s, l_s = _softmax_parts(s, bias)
            e_sb = e_s.astype(BF16)
            acc_ref[...] = stack([_dot(grp(e_sb, g), kv(vs_ref, g)[0:nk, :]) for g in range(ng)])
            inv_ref[...] = 1.0 / l_s

    acc_s = acc_ref[...]
    inv_s = inv_ref[...]

    span = min(WINDOW + bq, seq)
    w0 = pl.multiple_of(jnp.maximum(jnp.minimum(q0 - WINDOW, seq - span), 0), bq)
    kwt = kw_ref[pl.ds(w0, span), :]
    vwt = vw_ref[pl.ds(w0, span), :]
    s_w = stack([_dot_nt(qs[g], kwt[:, g * dh:(g + 1) * dh]) for g in range(ng)]) * scale2
    kpos_w = w0 + lax.broadcasted_iota(jnp.int32, (1, span), 1)
    bias_w = jnp.where((kpos_w <= qpos1) & (kpos_w > qpos1 - WINDOW), 0.0, NEG_BIG)
    e_w, l_w = _softmax_parts(s_w, tile_rows(bias_w))
    e_wb = e_w.astype(BF16)
    acc_w = stack([_dot(grp(e_wb, g), vwt[:, g * dh:(g + 1) * dh]) for g in range(ng)])
    inv_w = 1.0 / l_w

    gates = _sigmoid(g_ref[...])
    outs = []
    for g in range(ng):
        for h in range(hg):
            r = slice(g * rg + h * bq, g * rg + (h + 1) * bq)
            c = g * LANE + h * N_BRANCH
            outs.append(gates[:, c:c + 1] * o_c[r, :] + (gates[:, c + 1:c + 2] * inv_s[r, :]) * acc_s[r, :]
                        + (gates[:, c + 2:c + 3] * inv_w[r, :]) * acc_w[r, :])
    o_ref[...] = jnp.concatenate(outs, axis=1).astype(o_ref.dtype)


def _nsa_attention(q, kv4, cmp, gate_logits, batch, seq):
    t, width = q.shape
    g = NSA_KV_HEADS
    hg = width // (g * HEAD_DIM)
    bq = NSA_BQ
    nb = seq // bq
    ncmp = seq // CMP_STRIDE
    n_slc = seq // SLC_BLOCK
    dh = HEAD_DIM
    ng = NSA_GROUPS_PER_STEP
    gs = g // ng
    full = lambda off: pl.BlockSpec((seq, ng * dh), lambda b, gi, i: (b, off * gs + gi))
    jj = lax.broadcasted_iota(jnp.int32, (LANE, ncmp), 0)
    cstart = lax.broadcasted_iota(jnp.int32, (LANE, ncmp), 1) * CMP_STRIDE
    overlap_t = jnp.where((cstart < (jj + 1) * SLC_BLOCK) & (cstart + CMP_BLOCK > jj * SLC_BLOCK)
                          & (jj < n_slc), 1.0, 0.0).astype(BF16)
    expand = jnp.where(lax.broadcasted_iota(jnp.int32, (LANE, seq), 1) // SLC_BLOCK
                       == lax.broadcasted_iota(jnp.int32, (LANE, seq), 0), 1.0, 0.0).astype(BF16)
    const = lambda shape: pl.BlockSpec(shape, lambda b, gi, i: (0, 0))
    return pl.pallas_call(
        functools.partial(_nsa_kernel, seq=seq, hg=hg, ng=ng, n_slc=n_slc, scale=dh ** -0.5),
        grid=(batch, gs, nb),
        in_specs=[
            pl.BlockSpec((bq, ng * hg * dh), lambda b, gi, i: (b * nb + i, gi)),
            pl.BlockSpec((ng * ncmp, dh), lambda b, gi, i: (b * 2 * gs + gi, 0)),
            pl.BlockSpec((ng * ncmp, dh), lambda b, gi, i: (b * 2 * gs + gs + gi, 0)),
            full(0), full(1), full(2), full(3),
            pl.BlockSpec((bq, ng * LANE), lambda b, gi, i: (b * nb + i, gi)),
            const((LANE, ncmp)),
            const((LANE, seq)),
        ],
        out_specs=pl.BlockSpec((bq, ng * hg * dh), lambda b, gi, i: (b * nb + i, gi)),
        out_shape=jax.ShapeDtypeStruct((t, width), BF16),
        scratch_shapes=[pltpu.VMEM((ng * hg * bq, dh), F32), pltpu.VMEM((ng * hg * bq, 1), F32)],
        compiler_params=_params("arbitrary", "arbitrary", "arbitrary"),
        name="nsa_attn",
    )(q, cmp, cmp, kv4, kv4, kv4, kv4, gate_logits, overlap_t, expand)


def _gather_kernel(lo_ref, hi_ref, src_ref, h_ref, o_ref, *, chunk):
    i = pl.program_id(0)
    rows = o_ref.shape[0]
    src = src_ref[...]
    lane = lax.broadcasted_iota(jnp.int32, (rows, chunk), 1)
    o_ref[...] = jnp.zeros(o_ref.shape, o_ref.dtype)

    def body(c, carry):
        c0 = pl.multiple_of(c * chunk, chunk)
        onehot = jnp.where(src - c0 == lane, 1.0, 0.0).astype(BF16)
        o_ref[...] += _dot(onehot, h_ref[pl.ds(c0, chunk), :]).astype(o_ref.dtype)
        return carry

    lax.fori_loop(lo_ref[i], hi_ref[i], body, 0)


def _gather_rows(h, src, n_rows):
    t, d = h.shape
    bt, chunk = GATHER_ROWS, GATHER_CHUNK
    nt = n_rows // bt
    src2 = src.reshape(nt, bt)
    live = src2 >= 0
    lo = jnp.min(jnp.where(live, src2, t), axis=1) // chunk
    hi = jnp.where(jnp.any(live, axis=1), jnp.max(src2, axis=1) // chunk + 1, lo)
    grid_spec = pltpu.PrefetchScalarGridSpec(
        num_scalar_prefetch=2,
        grid=(nt,),
        in_specs=[
            pl.BlockSpec((bt, 1), lambda i, lo, hi: (i, 0)),
            pl.BlockSpec((t, d), lambda i, lo, hi: (0, 0), pipeline_mode=pl.Buffered(1)),
        ],
        out_specs=pl.BlockSpec((bt, d), lambda i, lo, hi: (i, 0)),
    )
    return pl.pallas_call(
        functools.partial(_gather_kernel, chunk=chunk),
        grid_spec=grid_spec,
        out_shape=jax.ShapeDtypeStruct((n_rows, d), h.dtype),
        compiler_params=_params("arbitrary"),
        name="moe_dispatch",
    )(jnp.minimum(lo, hi).astype(jnp.int32), hi.astype(jnp.int32), src.reshape(n_rows, 1), h)


def _router_kernel(l_ref, o_ref, *, n_exp):
    x = l_ref[...]
    lane = lax.broadcasted_iota(jnp.int32, x.shape, 1)
    xm = jnp.where(lane < n_exp, x, -jnp.inf)
    v0 = jnp.max(xm, axis=-1, keepdims=True)
    i0 = jnp.min(jnp.where(xm == v0, lane, LANE), axis=-1, keepdims=True)
    xm = jnp.where(lane == i0, -jnp.inf, xm)
    v1 = jnp.max(xm, axis=-1, keepdims=True)
    i1 = jnp.min(jnp.where(xm == v1, lane, LANE), axis=-1, keepdims=True)
    e = jnp.exp(v1 - v0)
    p0 = 1.0 / (1.0 + e)
    out = jnp.where(lane == 0, p0, jnp.where(lane == 1, e * p0, jnp.where(
        lane == 2, i0.astype(F32), jnp.where(lane == 3, i1.astype(F32), 0.0))))
    o_ref[...] = out


def _router_top2(logits, n_exp):
    t = logits.shape[0]
    bt = ROW_TILE
    out = pl.pallas_call(
        functools.partial(_router_kernel, n_exp=n_exp),
        grid=(t // bt,),
        in_specs=[pl.BlockSpec((bt, LANE), lambda i: (i, 0))],
        out_specs=pl.BlockSpec((bt, LANE), lambda i: (i, 0)),
        out_shape=jax.ShapeDtypeStruct((t, LANE), F32),
        compiler_params=_params("arbitrary"),
        name="router_top2",
    )(logits)
    return out[:, 0:TOP_K], out[:, TOP_K:2 * TOP_K].astype(jnp.int32)


def _route(probs, top_i, n_experts):
    t = top_i.shape[0]
    flat_e = top_i.reshape(-1)
    onehot = (flat_e[:, None] == jnp.arange(n_experts)[None, :]).astype(jnp.int32)
    rank = jnp.take_along_axis(jnp.cumsum(onehot, axis=0), flat_e[:, None], axis=1)[:, 0] - 1
    counts = jnp.sum(onehot, axis=0)
    padded = (counts + FFN_SUB - 1) // FFN_SUB * FFN_SUB
    end = jnp.cumsum(padded)
    start = end - padded
    n_rows = -(-(TOP_K * t + n_experts * (FFN_SUB - 1)) // FFN_BM) * FFN_BM
    n_tiles = n_rows // FFN_BM
    dest = (start[flat_e] + rank).astype(jnp.int32)
    src = jnp.full((n_rows,), -1, jnp.int32).at[dest].set(jnp.arange(TOP_K * t, dtype=jnp.int32) // TOP_K)
    cuts = jnp.sort(jnp.concatenate([start, jnp.arange(n_tiles) * FFN_BM]))
    nxt = jnp.concatenate([cuts[1:], jnp.full((1,), n_rows, cuts.dtype)])
    seg_rows = jnp.maximum(jnp.minimum(nxt, end[-1]) - cuts, 0)
    seg_tile = jnp.minimum(cuts // FFN_BM, n_tiles - 1)
    seg_expert = jnp.minimum(jnp.searchsorted(end, cuts, side="right"), n_experts - 1)
    i32 = lambda a: a.astype(jnp.int32)
    return probs, dest.reshape(t, TOP_K), src, i32(seg_tile), i32(seg_expert), i32(cuts % FFN_BM), i32(seg_rows)


def kernel(x, c, ada_mix_w, ada_mix_b, norm_mix_g, ada_ffn_w, ada_ffn_b, norm_ffn_g, even_in_w, rg_conv_w, rg_conv_b, rg_wa, rg_ba, rg_wx, rg_bx, rg_lambda, even_out_w, dense_w_gate, dense_w_up, dense_w_down, nsa_in_w, cmp_pos_k, cmp_pos_v, cmp_k_w1, cmp_k_w2, cmp_v_w1, cmp_v_w2, nsa_out_w, router_w, moe_w_gate, moe_w_up, moe_w_down, final_norm_g):
    batch, seq, d = x.shape
    t = batch * seq
    depth = ada_mix_w.shape[0]
    xf = x.reshape(t, d)

    c_pad = jnp.pad(c, ((0, (-batch) % SUBLANE), (0, 0)))
    m_mix = _adaln(c_pad, ada_mix_w, ada_mix_b)
    m_ffn = _adaln(c_pad, ada_ffn_w, ada_ffn_b)

    def mods(m, layer):
        v = m[layer, :batch].reshape(batch, 1, 3, d)
        return v[:, :, 0], v[:, :, 1], v[:, :, 2]

    def norm_step(xf, pending, g, mod, out_dtype=BF16):
        if pending is not None and pending[0] == "experts":
            _, y, dest, probs, gate = pending
            return _combine_norm(xf, y, dest, probs, gate, g, seq, mod=mod, out_dtype=out_dtype)
        res = None if pending is None else pending[1:]
        return _normmod(xf, g, seq, res=res, mod=mod, out_dtype=out_dtype)

    pending = None
    for layer in range(depth):
        j = layer // 2
        shift, scale, gate = mods(m_mix, layer)
        xf, h = norm_step(xf, pending, norm_mix_g[layer], (shift, scale))
        if layer % 2 == 0:
            w_in = even_in_w[j]
            sbw = (w_in.shape[1] - 2 * rg_conv_w.shape[2]) // 3
            heads = sbw // HEAD_DIM
            qkv = _matmul([h], w_in, 0, 3 * sbw, BF16)
            rg = _matmul([h], w_in, 3 * sbw, w_in.shape[1] - 3 * sbw, F32)
            o_a = _sb_attention(qkv, batch, seq, heads)
            o_b = _rglru(rg, batch, seq, rg_conv_w[j], rg_conv_b[j], rg_wa[j], rg_ba[j], rg_wx[j],
                         rg_bx[j], rg_lambda[j])
            xf = _matmul([o_a, o_b], even_out_w[j], 0, d, F32, residual=(xf, gate, seq))
        else:
            w_in = nsa_in_w[j]
            g = NSA_KV_HEADS
            kvw = g * HEAD_DIM
            nsa_w = nsa_out_w.shape[1]
            hg = nsa_w // kvw
            q = _matmul([h], w_in, 0, nsa_w, BF16)
            kcvc = _matmul([h], w_in, nsa_w, 2 * kvw, F32)
            kv4 = _matmul([h], w_in, nsa_w + 2 * kvw, 4 * kvw, BF16)
            wg = w_in[:, nsa_w + 6 * kvw:].reshape(d, g, hg * N_BRANCH)
            wg = jnp.pad(wg, ((0, 0), (0, 0), (0, LANE - hg * N_BRANCH))).reshape(d, g * LANE)
            gl = _matmul([h], wg, 0, g * LANE, F32)
            cmp = _compress(kcvc, batch, seq, jnp.stack([cmp_pos_k[j], cmp_pos_v[j]]),
                            jnp.stack([cmp_k_w1[j], cmp_v_w1[j]]), jnp.stack([cmp_k_w2[j], cmp_v_w2[j]]))
            o = _nsa_attention(q, kv4, cmp, gl, batch, seq)
            xf = _matmul([o], nsa_out_w[j], 0, d, F32, residual=(xf, gate, seq))
        pending = None

        shift, scale, gate = mods(m_ffn, layer)
        xf, h = norm_step(xf, pending, norm_ffn_g[layer], (shift, scale))
        if layer % 2 == 0:
            nt = t // FFN_BM
            ffn = _ffn(h, dense_w_gate[j:j + 1], dense_w_up[j:j + 1], dense_w_down[j:j + 1],
                       jnp.arange(nt, dtype=jnp.int32), jnp.zeros((nt,), jnp.int32),
                       jnp.zeros((nt,), jnp.int32), jnp.full((nt,), FFN_BM, jnp.int32))
            pending = ("dense", ffn, gate)
        else:
            n_exp = router_w.shape[2]
            rw = jnp.pad(router_w[j], ((0, 0), (0, LANE - n_exp)))
            probs, top_i = _router_top2(_matmul([h], rw, 0, LANE, F32), n_exp)
            probs, dest, src, seg_tile, seg_expert, seg_row0, seg_rows = _route(probs, top_i, n_exp)
            h_sorted = _gather_rows(h, src, src.shape[0])
            y = _ffn(h_sorted, moe_w_gate[j], moe_w_up[j], moe_w_down[j], seg_tile, seg_expert, seg_row0,
                     seg_rows)
            pending = ("experts", y, dest, probs, gate)

    _, out = norm_step(xf, pending, final_norm_g, None, F32)
    return out.reshape(batch, seq, d)
```

```python
import functools

import jax
import jax.numpy as jnp
from jax import lax
from jax.experimental import pallas as pl
from jax.experimental.pallas import tpu as pltpu

F32 = jnp.float32
BF16 = jnp.bfloat16

LANE = 128
SUBLANE = 8
VMEM_LIMIT_BYTES = 58 * 1024 * 1024

HEAD_DIM = 128
EPS = 1e-6
RG_C = 8.0
NSA_KV_HEADS = 4
N_BRANCH = 3
CMP_BLOCK = 32
CMP_STRIDE = 16
SLC_BLOCK = 64
N_SEL = 8
WINDOW = 512
FORCE_BONUS = 1e6
TOP_K = 2
NEG_BIG = -1e30
LOG2E = 1.4426950408889634
SB_EXP_ZERO = -110.0

ROW_TILE = 512
COMBINE_ROWS = 256
MM_BM = 2048
MM_BN = 512
FFN_BM = 2048
FFN_SUB = 256
FFN_NCOL = 4
FFN_BF = 256
GATHER_ROWS = 512
GATHER_CHUNK = 512
SB_BQ = 256
SB_HEADS_PER_STEP = 4
NSA_BQ = 128
NSA_BK = 256
NSA_GROUPS_PER_STEP = 2
RG_CB = 256
RG_TC = 256


def _params(*sem):
    return pltpu.CompilerParams(dimension_semantics=sem, vmem_limit_bytes=VMEM_LIMIT_BYTES)


def _dot(a, b):
    return jnp.dot(a, b, preferred_element_type=F32)


def _dot_nt(a, b):
    return lax.dot_general(a, b, (((1,), (1,)), ((), ())), preferred_element_type=F32)


def _sigmoid(x):
    return 1.0 / (1.0 + jnp.exp(-x))


def _log_sigmoid(x):
    return jnp.minimum(x, 0.0) - jnp.log1p(jnp.exp(-jnp.abs(x)))


def _gelu_tanh(x):
    return 0.5 * x * (1.0 + jnp.tanh(0.7978845608028654 * (x + 0.044715 * (x * x * x))))


def _split_bf16(x):
    hi = x.astype(BF16)
    lo = (x - hi.astype(F32)).astype(BF16)
    return hi, lo


def _adaln_kernel(c_ref, w_ref, b_ref, o_ref):
    c = c_ref[...]
    s = (c * _sigmoid(c)).astype(BF16)
    o_ref[0] = _dot(s, w_ref[0].astype(BF16)) + b_ref[0]


def _adaln(c_pad, w, b, bn=1024):
    depth, d, n3 = w.shape
    rows = c_pad.shape[0]
    return pl.pallas_call(
        _adaln_kernel,
        grid=(depth, n3 // bn),
        in_specs=[
            pl.BlockSpec((rows, d), lambda l, j: (0, 0)),
            pl.BlockSpec((1, d, bn), lambda l, j: (l, 0, j)),
            pl.BlockSpec((1, 1, bn), lambda l, j: (l, 0, j)),
        ],
        out_specs=pl.BlockSpec((1, rows, bn), lambda l, j: (l, 0, j)),
        out_shape=jax.ShapeDtypeStruct((depth, rows, n3), F32),
        compiler_params=_params("arbitrary", "arbitrary"),
        name="adaln",
    )(c_pad, w, b.reshape(depth, 1, n3))


def _normmod_kernel(*refs, has_res, has_mod):
    it = iter(refs)
    x_ref = next(it)
    if has_res:
        y_ref, gate_ref = next(it), next(it)
    g_ref = next(it)
    if has_mod:
        shift_ref, scale_ref = next(it), next(it)
    if has_res:
        xo_ref = next(it)
    h_ref = next(it)

    x = x_ref[...]
    if has_res:
        x = x + gate_ref[0] * y_ref[...]
        xo_ref[...] = x
    ms = jnp.mean(x * x, axis=-1, keepdims=True)
    h = x * lax.rsqrt(ms + EPS) * g_ref[...]
    if has_mod:
        h = h * (1.0 + scale_ref[0]) + shift_ref[0]
    h_ref[...] = h.astype(h_ref.dtype)


def _normmod(x, g, seq, res=None, mod=None, out_dtype=BF16):
    t, d = x.shape
    bt = ROW_TILE
    per_b = seq // bt
    row = pl.BlockSpec((bt, d), lambda i: (i, 0))
    vec = pl.BlockSpec((1, 1, d), lambda i: (i // per_b, 0, 0))
    args, specs = [x], [row]
    if res is not None:
        args += [res[0], res[1]]
        specs += [row, vec]
    args.append(g.reshape(1, d))
    specs.append(pl.BlockSpec((1, d), lambda i: (0, 0)))
    if mod is not None:
        args += [mod[0], mod[1]]
        specs += [vec, vec]
    out_shape, out_specs = [], []
    if res is not None:
        out_shape.append(jax.ShapeDtypeStruct((t, d), F32))
        out_specs.append(row)
    out_shape.append(jax.ShapeDtypeStruct((t, d), out_dtype))
    out_specs.append(row)
    outs = pl.pallas_call(
        functools.partial(_normmod_kernel, has_res=res is not None, has_mod=mod is not None),
        grid=(t // bt,),
        in_specs=specs,
        out_specs=out_specs,
        out_shape=out_shape,
        compiler_params=_params("arbitrary"),
        name="normmod",
    )(*args)
    return outs if res is not None else (x, outs[0])


def _combine_norm_kernel(dest_ref, *refs, has_mod, bt, n_steps):
    it = iter(refs)
    x_ref, y_hbm, p_ref, gate_ref, g_ref = (next(it) for _ in range(5))
    if has_mod:
        shift_ref, scale_ref, xo_ref = next(it), next(it), next(it)
    h_ref, buf, sem = next(it), next(it), next(it)
    i = pl.program_id(0)

    def row_copy(src_row, slot, k, r):
        return pltpu.make_async_copy(y_hbm.at[pl.ds(src_row, 1), :], buf.at[slot, k, pl.ds(r, 1), :],
                                     sem.at[slot])

    def start_tile(tile, slot):
        base = tile * (bt * TOP_K)

        def body(r, carry):
            for k in range(TOP_K):
                row_copy(dest_ref[base + r * TOP_K + k], slot, k, r).start()
            return carry

        lax.fori_loop(0, bt, body, 0, unroll=8)

    @pl.when(i == 0)
    def _():
        start_tile(0, 0)

    @pl.when(i + 1 < n_steps)
    def _():
        start_tile(i + 1, (i + 1) % 2)

    slot = i % 2

    def wait_row(r, carry):
        for k in range(TOP_K):
            row_copy(0, slot, k, r).wait()
        return carry

    lax.fori_loop(0, bt, wait_row, 0, unroll=8)
    p = p_ref[...]
    y = p[:, 0:1] * buf[slot, 0]
    for k in range(1, TOP_K):
        y = y + p[:, k:k + 1] * buf[slot, k]
    x = x_ref[...] + gate_ref[0] * y
    ms = jnp.mean(x * x, axis=-1, keepdims=True)
    h = x * lax.rsqrt(ms + EPS) * g_ref[...]
    if has_mod:
        xo_ref[...] = x
        h = h * (1.0 + scale_ref[0]) + shift_ref[0]
    h_ref[...] = h.astype(h_ref.dtype)


def _combine_norm(x, y, dest, probs, gate, g, seq, mod=None, out_dtype=BF16):
    t, d = x.shape
    bt = COMBINE_ROWS
    n_steps = t // bt
    per_b = seq // bt
    row = pl.BlockSpec((bt, d), lambda i, dst: (i, 0))
    vec = pl.BlockSpec((1, 1, d), lambda i, dst: (i // per_b, 0, 0))
    args = [x, y, probs, gate, g.reshape(1, d)]
    specs = [row, pl.BlockSpec(memory_space=pl.ANY), pl.BlockSpec((bt, TOP_K), lambda i, dst: (i, 0)), vec,
             pl.BlockSpec((1, d), lambda i, dst: (0, 0))]
    out_shape, out_specs = [], []
    if mod is not None:
        args += [mod[0], mod[1]]
        specs += [vec, vec]
        out_shape.append(jax.ShapeDtypeStruct((t, d), F32))
        out_specs.append(row)
    out_shape.append(jax.ShapeDtypeStruct((t, d), out_dtype))
    out_specs.append(row)
    grid_spec = pltpu.PrefetchScalarGridSpec(
        num_scalar_prefetch=1,
        grid=(n_steps,),
        in_specs=specs,
        out_specs=out_specs,
        scratch_shapes=[pltpu.VMEM((2, TOP_K, bt, d), F32), pltpu.SemaphoreType.DMA((2,))],
    )
    outs = pl.pallas_call(
        functools.partial(_combine_norm_kernel, has_mod=mod is not None, bt=bt, n_steps=n_steps),
        grid_spec=grid_spec,
        out_shape=out_shape,
        compiler_params=_params("arbitrary"),
        name="moe_combine_norm",
    )(dest.reshape(-1), *args)
    return (outs[0], outs[1]) if mod is not None else (None, outs[0])


def _mm_kernel(*refs, ks, has_res):
    a_refs = refs[: len(ks)]
    rest = refs[len(ks):]
    w_ref = rest[0]
    if has_res:
        x_ref, gate_ref = rest[1], rest[2]
    o_ref, wb_ref = rest[-2], rest[-1]

    wb_ref[...] = w_ref[...].astype(BF16)

    acc = None
    k0 = 0
    for a_ref, k in zip(a_refs, ks):
        part = _dot(a_ref[...], wb_ref[k0:k0 + k, :])
        acc = part if acc is None else acc + part
        k0 += k
    if has_res:
        acc = x_ref[...] + gate_ref[0] * acc
    o_ref[...] = acc.astype(o_ref.dtype)


def _matmul(a_list, w, col0, n, out_dtype, residual=None):
    m = a_list[0].shape[0]
    ks = tuple(a.shape[1] for a in a_list)
    ktot = sum(ks)
    bm = min(MM_BM, m)
    bn = min(MM_BN, n)
    cb0 = col0 // bn
    args = list(a_list) + [w]
    specs = [pl.BlockSpec((bm, k), lambda j, i: (i, 0)) for k in ks]
    specs.append(pl.BlockSpec((ktot, bn), lambda j, i: (0, cb0 + j)))
    if residual is not None:
        x, gate, seq = residual
        bm = min(bm, seq)
        per_b = seq // bm
        specs[: len(ks)] = [pl.BlockSpec((bm, k), lambda j, i: (i, 0)) for k in ks]
        args += [x, gate]
        specs += [pl.BlockSpec((bm, bn), lambda j, i: (i, j)),
                  pl.BlockSpec((1, 1, bn), lambda j, i: (i // per_b, 0, j))]
    return pl.pallas_call(
        functools.partial(_mm_kernel, ks=ks, has_res=residual is not None),
        grid=(n // bn, m // bm),
        in_specs=specs,
        out_specs=pl.BlockSpec((bm, bn), lambda j, i: (i, j)),
        out_shape=jax.ShapeDtypeStruct((m, n), out_dtype),
        scratch_shapes=[pltpu.VMEM((ktot, bn), BF16)],
        compiler_params=_params("arbitrary", "arbitrary"),
        name="proj_matmul",
    )(*args)


def _ffn_kernel(st_ref, se_ref, r0_ref, nr_ref, h_ref, wg_ref, wu_ref, wd_ref, o_ref, wgb, wub, wdb,
                *, sub, nsub, ncol):
    del se_ref
    k = pl.program_id(0)
    f = pl.program_id(1)
    nr = nr_ref[k]
    r0 = pl.multiple_of(r0_ref[k], sub)
    bm, d = o_ref.shape
    nchunk = (nr + sub - 1) // sub
    first = jnp.logical_or(k == 0, st_ref[k] != st_ref[jnp.maximum(k - 1, 0)])

    @pl.when(jnp.logical_and(f == 0, first))
    def _():
        o_ref[...] = jnp.zeros((bm, d), F32)

    for c in range(1, nsub + 1):
        rows = c * sub

        @pl.when(nchunk == c)
        def _():
            h = h_ref[pl.ds(r0, rows), :]
            wgb[...] = wg_ref[0].astype(BF16)
            g = _dot(h, wgb[...])
            wub[...] = wu_ref[0].astype(BF16)
            u = _dot(h, wub[...])
            wdb[...] = wd_ref[0].astype(BF16)
            a = (g * _sigmoid(g) * u).astype(BF16)
            for n in range(ncol):
                cols = slice(n * (d // ncol), (n + 1) * (d // ncol))
                o_ref[pl.ds(r0, rows), cols] += _dot(a, wdb[:, cols])


def _ffn(h, w_gate, w_up, w_down, seg_tile, seg_expert, seg_row0, seg_rows):
    rows, d = h.shape
    _, _, dff = w_gate.shape
    bm, bf, sub = FFN_BM, FFN_BF, FFN_SUB
    nf = dff // bf
    nseg = seg_tile.shape[0]

    def f_eff(k, f, nr):
        return jnp.where(nr[k] > 0, f, nf - 1)

    grid_spec = pltpu.PrefetchScalarGridSpec(
        num_scalar_prefetch=4,
        grid=(nseg, nf),
        in_specs=[
            pl.BlockSpec((bm, d), lambda k, f, st, se, r0, nr: (st[k], 0)),
            pl.BlockSpec((1, d, bf), lambda k, f, st, se, r0, nr: (se[k], 0, f_eff(k, f, nr))),
            pl.BlockSpec((1, d, bf), lambda k, f, st, se, r0, nr: (se[k], 0, f_eff(k, f, nr))),
            pl.BlockSpec((1, bf, d), lambda k, f, st, se, r0, nr: (se[k], f_eff(k, f, nr), 0)),
        ],
        out_specs=pl.BlockSpec((bm, d), lambda k, f, st, se, r0, nr: (st[k], 0), pipeline_mode=pl.Buffered(1)),
        scratch_shapes=[
            pltpu.VMEM((d, bf), BF16),
            pltpu.VMEM((d, bf), BF16),
            pltpu.VMEM((bf, d), BF16),
        ],
    )
    return pl.pallas_call(
        functools.partial(_ffn_kernel, sub=sub, nsub=bm // sub, ncol=FFN_NCOL),
        grid_spec=grid_spec,
        out_shape=jax.ShapeDtypeStruct((rows, d), F32),
        compiler_params=_params("arbitrary", "arbitrary"),
        name="swiglu_ffn",
    )(seg_tile, seg_expert, seg_row0, seg_rows, h, w_gate, w_up, w_down)


def _sb_kernel(q_ref, k_ref, v_ref, tri_ref, o_ref, *, bq, nh, scale):
    i = pl.program_id(2)
    dh = HEAD_DIM
    qs = [q_ref[:, h * dh:(h + 1) * dh] for h in range(nh)]

    def tile(j, state, diag):
        carry, acc = state
        k0 = pl.multiple_of(j * bq, bq)
        ks = [k_ref[pl.ds(k0, bq), h * dh:(h + 1) * dh] for h in range(nh)]
        vs = [v_ref[pl.ds(k0, bq), h * dh:(h + 1) * dh] for h in range(nh)]
        z = jnp.concatenate([_dot_nt(qs[h], ks[h]) for h in range(nh)], axis=0) * scale
        log_beta = jnp.minimum(z, 0.0) - jnp.log(1.0 + jnp.exp(-jnp.abs(z)))
        log_keep = log_beta - z
        if diag:
            row = lax.broadcasted_iota(jnp.int32, (bq, bq), 0)
            col = lax.broadcasted_iota(jnp.int32, (bq, bq), 1)
            causal = jnp.concatenate([jnp.where(col < row, 1.0, 0.0)] * nh, axis=0) > 0.5
            log_keep = jnp.where(causal, log_keep, 0.0)
        hi, lo = _split_bf16(log_keep)
        between = _dot(jnp.concatenate([hi, lo], axis=1), tri_ref[...]) + carry
        w = jnp.exp(log_beta + between)
        if diag:
            w = jnp.where(causal, w, 0.0)
        wb = w.astype(BF16)
        pv = jnp.concatenate([_dot(wb[h * bq:(h + 1) * bq, :], vs[h]) for h in range(nh)], axis=0)
        return carry + jnp.sum(log_keep, axis=-1, keepdims=True), acc + pv

    state = tile(i, (jnp.zeros((nh * bq, 1), F32), jnp.zeros((nh * bq, dh), F32)), True)

    def cond(c):
        return jnp.logical_and(c[0] < i, c[1] > SB_EXP_ZERO)

    def body(c):
        st = tile(i - 1 - c[0], c[2], False)
        return c[0] + 1, jnp.max(st[0]), st

    _, _, state = lax.while_loop(cond, body, (jnp.int32(0), jnp.max(state[0]), state))
    o_ref[...] = jnp.concatenate([state[1][h * bq:(h + 1) * bq, :] for h in range(nh)],
                                 axis=1).astype(o_ref.dtype)


def _sb_attention(qkv, batch, seq, heads):
    t = qkv.shape[0]
    bq = min(SB_BQ, seq)
    nq = seq // bq
    nh = SB_HEADS_PER_STEP
    hw = nh * HEAD_DIM
    ng = heads // nh
    rj = lax.broadcasted_iota(jnp.int32, (2 * bq, bq), 0)
    cs = lax.broadcasted_iota(jnp.int32, (2 * bq, bq), 1)
    tri = jnp.where((rj % bq) > cs, 1.0, 0.0).astype(BF16)
    return pl.pallas_call(
        functools.partial(_sb_kernel, bq=bq, nh=nh, scale=HEAD_DIM ** -0.5),
        grid=(batch, ng, nq),
        in_specs=[
            pl.BlockSpec((bq, hw), lambda b, h, i: (b * nq + i, h)),
            pl.BlockSpec((seq, hw), lambda b, h, i: (b, ng + h)),
            pl.BlockSpec((seq, hw), lambda b, h, i: (b, 2 * ng + h)),
            pl.BlockSpec((2 * bq, bq), lambda b, h, i: (0, 0)),
        ],
        out_specs=pl.BlockSpec((bq, hw), lambda b, h, i: (b * nq + i, h)),
        out_shape=jax.ShapeDtypeStruct((t, heads * HEAD_DIM), BF16),
        compiler_params=_params("arbitrary", "arbitrary", "arbitrary"),
        name="stickbreak_attn",
    )(qkv, qkv, qkv, tri)


def _rglru_kernel(x_ref, g_ref, cw_ref, cb_ref, wa_ref, ba_ref, wx_ref, bx_ref, lam_ref, o_ref, xp_ref,
                  *, seq, cb, tc, kw):
    pad = SUBLANE
    xp_ref[0:pad, :] = jnp.zeros((pad, cb), F32)
    xp_ref[pad:, :] = x_ref[...]
    log_lam = _log_sigmoid(lam_ref[...])
    sub_iota = lax.broadcasted_iota(jnp.int32, (SUBLANE, cb), 0)
    ngroup = cb // LANE

    def chunk(ci, h):
        t0 = pl.multiple_of(ci * tc, tc)
        win = xp_ref[pl.ds(t0, tc + pad), :]
        xc = cb_ref[...] + cw_ref[0:1, :] * win[pad - kw + 1:pad - kw + 1 + tc, :]
        for k in range(1, kw):
            off = pad - kw + 1 + k
            xc = xc + cw_ref[k:k + 1, :] * win[off:off + tc, :]
        xcb = xc.astype(BF16)
        ra = jnp.concatenate(
            [_dot(xcb[:, q * LANE:(q + 1) * LANE], wa_ref[q].astype(BF16)) for q in range(ngroup)], axis=1)
        rx = jnp.concatenate(
            [_dot(xcb[:, q * LANE:(q + 1) * LANE], wx_ref[q].astype(BF16)) for q in range(ngroup)], axis=1)
        r = _sigmoid(ra + ba_ref[...])
        gi = _sigmoid(rx + bx_ref[...])
        log_a = RG_C * r * log_lam
        a = jnp.exp(log_a)
        u = jnp.sqrt(-jnp.tanh(log_a) * (a * a + 1.0)) * (gi * xc)
        gate = _gelu_tanh(g_ref[pl.ds(t0, tc), :])
        outs = []
        for gidx in range(tc // SUBLANE):
            av = a[gidx * SUBLANE:(gidx + 1) * SUBLANE, :]
            bv = u[gidx * SUBLANE:(gidx + 1) * SUBLANE, :]
            for sh in (1, 2, 4):
                a_s = pltpu.roll(av, sh, axis=0)
                b_s = pltpu.roll(bv, sh, axis=0)
                m = sub_iota >= sh
                bv = jnp.where(m, av * b_s + bv, bv)
                av = jnp.where(m, av * a_s, av)
            hv = av * h + bv
            outs.append(hv)
            h = jnp.broadcast_to(hv[SUBLANE - 1:SUBLANE, :], (SUBLANE, cb))
        hs = jnp.concatenate(outs, axis=0)
        o_ref[pl.ds(t0, tc), :] = (hs * gate).astype(o_ref.dtype)
        return h

    lax.fori_loop(0, seq // tc, chunk, jnp.zeros((SUBLANE, cb), F32))


def _block_diag_pairs(w):
    nblk, c, _ = w.shape
    w2 = w.reshape(nblk // 2, 2, c, c)
    z = jnp.zeros((nblk // 2, c, c), w.dtype)
    top = jnp.concatenate([w2[:, 0], z], axis=2)
    bot = jnp.concatenate([z, w2[:, 1]], axis=2)
    return jnp.concatenate([top, bot], axis=1)


def _rglru(rg, batch, seq, conv_w, conv_b, wa, ba, wx, bx, lam):
    t, c2 = rg.shape
    c = c2 // 2
    cb = min(RG_CB, c)
    tc = min(RG_TC, seq)
    ncb = c // cb
    kw = conv_w.shape[0]
    gpb = cb // LANE
    vec = pl.BlockSpec((1, cb), lambda b, j: (0, j))
    return pl.pallas_call(
        functools.partial(_rglru_kernel, seq=seq, cb=cb, tc=tc, kw=kw),
        grid=(batch, ncb),
        in_specs=[
            pl.BlockSpec((seq, cb), lambda b, j: (b, j)),
            pl.BlockSpec((seq, cb), lambda b, j: (b, ncb + j)),
            pl.BlockSpec((kw, cb), lambda b, j: (0, j)),
            vec,
            pl.BlockSpec((gpb, LANE, LANE), lambda b, j: (j, 0, 0)),
            vec,
            pl.BlockSpec((gpb, LANE, LANE), lambda b, j: (j, 0, 0)),
            vec,
            vec,
        ],
        out_specs=pl.BlockSpec((seq, cb), lambda b, j: (b, j)),
        out_shape=jax.ShapeDtypeStruct((t, c), BF16),
        scratch_shapes=[pltpu.VMEM((seq + SUBLANE, cb), F32)],
        compiler_params=_params("arbitrary", "arbitrary"),
        name="rglru",
    )(rg, rg, conv_w, conv_b.reshape(1, c), _block_diag_pairs(wa), ba.reshape(1, c),
      _block_diag_pairs(wx), bx.reshape(1, c), lam.reshape(1, c))


def _compress_kernel(x_ref, pos_ref, w1_ref, w2_ref, o_ref, *, ngrp):
    st = CMP_STRIDE
    half = st * HEAD_DIM
    xs = [x_ref[pl.ds(r, ngrp, stride=st), :] for r in range(st)]
    pos = pos_ref[0]
    x0 = jnp.concatenate([xs[r] + pos[r:r + 1, :] for r in range(st)], axis=1).astype(BF16)
    x1 = jnp.concatenate([xs[r] + pos[st + r:st + r + 1, :] for r in range(st)], axis=1).astype(BF16)
    p0 = _dot(x0, w1_ref[0, 0:half, :].astype(BF16))
    p1 = _dot(x1, w1_ref[0, half:2 * half, :].astype(BF16))
    pre = p0 + pltpu.roll(p1, ngrp - 1, axis=0)
    out = _dot(_gelu_tanh(pre).astype(BF16), w2_ref[0].astype(BF16))
    rown = lax.broadcasted_iota(jnp.int32, out.shape, 0)
    o_ref[...] = jnp.where(rown < ngrp - 1, out, 0.0).astype(o_ref.dtype)


def _compress(kcvc, batch, seq, pos, w1, w2):
    assert CMP_BLOCK == 2 * CMP_STRIDE
    g2 = kcvc.shape[1] // HEAD_DIM
    per = g2 // 2
    ngrp = seq // CMP_STRIDE
    return pl.pallas_call(
        functools.partial(_compress_kernel, ngrp=ngrp),
        grid=(batch, g2),
        in_specs=[
            pl.BlockSpec((seq, HEAD_DIM), lambda b, j: (b, j)),
            pl.BlockSpec((1, CMP_BLOCK, HEAD_DIM), lambda b, j: (j // per, 0, 0)),
            pl.BlockSpec((1, CMP_BLOCK * HEAD_DIM, HEAD_DIM), lambda b, j: (j // per, 0, 0)),
            pl.BlockSpec((1, HEAD_DIM, HEAD_DIM), lambda b, j: (j // per, 0, 0)),
        ],
        out_specs=pl.BlockSpec((ngrp, HEAD_DIM), lambda b, j: (b * g2 + j, 0)),
        out_shape=jax.ShapeDtypeStruct((batch * g2 * ngrp, HEAD_DIM), BF16),
        compiler_params=_params("arbitrary", "arbitrary"),
        name="nsa_compress",
    )(kcvc, pos, w1, w2)


def _softmax_parts(s2, bias):
    sb = s2 + bias
    m = jnp.max(sb, axis=-1, keepdims=True)
    e = jnp.exp2(sb - m)
    return e, jnp.sum(e, axis=-1, keepdims=True)


def _nsa_kernel(q_ref, kc_ref, vc_ref, ks_ref, vs_ref, kw_ref, vw_ref, g_ref, ov_ref, ex_ref, o_ref,
                acc_ref, inv_ref, *, seq, hg, ng, n_slc, scale):
    bq, bk, dh = NSA_BQ, NSA_BK, HEAD_DIM
    i = pl.program_id(2)
    q0 = i * bq
    rg = hg * bq
    qb = q_ref[...]
    qs = [jnp.concatenate([qb[:, (g * hg + h) * dh:(g * hg + h + 1) * dh] for h in range(hg)], axis=0)
          for g in range(ng)]
    kv = lambda ref, g: ref[:, g * dh:(g + 1) * dh]
    grp = lambda a, g: a[g * rg:(g + 1) * rg, :]
    stack = lambda parts: jnp.concatenate(parts, axis=0)
    tile_heads = lambda a: jnp.concatenate([a] * hg, axis=0)
    tile_rows = lambda a: jnp.concatenate([a] * (ng * hg), axis=0)
    qpos1 = q0 + lax.broadcasted_iota(jnp.int32, (bq, 1), 0)
    qpos = tile_rows(qpos1)

    scale2 = scale * LOG2E

    ncmp = kc_ref.shape[0] // ng
    s_c = stack([_dot_nt(qs[g], kc_ref[g * ncmp:(g + 1) * ncmp, :]) for g in range(ng)]) * scale2
    n_idx = lax.broadcasted_iota(jnp.int32, (1, ncmp), 1)
    bias_c = jnp.where(n_idx * CMP_STRIDE + (CMP_BLOCK - 1) <= qpos1, 0.0, NEG_BIG)
    e_c, l_c = _softmax_parts(s_c, tile_rows(bias_c))
    p_c = e_c * jnp.where(qpos >= CMP_BLOCK - 1, 1.0 / l_c, 0.0)
    p_cb = p_c.astype(BF16)
    o_c = stack([_dot(grp(p_cb, g), vc_ref[g * ncmp:(g + 1) * ncmp, :]) for g in range(ng)])

    overlap_t = ov_ref[...]
    blk = lax.broadcasted_iota(jnp.int32, (n_slc, bq), 0)
    qp_l = q0 + lax.broadcasted_iota(jnp.int32, (n_slc, bq), 1)
    cur = qp_l // SLC_BLOCK
    forced = (blk == 0) | (blk == cur) | (blk == cur - 1)
    valid = blk * SLC_BLOCK <= qp_l
    k_sel = min(N_SEL, n_slc)
    sels = []
    for g in range(ng):
        p_g = grp(p_c, g)
        p_sum = p_g[0:bq, :]
        for h in range(1, hg):
            p_sum = p_sum + p_g[h * bq:(h + 1) * bq, :]
        p_hi, p_lo = _split_bf16(p_sum)
        imp_t = (_dot_nt(overlap_t, p_hi) + _dot_nt(overlap_t, p_lo))[0:n_slc, :]
        rank = jnp.where(valid, imp_t + FORCE_BONUS * jnp.where(forced, 1.0, 0.0), -jnp.inf)
        ahead = jnp.zeros((n_slc, bq), F32)
        for jp in range(n_slc):
            other = rank[jp:jp + 1, :]
            beats = (other > rank) | ((other == rank) & (blk > jp))
            ahead = ahead + jnp.where(beats, 1.0, 0.0)
        sel_t = jnp.where(valid & (ahead < k_sel), 1.0, 0.0)
        sel_t = jnp.concatenate([sel_t, jnp.zeros((LANE - n_slc, bq), F32)], axis=0)
        sels.append(sel_t.T.astype(BF16))

    n_need = (q0 + bq + bk - 1) // bk
    for v in range(1, seq // bk + 1):
        nk = v * bk

        @pl.when(n_need == v)
        def _():
            s = stack([_dot_nt(qs[g], kv(ks_ref, g)[0:nk, :]) for g in range(ng)]) * scale2
            kpos = lax.broadcasted_iota(jnp.int32, (1, nk), 1)
            bias = stack([tile_heads(jnp.where((_dot(sels[g], ex_ref[:, 0:nk]) > 0.5) & (kpos <= qpos1),
                                               0.0, NEG_BIG)) for g in range(ng)])
            e_s, l_s = _softmax_parts(s, bias)
            e_sb = e_s.astype(BF16)
            acc_ref[...] = stack([_dot(grp(e_sb, g), kv(vs_ref, g)[0:nk, :]) for g in range(ng)])
            inv_ref[...] = 1.0 / l_s

    acc_s = acc_ref[...]
    inv_s = inv_ref[...]

    span = min(WINDOW + bq, seq)
    w0 = pl.multiple_of(jnp.maximum(jnp.minimum(q0 - WINDOW, seq - span), 0), bq)
    kwt = kw_ref[pl.ds(w0, span), :]
    vwt = vw_ref[pl.ds(w0, span), :]
    s_w = stack([_dot_nt(qs[g], kwt[:, g * dh:(g + 1) * dh]) for g in range(ng)]) * scale2
    kpos_w = w0 + lax.broadcasted_iota(jnp.int32, (1, span), 1)
    bias_w = jnp.where((kpos_w <= qpos1) & (kpos_w > qpos1 - WINDOW), 0.0, NEG_BIG)
    e_w, l_w = _softmax_parts(s_w, tile_rows(bias_w))
    e_wb = e_w.astype(BF16)
    acc_w = stack([_dot(grp(e_wb, g), vwt[:, g * dh:(g + 1) * dh]) for g in range(ng)])
    inv_w = 1.0 / l_w

    gates = _sigmoid(g_ref[...])
    outs = []
    for g in range(ng):
        for h in range(hg):
            r = slice(g * rg + h * bq, g * rg + (h + 1) * bq)
            c = g * LANE + h * N_BRANCH
            outs.append(gates[:, c:c + 1] * o_c[r, :] + (gates[:, c + 1:c + 2] * inv_s[r, :]) * acc_s[r, :]
                        + (gates[:, c + 2:c + 3] * inv_w[r, :]) * acc_w[r, :])
    o_ref[...] = jnp.concatenate(outs, axis=1).astype(o_ref.dtype)


def _nsa_attention(q, kv4, cmp, gate_logits, batch, seq):
    t, width = q.shape
    g = NSA_KV_HEADS
    hg = width // (g * HEAD_DIM)
    bq = NSA_BQ
    nb = seq // bq
    ncmp = seq // CMP_STRIDE
    n_slc = seq // SLC_BLOCK
    dh = HEAD_DIM
    ng = NSA_GROUPS_PER_STEP
    gs = g // ng
    full = lambda off: pl.BlockSpec((seq, ng * dh), lambda b, gi, i: (b, off * gs + gi))
    jj = lax.broadcasted_iota(jnp.int32, (LANE, ncmp), 0)
    cstart = lax.broadcasted_iota(jnp.int32, (LANE, ncmp), 1) * CMP_STRIDE
    overlap_t = jnp.where((cstart < (jj + 1) * SLC_BLOCK) & (cstart + CMP_BLOCK > jj * SLC_BLOCK)
                          & (jj < n_slc), 1.0, 0.0).astype(BF16)
    expand = jnp.where(lax.broadcasted_iota(jnp.int32, (LANE, seq), 1) // SLC_BLOCK
                       == lax.broadcasted_iota(jnp.int32, (LANE, seq), 0), 1.0, 0.0).astype(BF16)
    const = lambda shape: pl.BlockSpec(shape, lambda b, gi, i: (0, 0))
    return pl.pallas_call(
        functools.partial(_nsa_kernel, seq=seq, hg=hg, ng=ng, n_slc=n_slc, scale=dh ** -0.5),
        grid=(batch, gs, nb),
        in_specs=[
            pl.BlockSpec((bq, ng * hg * dh), lambda b, gi, i: (b * nb + i, gi)),
            pl.BlockSpec((ng * ncmp, dh), lambda b, gi, i: (b * 2 * gs + gi, 0)),
            pl.BlockSpec((ng * ncmp, dh), lambda b, gi, i: (b * 2 * gs + gs + gi, 0)),
            full(0), full(1), full(2), full(3),
            pl.BlockSpec((bq, ng * LANE), lambda b, gi, i: (b * nb + i, gi)),
            const((LANE, ncmp)),
            const((LANE, seq)),
        ],
        out_specs=pl.BlockSpec((bq, ng * hg * dh), lambda b, gi, i: (b * nb + i, gi)),
        out_shape=jax.ShapeDtypeStruct((t, width), BF16),
        scratch_shapes=[pltpu.VMEM((ng * hg * bq, dh), F32), pltpu.VMEM((ng * hg * bq, 1), F32)],
        compiler_params=_params("arbitrary", "arbitrary", "arbitrary"),
        name="nsa_attn",
    )(q, cmp, cmp, kv4, kv4, kv4, kv4, gate_logits, overlap_t, expand)


def _gather_kernel(lo_ref, hi_ref, src_ref, h_ref, o_ref, *, chunk):
    i = pl.program_id(0)
    rows = o_ref.shape[0]
    src = src_ref[...]
    lane = lax.broadcasted_iota(jnp.int32, (rows, chunk), 1)
    o_ref[...] = jnp.zeros(o_ref.shape, o_ref.dtype)

    def body(c, carry):
        c0 = pl.multiple_of(c * chunk, chunk)
        onehot = jnp.where(src - c0 == lane, 1.0, 0.0).astype(BF16)
        o_ref[...] += _dot(onehot, h_ref[pl.ds(c0, chunk), :]).astype(o_ref.dtype)
        return carry

    lax.fori_loop(lo_ref[i], hi_ref[i], body, 0)


def _gather_rows(h, src, n_rows):
    t, d = h.shape
    bt, chunk = GATHER_ROWS, GATHER_CHUNK
    nt = n_rows // bt
    src2 = src.reshape(nt, bt)
    live = src2 >= 0
    lo = jnp.min(jnp.where(live, src2, t), axis=1) // chunk
    hi = jnp.where(jnp.any(live, axis=1), jnp.max(src2, axis=1) // chunk + 1, lo)
    grid_spec = pltpu.PrefetchScalarGridSpec(
        num_scalar_prefetch=2,
        grid=(nt,),
        in_specs=[
            pl.BlockSpec((bt, 1), lambda i, lo, hi: (i, 0)),
            pl.BlockSpec((t, d), lambda i, lo, hi: (0, 0), pipeline_mode=pl.Buffered(1)),
        ],
        out_specs=pl.BlockSpec((bt, d), lambda i, lo, hi: (i, 0)),
    )
    return pl.pallas_call(
        functools.partial(_gather_kernel, chunk=chunk),
        grid_spec=grid_spec,
        out_shape=jax.ShapeDtypeStruct((n_rows, d), h.dtype),
        compiler_params=_params("arbitrary"),
        name="moe_dispatch",
    )(jnp.minimum(lo, hi).astype(jnp.int32), hi.astype(jnp.int32), src.reshape(n_rows, 1), h)


def _router_kernel(l_ref, o_ref, *, n_exp):
    x = l_ref[...]
    lane = lax.broadcasted_iota(jnp.int32, x.shape, 1)
    xm = jnp.where(lane < n_exp, x, -jnp.inf)
    v0 = jnp.max(xm, axis=-1, keepdims=True)
    i0 = jnp.min(jnp.where(xm == v0, lane, LANE), axis=-1, keepdims=True)
    xm = jnp.where(lane == i0, -jnp.inf, xm)
    v1 = jnp.max(xm, axis=-1, keepdims=True)
    i1 = jnp.min(jnp.where(xm == v1, lane, LANE), axis=-1, keepdims=True)
    e = jnp.exp(v1 - v0)
    p0 = 1.0 / (1.0 + e)
    out = jnp.where(lane == 0, p0, jnp.where(lane == 1, e * p0, jnp.where(
        lane == 2, i0.astype(F32), jnp.where(lane == 3, i1.astype(F32), 0.0))))
    o_ref[...] = out


def _router_top2(logits, n_exp):
    t = logits.shape[0]
    bt = ROW_TILE
    out = pl.pallas_call(
        functools.partial(_router_kernel, n_exp=n_exp),
        grid=(t // bt,),
        in_specs=[pl.BlockSpec((bt, LANE), lambda i: (i, 0))],
        out_specs=pl.BlockSpec((bt, LANE), lambda i: (i, 0)),
        out_shape=jax.ShapeDtypeStruct((t, LANE), F32),
        compiler_params=_params("arbitrary"),
        name="router_top2",
    )(logits)
    return out[:, 0:TOP_K], out[:, TOP_K:2 * TOP_K].astype(jnp.int32)


def _route(probs, top_i, n_experts):
    t = top_i.shape[0]
    flat_e = top_i.reshape(-1)
    onehot = (flat_e[:, None] == jnp.arange(n_experts)[None, :]).astype(jnp.int32)
    rank = jnp.take_along_axis(jnp.cumsum(onehot, axis=0), flat_e[:, None], axis=1)[:, 0] - 1
    counts = jnp.sum(onehot, axis=0)
    padded = (counts + FFN_SUB - 1) // FFN_SUB * FFN_SUB
    end = jnp.cumsum(padded)
    start = end - padded
    n_rows = -(-(TOP_K * t + n_experts * (FFN_SUB - 1)) // FFN_BM) * FFN_BM
    n_tiles = n_rows // FFN_BM
    dest = (start[flat_e] + rank).astype(jnp.int32)
    src = jnp.full((n_rows,), -1, jnp.int32).at[dest].set(jnp.arange(TOP_K * t, dtype=jnp.int32) // TOP_K)
    cuts = jnp.sort(jnp.concatenate([start, jnp.arange(n_tiles) * FFN_BM]))
    nxt = jnp.concatenate([cuts[1:], jnp.full((1,), n_rows, cuts.dtype)])
    seg_rows = jnp.maximum(jnp.minimum(nxt, end[-1]) - cuts, 0)
    seg_tile = jnp.minimum(cuts // FFN_BM, n_tiles - 1)
    seg_expert = jnp.minimum(jnp.searchsorted(end, cuts, side="right"), n_experts - 1)
    i32 = lambda a: a.astype(jnp.int32)
    return probs, dest.reshape(t, TOP_K), src, i32(seg_tile), i32(seg_expert), i32(cuts % FFN_BM), i32(seg_rows)


def kernel(x, c, ada_mix_w, ada_mix_b, norm_mix_g, ada_ffn_w, ada_ffn_b, norm_ffn_g, even_in_w, rg_conv_w, rg_conv_b, rg_wa, rg_ba, rg_wx, rg_bx, rg_lambda, even_out_w, dense_w_gate, dense_w_up, dense_w_down, nsa_in_w, cmp_pos_k, cmp_pos_v, cmp_k_w1, cmp_k_w2, cmp_v_w1, cmp_v_w2, nsa_out_w, router_w, moe_w_gate, moe_w_up, moe_w_down, final_norm_g):
    batch, seq, d = x.shape
    t = batch * seq
    depth = ada_mix_w.shape[0]
    xf = x.reshape(t, d)

    c_pad = jnp.pad(c, ((0, (-batch) % SUBLANE), (0, 0)))
    m_mix = _adaln(c_pad, ada_mix_w, ada_mix_b)
    m_ffn = _adaln(c_pad, ada_ffn_w, ada_ffn_b)

    def mods(m, layer):
        v = m[layer, :batch].reshape(batch, 1, 3, d)
        return v[:, :, 0], v[:, :, 1], v[:, :, 2]

    def norm_step(xf, pending, g, mod, out_dtype=BF16):
        if pending is not None and pending[0] == "experts":
            _, y, dest, probs, gate = pending
            return _combine_norm(xf, y, dest, probs, gate, g, seq, mod=mod, out_dtype=out_dtype)
        res = None if pending is None else pending[1:]
        return _normmod(xf, g, seq, res=res, mod=mod, out_dtype=out_dtype)

    pending = None
    for layer in range(depth):
        j = layer // 2
        shift, scale, gate = mods(m_mix, layer)
        xf, h = norm_step(xf, pending, norm_mix_g[layer], (shift, scale))
        if layer % 2 == 0:
            w_in = even_in_w[j]
            sbw = (w_in.shape[1] - 2 * rg_conv_w.shape[2]) // 3
            heads = sbw // HEAD_DIM
            qkv = _matmul([h], w_in, 0, 3 * sbw, BF16)
            rg = _matmul([h], w_in, 3 * sbw, w_in.shape[1] - 3 * sbw, F32)
            o_a = _sb_attention(qkv, batch, seq, heads)
            o_b = _rglru(rg, batch, seq, rg_conv_w[j], rg_conv_b[j], rg_wa[j], rg_ba[j], rg_wx[j],
                         rg_bx[j], rg_lambda[j])
            xf = _matmul([o_a, o_b], even_out_w[j], 0, d, F32, residual=(xf, gate, seq))
        else:
            w_in = nsa_in_w[j]
            g = NSA_KV_HEADS
            kvw = g * HEAD_DIM
            nsa_w = nsa_out_w.shape[1]
            hg = nsa_w // kvw
            q = _matmul([h], w_in, 0, nsa_w, BF16)
            kcvc = _matmul([h], w_in, nsa_w, 2 * kvw, F32)
            kv4 = _matmul([h], w_in, nsa_w + 2 * kvw, 4 * kvw, BF16)
            wg = w_in[:, nsa_w + 6 * kvw:].reshape(d, g, hg * N_BRANCH)
            wg = jnp.pad(wg, ((0, 0), (0, 0), (0, LANE - hg * N_BRANCH))).reshape(d, g * LANE)
            gl = _matmul([h], wg, 0, g * LANE, F32)
            cmp = _compress(kcvc, batch, seq, jnp.stack([cmp_pos_k[j], cmp_pos_v[j]]),
                            jnp.stack([cmp_k_w1[j], cmp_v_w1[j]]), jnp.stack([cmp_k_w2[j], cmp_v_w2[j]]))
            o = _nsa_attention(q, kv4, cmp, gl, batch, seq)
            xf = _matmul([o], nsa_out_w[j], 0, d, F32, residual=(xf, gate, seq))
        pending = None

        shift, scale, gate = mods(m_ffn, layer)
        xf, h = norm_step(xf, pending, norm_ffn_g[layer], (shift, scale))
        if layer % 2 == 0:
            nt = t // FFN_BM
            ffn = _ffn(h, dense_w_gate[j:j + 1], dense_w_up[j:j + 1], dense_w_down[j:j + 1],
                       jnp.arange(nt, dtype=jnp.int32), jnp.zeros((nt,), jnp.int32),
                       jnp.zeros((nt,), jnp.int32), jnp.full((nt,), FFN_BM, jnp.int32))
            pending = ("dense", ffn, gate)
        else:
            n_exp = router_w.shape[2]
            rw = jnp.pad(router_w[j], ((0, 0), (0, LANE - n_exp)))
            probs, top_i = _router_top2(_matmul([h], rw, 0, LANE, F32), n_exp)
            probs, dest, src, seg_tile, seg_expert, seg_row0, seg_rows = _route(probs, top_i, n_exp)
            h_sorted = _gather_rows(h, src, src.shape[0])
            y = _ffn(h_sorted, moe_w_gate[j], moe_w_up[j], moe_w_down[j], seg_tile, seg_expert, seg_row0,
                     seg_rows)
            pending = ("experts", y, dest, probs, gate)

    _, out = norm_step(xf, pending, final_norm_g, None, F32)
    return out.reshape(batch, seq, d)
```
